```python
import math
import jax, jax.numpy as jnp
from jax import lax
import numpy as np

D_MODEL = 2048
BATCH = 16
SEQ = 256
DEPTH = 4
DEC_BATCH = 2
DEC_SEQ = 1024
PAST_LEN = 256

GRID_W = 64
N_MIXERS = 4
Q_BLOCK = 128
ROPE_BASE = 10000.0
LN_EPS = 1e-5
RMS_EPS = 1e-6
NEG_INF = -1e30
DEEPNORM_ALPHA = (2 * DEPTH) ** 0.25
DEEPNORM_BETA = (8 * DEPTH) ** -0.25

MLA_HEADS = 16
MLA_Q_LORA = 512
MLA_KV_LORA = 512
MLA_NOPE = 128
MLA_ROPE = 64
MLA_V = 128
MLA_SCALE = (MLA_NOPE + MLA_ROPE) ** -0.5

GQA_HEADS = 32
GQA_KV_HEADS = 8
GQA_HEAD_DIM = 64
WINDOW = 128
BAND_BLOCK = 128
GQA_SCALE = GQA_HEAD_DIM ** -0.5

FNET_GROUPS = 4
FNET_GROUP_DIM = D_MODEL // FNET_GROUPS

CONV_WIDTH = 3

PEER_HEADS = 8
PEER_N_KEYS = 128
PEER_N_EXPERTS = PEER_N_KEYS * PEER_N_KEYS
PEER_QUERY_DIM = 256
PEER_HALF = PEER_QUERY_DIM // 2
PEER_TOPK = 16
TOKEN_BLOCK = 128

kernel_name = 'hybrid_diffusion_mla_swa_fnet_conv_peer_step'


def layer_norm(x, g, b):
    xf = x.astype(jnp.float32)
    mu = xf.mean(-1, keepdims=True)
    var = jnp.square(xf - mu).mean(-1, keepdims=True)
    return ((xf - mu) * lax.rsqrt(var + LN_EPS) * g.astype(jnp.float32) + b.astype(jnp.float32)).astype(x.dtype)


def rms_norm(x, g):
    xf = x.astype(jnp.float32)
    return (xf * lax.rsqrt(jnp.square(xf).mean(-1, keepdims=True) + RMS_EPS) * g.astype(jnp.float32)).astype(x.dtype)


def ada_modulation(cond, w, b):
    return jnp.split(jax.nn.silu(cond) @ w + b, 6, axis=-1)


def modulate(x, shift, scale):
    return x * (1 + scale[:, None, :]) + shift[:, None, :]


def post_norm(x, delta, gate, g, b):
    return layer_norm(DEEPNORM_ALPHA * x + gate[:, None, :] * delta, g, b)


def axial_rope(x):
    T, R = x.shape[1], x.shape[-1]
    rows = T // GRID_W
    row_id = jnp.repeat(jnp.arange(rows), GRID_W)
    col_id = jnp.tile(jnp.arange(GRID_W), rows)
    half = R // 2
    quarter = half // 2
    inv_freq = ROPE_BASE ** (-jnp.arange(quarter, dtype=jnp.float32) / quarter)

    def rot(xa, pos):
        ang = pos.astype(jnp.float32)[:, None] * inv_freq[None, :]
        cos = jnp.cos(ang)[None, :, None, :]
        sin = jnp.sin(ang)[None, :, None, :]
        x1, x2 = xa[..., :quarter], xa[..., quarter:]
        return jnp.concatenate([x1 * cos - x2 * sin, x1 * sin + x2 * cos], -1)

    xf = x.astype(jnp.float32)
    return jnp.concatenate([rot(xf[..., :half], row_id), rot(xf[..., half:], col_id)], -1).astype(x.dtype)


def block_attention(q, k, v, scale, sink=None):
    B, T, H, dq = q.shape
    Hk, dv = k.shape[2], v.shape[-1]
    G = H // Hk
    nb = T // Q_BLOCK
    qb = q.reshape(B, nb, Q_BLOCK, Hk, G, dq).transpose(1, 0, 2, 3, 4, 5)

    def one_block(qi):
        s = jnp.einsum('bqkgd,bskd->bkgqs', qi, k, preferred_element_type=jnp.float32) * scale
        if sink is not None:
            s_sink = jnp.broadcast_to(sink.reshape(Hk, G)[None, :, :, None, None].astype(jnp.float32), s.shape[:-1] + (1,))
            p = jax.nn.softmax(jnp.concatenate([s, s_sink], -1), axis=-1)[..., :-1]
        else:
            p = jax.nn.softmax(s, axis=-1)
        return jnp.einsum('bkgqs,bskd->bqkgd', p.astype(v.dtype), v)

    o = lax.map(one_block, qb)
    return o.transpose(1, 0, 2, 3, 4, 5).reshape(B, T, H, dv)


def banded_window_attention(q, k, v, k_ctx, v_ctx, sink, scale):
    B, T, H, d = q.shape
    Hk = k.shape[2]
    G = H // Hk
    bb = BAND_BLOCK
    nb = T // bb
    qb = q.reshape(B, nb, bb, Hk, G, d).transpose(1, 0, 2, 3, 4, 5)
    pad = ((0, 0), (bb, bb), (0, 0), (0, 0))
    kp = jnp.pad(k, pad).reshape(B, nb + 2, bb, Hk, d)
    vp = jnp.pad(v, pad).reshape(B, nb + 2, bb, Hk, d)
    kband = jnp.concatenate([kp[:, :-2], kp[:, 1:-1], kp[:, 2:]], axis=2).transpose(1, 0, 2, 3, 4)
    vband = jnp.concatenate([vp[:, :-2], vp[:, 1:-1], vp[:, 2:]], axis=2).transpose(1, 0, 2, 3, 4)
    blk = jnp.arange(nb)
    q_pos = blk[:, None] * bb + jnp.arange(bb)[None, :]
    k_pos = (blk[:, None] - 1) * bb + jnp.arange(3 * bb)[None, :]
    valid = ((jnp.abs(q_pos[:, :, None] - k_pos[:, None, :]) <= WINDOW)
             & (k_pos[:, None, :] >= 0) & (k_pos[:, None, :] < T))
    sink_l = sink.reshape(Hk, G)[None, :, :, None, None].astype(jnp.float32)
    n_band = 3 * bb

    def one_block(args):
        qi, kb, vb, ok = args
        s_band = jnp.einsum('bqkgd,bskd->bkgqs', qi, kb, preferred_element_type=jnp.float32) * scale
        s_band = jnp.where(ok[None, None, None], s_band, NEG_INF)
        s_ctx = jnp.einsum('bqkgd,bskd->bkgqs', qi, k_ctx, preferred_element_type=jnp.float32) * scale
        s_sink = jnp.broadcast_to(sink_l, s_band.shape[:-1] + (1,))
        p = jax.nn.softmax(jnp.concatenate([s_band, s_ctx, s_sink], -1), axis=-1).astype(v.dtype)
        return (jnp.einsum('bkgqs,bskd->bqkgd', p[..., :n_band], vb)
                + jnp.einsum('bkgqs,bskd->bqkgd', p[..., n_band:-1], v_ctx))

    o = lax.map(one_block, (qb, kband, vband, valid))
    return o.transpose(1, 0, 2, 3, 4, 5).reshape(B, T, H, d)


def mla_queries(u, w_dq, q_norm, w_uq):
    B, T, _ = u.shape
    q = (rms_norm(u @ w_dq, q_norm) @ w_uq).reshape(B, T, MLA_HEADS, MLA_NOPE + MLA_ROPE)
    return q[..., :MLA_NOPE], q[..., MLA_NOPE:]


def mla_compress(u, w_dkv, kv_norm):
    kv = u @ w_dkv
    return rms_norm(kv[..., :MLA_KV_LORA], kv_norm), kv[..., MLA_KV_LORA:]


def mla_attend(q_nope, q_rope, ckv, krope, w_uk, w_uv, w_o):
    B, S, _ = ckv.shape
    T = q_nope.shape[1]
    k_nope = (ckv @ w_uk).reshape(B, S, MLA_HEADS, MLA_NOPE)
    v = (ckv @ w_uv).reshape(B, S, MLA_HEADS, MLA_V)
    k = jnp.concatenate([k_nope, jnp.broadcast_to(krope[:, :, None, :], (B, S, MLA_HEADS, MLA_ROPE))], -1)
    q = jnp.concatenate([q_nope, q_rope], -1)
    o = block_attention(q, k, v, MLA_SCALE)
    return o.reshape(B, T, MLA_HEADS * MLA_V) @ w_o


def gqa_split(u, w_qkv):
    B, T, _ = u.shape
    qkv = u @ w_qkv
    nq = GQA_HEADS * GQA_HEAD_DIM
    nk = GQA_KV_HEADS * GQA_HEAD_DIM
    q = qkv[..., :nq].reshape(B, T, GQA_HEADS, GQA_HEAD_DIM)
    k = qkv[..., nq:nq + nk].reshape(B, T, GQA_KV_HEADS, GQA_HEAD_DIM)
    v = qkv[..., nq + nk:].reshape(B, T, GQA_KV_HEADS, GQA_HEAD_DIM)
    return q, k, v


def fourier_mix(u, w_out):
    B, T, D = u.shape
    ug = u.astype(jnp.float32).reshape(B, T, FNET_GROUPS, FNET_GROUP_DIM)
    f = jnp.fft.fftn(ug, axes=(1, 3), norm='ortho').real
    return f.reshape(B, T, D).astype(u.dtype) @ w_out


def short_conv_mix(u, w_in, conv_w, conv_b, w_out):
    D = u.shape[-1]
    b_gate, c_gate, h = jnp.split(u @ w_in, 3, axis=-1)
    z = c_gate * h
    conv = lax.conv_general_dilated(z, conv_w.reshape(CONV_WIDTH, 1, D).astype(z.dtype), window_strides=(1,),
                                    padding=((CONV_WIDTH // 2, CONV_WIDTH // 2),),
                                    dimension_numbers=('NWC', 'WIO', 'NWC'), feature_group_count=D)
    return (b_gate * (conv + conv_b)) @ w_out


def peer_ffn(u, w_q, sub_keys, exp_u, exp_v):
    B, T, D = u.shape
    N = B * T
    xt = u.reshape(N, D)
    q = (xt @ w_q).reshape(N, PEER_HEADS, 2, PEER_HALF)
    s = jnp.einsum('nhcd,hckd->nhck', q, sub_keys, preferred_element_type=jnp.float32)
    v1, i1 = lax.top_k(s[:, :, 0], PEER_TOPK)
    v2, i2 = lax.top_k(s[:, :, 1], PEER_TOPK)
    cand_s = (v1[..., :, None] + v2[..., None, :]).reshape(N, PEER_HEADS, PEER_TOPK * PEER_TOPK)
    cand_i = (i1[..., :, None] * PEER_N_KEYS + i2[..., None, :]).reshape(N, PEER_HEADS, PEER_TOPK * PEER_TOPK)
    top_s, pos = lax.top_k(cand_s, PEER_TOPK)
    idx = jnp.take_along_axis(cand_i, pos, axis=-1)
    gate = jax.nn.softmax(top_s, axis=-1)
    nb = N // TOKEN_BLOCK
    xb = xt.reshape(nb, TOKEN_BLOCK, D)
    ib = idx.reshape(nb, TOKEN_BLOCK, PEER_HEADS * PEER_TOPK)
    gb = gate.reshape(nb, TOKEN_BLOCK, PEER_HEADS * PEER_TOPK)

    def one_block(args):
        xi, ii, gi = args
        h = jnp.einsum('td,ted->te', xi, exp_u[ii], preferred_element_type=jnp.float32)
        a = (jax.nn.gelu(h, approximate=False) * gi).astype(xi.dtype)
        return jnp.einsum('te,ted->td', a, exp_v[ii])

    return lax.map(one_block, (xb, ib, gb)).reshape(B, T, D)


def setup_inputs(seed: int = 0) -> dict:
    key = jax.random.key(seed)
    ks = iter(jax.random.split(key, 40))
    f32 = jnp.float32
    D = D_MODEL

    def nrm(shape, scale):
        return jax.random.normal(next(ks), shape, f32) * scale

    inp = {}
    inp['x_prompt'] = nrm((BATCH, SEQ, D), 1.0)
    inp['x_sample'] = nrm((DEC_BATCH, DEC_SEQ, D), 1.0)
    inp['cache_l0_ckv'] = nrm((DEC_BATCH, PAST_LEN, MLA_KV_LORA), 1.0)
    inp['cache_l0_krope'] = nrm((DEC_BATCH, PAST_LEN, MLA_ROPE), 1.0)
    inp['cache_l1_k'] = nrm((DEC_BATCH, PAST_LEN, GQA_KV_HEADS, GQA_HEAD_DIM), 1.0)
    inp['cache_l1_v'] = nrm((DEC_BATCH, PAST_LEN, GQA_KV_HEADS, GQA_HEAD_DIM), 1.0)
    inp['c'] = nrm((DEC_BATCH, D), 1.0)
    inp['c_ctx'] = nrm((D,), 1.0)
    inp['ada_w'] = nrm((DEPTH, D, 6 * D), 0.5 * D ** -0.5)
    inp['ada_b'] = nrm((DEPTH, 6 * D), 0.01)
    inp['ln1_g'] = 1.0 + nrm((DEPTH, D), 0.01)
    inp['ln1_b'] = nrm((DEPTH, D), 0.01)
    inp['ln2_g'] = 1.0 + nrm((DEPTH, D), 0.01)
    inp['ln2_b'] = nrm((DEPTH, D), 0.01)
    inp['mla_w_dq'] = nrm((D, MLA_Q_LORA), D ** -0.5)
    inp['mla_q_norm'] = 1.0 + nrm((MLA_Q_LORA,), 0.01)
    inp['mla_w_uq'] = nrm((MLA_Q_LORA, MLA_HEADS * (MLA_NOPE + MLA_ROPE)), MLA_Q_LORA ** -0.5)
    inp['mla_w_dkv'] = nrm((D, MLA_KV_LORA + MLA_ROPE), D ** -0.5)
    inp['mla_kv_norm'] = 1.0 + nrm((MLA_KV_LORA,), 0.01)
    inp['mla_w_uk'] = nrm((MLA_KV_LORA, MLA_HEADS * MLA_NOPE), MLA_KV_LORA ** -0.5)
    inp['mla_w_uv'] = nrm((MLA_KV_LORA, MLA_HEADS * MLA_V), MLA_KV_LORA ** -0.5)
    inp['mla_w_o'] = nrm((MLA_HEADS * MLA_V, D), DEEPNORM_BETA * (MLA_HEADS * MLA_V) ** -0.5)
    inp['gqa_w_qkv'] = nrm((D, (GQA_HEADS + 2 * GQA_KV_HEADS) * GQA_HEAD_DIM), D ** -0.5)
    inp['gqa_sink'] = nrm((GQA_HEADS,), 0.5)
    inp['gqa_w_o'] = nrm((GQA_HEADS * GQA_HEAD_DIM, D), DEEPNORM_BETA * (GQA_HEADS * GQA_HEAD_DIM) ** -0.5)
    inp['fnet_w_out'] = nrm((D, D), DEEPNORM_BETA * D ** -0.5)
    inp['conv_w_in'] = nrm((D, 3 * D), D ** -0.5)
    inp['conv_w'] = nrm((CONV_WIDTH, D), CONV_WIDTH ** -0.5)
    inp['conv_b'] = nrm((D,), 0.01)
    inp['conv_w_out'] = nrm((D, D), DEEPNORM_BETA * D ** -0.5)
    inp['peer_w_q'] = nrm((DEPTH, D, PEER_HEADS * PEER_QUERY_DIM), D ** -0.5)
    inp['peer_sub_keys'] = nrm((DEPTH, PEER_HEADS, 2, PEER_N_KEYS, PEER_HALF), PEER_HALF ** -0.5)
    inp['peer_u'] = nrm((DEPTH, PEER_N_EXPERTS, D), D ** -0.5)
    inp['peer_v'] = nrm((DEPTH, PEER_N_EXPERTS, D), DEEPNORM_BETA)
    return inp


def reference(x_prompt, x_sample, cache_l0_ckv, cache_l0_krope, cache_l1_k, cache_l1_v, c, c_ctx,
              ada_w, ada_b, ln1_g, ln1_b, ln2_g, ln2_b,
              mla_w_dq, mla_q_norm, mla_w_uq, mla_w_dkv, mla_kv_norm, mla_w_uk, mla_w_uv, mla_w_o,
              gqa_w_qkv, gqa_sink, gqa_w_o,
              fnet_w_out,
              conv_w_in, conv_w, conv_b, conv_w_out,
              peer_w_q, peer_sub_keys, peer_u, peer_v):
    xp, xs = x_prompt, x_sample
    new_l0_ckv = new_l0_krope = new_l1_k = new_l1_v = None
    for i in range(DEPTH):
        m = i % N_MIXERS
        mp = ada_modulation(c_ctx[None, :], ada_w[i], ada_b[i])
        ms = ada_modulation(c, ada_w[i], ada_b[i])
        up = modulate(xp, mp[0], mp[1])
        us = modulate(xs, ms[0], ms[1])
        if m == 0:
            qn, qr = mla_queries(up, mla_w_dq, mla_q_norm, mla_w_uq)
            ckv_p, kr_p = mla_compress(up, mla_w_dkv, mla_kv_norm)
            op = mla_attend(qn, qr, ckv_p, kr_p, mla_w_uk, mla_w_uv, mla_w_o)
            new_l0_ckv, new_l0_krope = ckv_p, kr_p
            qn, qr = mla_queries(us, mla_w_dq, mla_q_norm, mla_w_uq)
            qr = axial_rope(qr)
            ckv_s, kr_s = mla_compress(us, mla_w_dkv, mla_kv_norm)
            kr_s = axial_rope(kr_s[:, :, None, :])[:, :, 0, :]
            os_ = mla_attend(qn, qr, jnp.concatenate([ckv_s, cache_l0_ckv], 1),
                             jnp.concatenate([kr_s, cache_l0_krope], 1), mla_w_uk, mla_w_uv, mla_w_o)
        elif m == 1:
            q, k, v = gqa_split(up, gqa_w_qkv)
            op = block_attention(q, k, v, GQA_SCALE, gqa_sink).reshape(xp.shape[0], xp.shape[1], -1) @ gqa_w_o
            new_l1_k, new_l1_v = k, v
            q, k, v = gqa_split(us, gqa_w_qkv)
            o = banded_window_attention(axial_rope(q), axial_rope(k), v, cache_l1_k, cache_l1_v, gqa_sink, GQA_SCALE)
            os_ = o.reshape(xs.shape[0], xs.shape[1], -1) @ gqa_w_o
        elif m == 2:
            op = fourier_mix(up, fnet_w_out)
            os_ = fourier_mix(us, fnet_w_out)
        else:
            op = short_conv_mix(up, conv_w_in, conv_w, conv_b, conv_w_out)
            os_ = short_conv_mix(us, conv_w_in, conv_w, conv_b, conv_w_out)
        xp = post_norm(xp, op, mp[2], ln1_g[i], ln1_b[i])
        xs = post_norm(xs, os_, ms[2], ln1_g[i], ln1_b[i])
        up = modulate(xp, mp[3], mp[4])
        us = modulate(xs, ms[3], ms[4])
        fp = peer_ffn(up, peer_w_q[i], peer_sub_keys[i], peer_u[i], peer_v[i])
        fs = peer_ffn(us, peer_w_q[i], peer_sub_keys[i], peer_u[i], peer_v[i])
        xp = post_norm(xp, fp, mp[5], ln2_g[i], ln2_b[i])
        xs = post_norm(xs, fs, ms[5], ln2_g[i], ln2_b[i])
    return (xp, xs, new_l0_ckv, new_l0_krope, new_l1_k, new_l1_v)
```

```python
import functools
import math

import numpy as np
import jax
import jax.numpy as jnp
from jax import lax
from jax.experimental import pallas as pl
from jax.experimental.pallas import tpu as pltpu

F32 = jnp.float32
BF16 = jnp.bfloat16

D_MODEL = 2048
BATCH = 16
SEQ = 256
DEPTH = 4
DEC_BATCH = 2
DEC_SEQ = 1024
PAST_LEN = 256
GRID_W = 64
ROPE_BASE = 10000.0
LN_EPS = 1e-5
RMS_EPS = 1e-6
NEG_INF = -1e30
DEEPNORM_ALPHA = (2 * DEPTH) ** 0.25

MLA_HEADS = 16
MLA_Q_LORA = 512
MLA_KV_LORA = 512
MLA_NOPE = 128
MLA_ROPE = 64
MLA_V = 128
MLA_SCALE = (MLA_NOPE + MLA_ROPE) ** -0.5

GQA_HEADS = 32
GQA_KV_HEADS = 8
GQA_HEAD_DIM = 64
WINDOW = 128
GQA_SCALE = GQA_HEAD_DIM ** -0.5

FNET_GROUPS = 4
FNET_GROUP_DIM = D_MODEL // FNET_GROUPS

PEER_HEADS = 8
PEER_N_KEYS = 128
PEER_N_EXPERTS = PEER_N_KEYS * PEER_N_KEYS
PEER_HALF = 128
PEER_TOPK = 16

N_PROMPT = BATCH * SEQ
N_SAMPLE = DEC_BATCH * DEC_SEQ
N_TOK = N_PROMPT + N_SAMPLE
N_COND = 8

MIB = 1024 * 1024

NT_DIMS = (((1,), (1,)), ((), ()))
TN_DIMS = (((0,), (0,)), ((), ()))


def _params(sem, vmem_mib):
    return pltpu.CompilerParams(dimension_semantics=sem, vmem_limit_bytes=vmem_mib * MIB)


def _group_of_tile(i, tm):
    return jnp.maximum((i * tm) // DEC_SEQ - (N_PROMPT // DEC_SEQ - 1), 0)


def _mod_spec(tm, which):
    return pl.BlockSpec((None, None, 1, D_MODEL), lambda i: (_group_of_tile(i, tm), which, 0, 0))


def _ada_kernel(c_ref, w_ref, b_ref, o_ref):
    c = c_ref[...]
    s = (c * jax.nn.sigmoid(c)).astype(BF16)
    o_ref[...] = jnp.dot(s, w_ref[...].astype(BF16), preferred_element_type=F32) + b_ref[...]


def ada_modulation_all(cond, ada_w, ada_b):
    tn = 1024
    n_out = 6 * D_MODEL
    return pl.pallas_call(
        _ada_kernel,
        out_shape=jax.ShapeDtypeStruct((DEPTH, N_COND, n_out), F32),
        grid=(DEPTH, n_out // tn),
        in_specs=[
            pl.BlockSpec((N_COND, D_MODEL), lambda l, j: (0, 0)),
            pl.BlockSpec((None, D_MODEL, tn), lambda l, j: (l, 0, j)),
            pl.BlockSpec((None, 1, tn), lambda l, j: (l, 0, j)),
        ],
        out_specs=pl.BlockSpec((None, N_COND, tn), lambda l, j: (l, 0, j)),
        compiler_params=_params(("parallel", "parallel"), 40),
        name="ada_modulation",
    )(cond, ada_w, ada_b.reshape(DEPTH, 1, n_out))


def _modulate_kernel(x_ref, sh_ref, sc_ref, u_ref):
    u_ref[...] = (x_ref[...] * (1.0 + sc_ref[...]) + sh_ref[...]).astype(u_ref.dtype)


def modulate_tokens(x, mods):
    tm = 256
    return pl.pallas_call(
        _modulate_kernel,
        out_shape=jax.ShapeDtypeStruct((N_TOK, D_MODEL), BF16),
        grid=(N_TOK // tm,),
        in_specs=[pl.BlockSpec((tm, D_MODEL), lambda i: (i, 0)), _mod_spec(tm, 0), _mod_spec(tm, 1)],
        out_specs=pl.BlockSpec((tm, D_MODEL), lambda i: (i, 0)),
        compiler_params=_params(("parallel",), 24),
        name="modulate",
    )(x, mods, mods)


def _layer_norm_rows(y, g, b):
    mu = jnp.mean(y, axis=-1, keepdims=True)
    yc = y - mu
    var = jnp.mean(yc * yc, axis=-1, keepdims=True)
    return yc * lax.rsqrt(var + LN_EPS) * g + b


def _post_norm_kernel(x_ref, d_ref, gate_ref, g_ref, b_ref, xo_ref):
    y = DEEPNORM_ALPHA * x_ref[...] + gate_ref[...] * d_ref[...].astype(F32)
    xo_ref[...] = _layer_norm_rows(y, g_ref[...], b_ref[...])


def _post_norm_mod_kernel(x_ref, d_ref, gate_ref, g_ref, b_ref, sh_ref, sc_ref, xo_ref, uo_ref):
    y = DEEPNORM_ALPHA * x_ref[...] + gate_ref[...] * d_ref[...].astype(F32)
    xn = _layer_norm_rows(y, g_ref[...], b_ref[...])
    xo_ref[...] = xn
    uo_ref[...] = (xn * (1.0 + sc_ref[...]) + sh_ref[...]).astype(uo_ref.dtype)


def post_norm(x, delta, mods, gate_idx, g, b, next_mods=None, next_idx=None):
    tm = 256
    row = pl.BlockSpec((tm, D_MODEL), lambda i: (i, 0))
    vec = pl.BlockSpec((1, D_MODEL), lambda i: (0, 0))
    ins = [x, delta, mods, g.reshape(1, D_MODEL), b.reshape(1, D_MODEL)]
    specs = [row, row, _mod_spec(tm, gate_idx), vec, vec]
    if next_mods is None:
        return pl.pallas_call(
            _post_norm_kernel,
            out_shape=jax.ShapeDtypeStruct((N_TOK, D_MODEL), F32),
            grid=(N_TOK // tm,), in_specs=specs, out_specs=row,
            compiler_params=_params(("parallel",), 32), name="post_norm",
        )(*ins)
    ins += [next_mods, next_mods]
    specs += [_mod_spec(tm, next_idx), _mod_spec(tm, next_idx + 1)]
    return pl.pallas_call(
        _post_norm_mod_kernel,
        out_shape=(jax.ShapeDtypeStruct((N_TOK, D_MODEL), F32), jax.ShapeDtypeStruct((N_TOK, D_MODEL), BF16)),
        grid=(N_TOK // tm,), in_specs=specs, out_specs=(row, row),
        compiler_params=_params(("parallel",), 32), name="post_norm_mod",
    )(*ins)


def _mm_kernel(x_ref, w_ref, o_ref):
    o_ref[...] = jnp.dot(x_ref[...], w_ref[...].astype(BF16), preferred_element_type=F32).astype(o_ref.dtype)


def matmul(x, w, out_dtype, tm=1024, tn=512, name="matmul"):
    m, k = x.shape
    n = w.shape[1]
    tm = min(tm, m)
    tn = min(tn, n)
    assert m % tm == 0 and n % tn == 0, (m, n, tm, tn)
    return pl.pallas_call(
        _mm_kernel,
        out_shape=jax.ShapeDtypeStruct((m, n), out_dtype),
        grid=(m // tm, n // tn),
        in_specs=[pl.BlockSpec((tm, k), lambda i, j: (i, 0)), pl.BlockSpec((k, tn), lambda i, j: (0, j))],
        out_specs=pl.BlockSpec((tm, tn), lambda i, j: (i, j)),
        compiler_params=_params(("parallel", "parallel"), 40),
        name=name,
    )(x, w)


def _rms_rows(y, g):
    return y * lax.rsqrt(jnp.mean(y * y, axis=-1, keepdims=True) + RMS_EPS) * g


def _mla_down_kernel(x_ref, w_ref, qn_ref, kvn_ref, cq_ref, ckv_ref, kr_ref):
    y = jnp.dot(x_ref[...], w_ref[...].astype(BF16), preferred_element_type=F32)
    cq_ref[...] = _rms_rows(y[:, :MLA_Q_LORA], qn_ref[...]).astype(cq_ref.dtype)
    ckv_ref[...] = _rms_rows(y[:, MLA_Q_LORA:MLA_Q_LORA + MLA_KV_LORA], kvn_ref[...])
    kr_ref[...] = y[:, MLA_Q_LORA + MLA_KV_LORA:MLA_Q_LORA + MLA_KV_LORA + MLA_ROPE]


def mla_down(u, w_cat, q_norm, kv_norm):
    tm = 512
    n = w_cat.shape[1]
    return pl.pallas_call(
        _mla_down_kernel,
        out_shape=(jax.ShapeDtypeStruct((N_TOK, MLA_Q_LORA), BF16),
                   jax.ShapeDtypeStruct((N_TOK, MLA_KV_LORA), F32),
                   jax.ShapeDtypeStruct((N_TOK, MLA_ROPE), F32)),
        grid=(N_TOK // tm,),
        in_specs=[pl.BlockSpec((tm, D_MODEL), lambda i: (i, 0)),
                  pl.BlockSpec((D_MODEL, n), lambda i: (0, 0)),
                  pl.BlockSpec((1, MLA_Q_LORA), lambda i: (0, 0)),
                  pl.BlockSpec((1, MLA_KV_LORA), lambda i: (0, 0))],
        out_specs=(pl.BlockSpec((tm, MLA_Q_LORA), lambda i: (i, 0)),
                   pl.BlockSpec((tm, MLA_KV_LORA), lambda i: (i, 0)),
                   pl.BlockSpec((tm, MLA_ROPE), lambda i: (i, 0))),
        compiler_params=_params(("parallel",), 48),
        name="mla_down",
    )(u, w_cat, q_norm.reshape(1, -1), kv_norm.reshape(1, -1))


def _rope_tables():
    t = np.arange(DEC_SEQ)
    quarter = MLA_ROPE // 4
    inv_freq = ROPE_BASE ** (-np.arange(quarter, dtype=np.float64) / quarter)
    ang_row = (t // GRID_W)[:, None] * inv_freq[None, :]
    ang_col = (t % GRID_W)[:, None] * inv_freq[None, :]
    cos = np.concatenate([np.cos(ang_row)] * 2 + [np.cos(ang_col)] * 2, -1)
    sin = np.concatenate([-np.sin(ang_row), np.sin(ang_row), -np.sin(ang_col), np.sin(ang_col)], -1)
    return np.tile(cos, (1, 2)), np.tile(sin, (1, 2))


def _rope_kernel(x_ref, c_ref, s_ref, o_ref):
    x = x_ref[...].astype(F32)
    w = x.shape[-1]
    reps = w // c_ref.shape[-1]
    cos = jnp.tile(c_ref[...], (1, reps))
    sin = jnp.tile(s_ref[...], (1, reps))
    lane = lax.broadcasted_iota(jnp.int32, x.shape, 1)
    partner = jnp.where((lane % 32) < 16, pltpu.roll(x, w - 16, 1), pltpu.roll(x, 16, 1))
    o_ref[...] = (x * cos + partner * sin).astype(o_ref.dtype)


def rope_rows(x, row_block0, col_block, width, n_rows, cos, sin, out_dtype, tr=256):
    t_blocks = cos.shape[0] // tr
    return pl.pallas_call(
        _rope_kernel,
        out_shape=jax.ShapeDtypeStruct((n_rows, width), out_dtype),
        grid=(n_rows // tr,),
        in_specs=[pl.BlockSpec((tr, width), lambda i: (row_block0 + i, col_block)),
                  pl.BlockSpec((tr, 128), lambda i: (i % t_blocks, 0)),
                  pl.BlockSpec((tr, 128), lambda i: (i % t_blocks, 0))],
        out_specs=pl.BlockSpec((tr, width), lambda i: (i, 0)),
        compiler_params=_params(("parallel",), 32),
        name="axial_rope",
    )(x, cos, sin)


def _softmax_parts(parts, sink):
    m = parts[0].max(axis=-1, keepdims=True)
    for s in parts[1:]:
        m = jnp.maximum(m, s.max(axis=-1, keepdims=True))
    if sink is not None:
        m = jnp.maximum(m, sink)
    ps = [jnp.exp(s - m) for s in parts]
    l = ps[0].sum(axis=-1, keepdims=True)
    for p in ps[1:]:
        l = l + p.sum(axis=-1, keepdims=True)
    if sink is not None:
        l = l + jnp.exp(sink - m)
    return ps, 1.0 / l


def _mla_attn_kernel(*refs, with_ctx):
    if with_ctx:
        qn_ref, qr_ref, kn_ref, v_ref, kr_ref, kn2_ref, v2_ref, kr2_ref, o_ref = refs
    else:
        qn_ref, qr_ref, kn_ref, v_ref, kr_ref, o_ref = refs
    kr = kr_ref[...].astype(BF16)
    kr2 = kr2_ref[...].astype(BF16) if with_ctx else None
    for h in range(MLA_HEADS):
        n0, n1 = h * MLA_NOPE, (h + 1) * MLA_NOPE
        qn = qn_ref[:, n0:n1]
        qr = qr_ref[:, h * MLA_ROPE:(h + 1) * MLA_ROPE]
        s = (lax.dot_general(qn, kn_ref[:, n0:n1], NT_DIMS, preferred_element_type=F32)
             + lax.dot_general(qr, kr, NT_DIMS, preferred_element_type=F32)) * MLA_SCALE
        parts = [s]
        if with_ctx:
            s2 = (lax.dot_general(qn, kn2_ref[:, n0:n1], NT_DIMS, preferred_element_type=F32)
                  + lax.dot_general(qr, kr2, NT_DIMS, preferred_element_type=F32)) * MLA_SCALE
            parts.append(s2)
        ps, inv_l = _softmax_parts(parts, None)
        o = jnp.dot(ps[0].astype(BF16), v_ref[:, n0:n1], preferred_element_type=F32)
        if with_ctx:
            o = o + jnp.dot(ps[1].astype(BF16), v2_ref[:, n0:n1], preferred_element_type=F32)
        o_ref[:, n0:n1] = (o * inv_l).astype(o_ref.dtype)


def mla_attention_prompt(q, kv_up, kr):
    w = MLA_HEADS * MLA_NOPE
    return pl.pallas_call(
        functools.partial(_mla_attn_kernel, with_ctx=False),
        out_shape=jax.ShapeDtypeStruct((N_PROMPT, w), BF16),
        grid=(BATCH,),
        in_specs=[pl.BlockSpec((SEQ, w), lambda b: (b, 0)),
                  pl.BlockSpec((SEQ, MLA_HEADS * MLA_ROPE), lambda b: (b, w // (MLA_HEADS * MLA_ROPE))),
                  pl.BlockSpec((SEQ, w), lambda b: (b, 0)),
                  pl.BlockSpec((SEQ, w), lambda b: (b, 1)),
                  pl.BlockSpec((SEQ, MLA_ROPE), lambda b: (b, 0))],
        out_specs=pl.BlockSpec((SEQ, w), lambda b: (b, 0)),
        compiler_params=_params(("parallel",), 32),
        name="mla_attention_prompt",
    )(q, q, kv_up, kv_up, kr)


def mla_attention_sample(qn, qr_rot, kv_up, kr_rot, kr_cache):
    w = MLA_HEADS * MLA_NOPE
    tq = 256
    nq = DEC_SEQ // tq
    q0 = N_PROMPT // tq
    lat0 = N_PROMPT // DEC_SEQ
    ctx0 = N_TOK // PAST_LEN
    return pl.pallas_call(
        functools.partial(_mla_attn_kernel, with_ctx=True),
        out_shape=jax.ShapeDtypeStruct((N_SAMPLE, w), BF16),
        grid=(DEC_BATCH, nq),
        in_specs=[pl.BlockSpec((tq, w), lambda b, i: (q0 + b * nq + i, 0)),
                  pl.BlockSpec((tq, MLA_HEADS * MLA_ROPE), lambda b, i: (b * nq + i, 0)),
                  pl.BlockSpec((DEC_SEQ, w), lambda b, i: (lat0 + b, 0)),
                  pl.BlockSpec((DEC_SEQ, w), lambda b, i: (lat0 + b, 1)),
                  pl.BlockSpec((DEC_SEQ, MLA_ROPE), lambda b, i: (b, 0)),
                  pl.BlockSpec((PAST_LEN, w), lambda b, i: (ctx0 + b, 0)),
                  pl.BlockSpec((PAST_LEN, w), lambda b, i: (ctx0 + b, 1)),
                  pl.BlockSpec((PAST_LEN, MLA_ROPE), lambda b, i: (b, 0))],
        out_specs=pl.BlockSpec((tq, w), lambda b, i: (b * nq + i, 0)),
        compiler_params=_params(("parallel", "parallel"), 56),
        name="mla_attention_sample",
    )(qn, qr_rot, kv_up, kv_up, kr_rot, kv_up, kv_up, kr_cache)


def _gqa_attn_kernel(*refs, with_ctx, tq):
    if with_ctx:
        sink_ref, q_ref, k_ref, v_ref, k2_ref, v2_ref, o_ref = refs
    else:
        sink_ref, q_ref, k_ref, v_ref, o_ref = refs
    group = GQA_HEADS // GQA_KV_HEADS
    d = GQA_HEAD_DIM
    if with_ctx:
        t = pl.program_id(1) * tq + lax.broadcasted_iota(jnp.int32, (tq, DEC_SEQ), 0)
        s_pos = lax.broadcasted_iota(jnp.int32, (tq, DEC_SEQ), 1)
        in_window = jnp.abs(t - s_pos) <= WINDOW
    for hk in range(GQA_KV_HEADS):
        k = k_ref[:, hk * d:(hk + 1) * d].astype(BF16)
        v = v_ref[:, hk * d:(hk + 1) * d].astype(BF16)
        if with_ctx:
            k2 = k2_ref[:, hk * d:(hk + 1) * d].astype(BF16)
            v2 = v2_ref[:, hk * d:(hk + 1) * d].astype(BF16)
        for g in range(group):
            h = hk * group + g
            q = q_ref[:, h * d:(h + 1) * d]
            s = lax.dot_general(q, k, NT_DIMS, preferred_element_type=F32) * GQA_SCALE
            parts = [s]
            if with_ctx:
                parts = [jnp.where(in_window, s, NEG_INF),
                         lax.dot_general(q, k2, NT_DIMS, preferred_element_type=F32) * GQA_SCALE]
            ps, inv_l = _softmax_parts(parts, sink_ref[h])
            o = jnp.dot(ps[0].astype(BF16), v, preferred_element_type=F32)
            if with_ctx:
                o = o + jnp.dot(ps[1].astype(BF16), v2, preferred_element_type=F32)
            o_ref[:, h * d:(h + 1) * d] = (o * inv_l).astype(o_ref.dtype)


def gqa_attention_prompt(q, kv, sink):
    wq = GQA_HEADS * GQA_HEAD_DIM
    wk = GQA_KV_HEADS * GQA_HEAD_DIM
    return pl.pallas_call(
        functools.partial(_gqa_attn_kernel, with_ctx=False, tq=SEQ),
        out_shape=jax.ShapeDtypeStruct((N_PROMPT, wq), BF16),
        grid=(BATCH,),
        in_specs=[pl.BlockSpec(memory_space=pltpu.SMEM),
                  pl.BlockSpec((SEQ, wq), lambda b: (b, 0)),
                  pl.BlockSpec((SEQ, wk), lambda b: (b, 0)),
                  pl.BlockSpec((SEQ, wk), lambda b: (b, 1))],
        out_specs=pl.BlockSpec((SEQ, wq), lambda b: (b, 0)),
        compiler_params=_params(("parallel",), 32),
        name="gqa_attention_prompt",
    )(sink, q, kv, kv)


def gqa_attention_sample(q_rot, k_rot, kv, k_cache, v_cache, sink):
    wq = GQA_HEADS * GQA_HEAD_DIM
    wk = GQA_KV_HEADS * GQA_HEAD_DIM
    tq = 256
    nq = DEC_SEQ // tq
    lat0 = N_PROMPT // DEC_SEQ
    return pl.pallas_call(
        functools.partial(_gqa_attn_kernel, with_ctx=True, tq=tq),
        out_shape=jax.ShapeDtypeStruct((N_SAMPLE, wq), BF16),
        grid=(DEC_BATCH, nq),
        in_specs=[pl.BlockSpec(memory_space=pltpu.SMEM),
                  pl.BlockSpec((tq, wq), lambda b, i: (b * nq + i, 0)),
                  pl.BlockSpec((DEC_SEQ, wk), lambda b, i: (b, 0)),
                  pl.BlockSpec((DEC_SEQ, wk), lambda b, i: (lat0 + b, 1)),
                  pl.BlockSpec((PAST_LEN, wk), lambda b, i: (b, 0)),
                  pl.BlockSpec((PAST_LEN, wk), lambda b, i: (b, 0))],
        out_specs=pl.BlockSpec((tq, wq), lambda b, i: (b * nq + i, 0)),
        compiler_params=_params(("parallel", "parallel"), 56),
        name="gqa_attention_sample",
    )(sink, q_rot, k_rot, kv, k_cache, v_cache)


def _dft_tables(n):
    jk = (np.arange(n)[:, None] * np.arange(n)[None, :]) % n
    ang = 2.0 * np.pi * jk / n
    return np.cos(ang) / math.sqrt(n), np.sin(ang) / math.sqrt(n)


def _fnet_kernel(x_ref, ct_ref, st_ref, cc_ref, sc_ref, o_ref):
    x = x_ref[...]
    y1 = jnp.dot(x, cc_ref[...], preferred_element_type=F32).astype(BF16)
    y2 = jnp.dot(x, sc_ref[...], preferred_element_type=F32).astype(BF16)
    o = (jnp.dot(ct_ref[...], y1, preferred_element_type=F32)
         - jnp.dot(st_ref[...], y2, preferred_element_type=F32))
    o_ref[...] = o.astype(o_ref.dtype)


def fourier_real_2d(u, seq_block0, n_seq, t):
    ct, st = (jnp.asarray(a, BF16) for a in _dft_tables(t))
    cc, sc = (jnp.asarray(a, BF16) for a in _dft_tables(FNET_GROUP_DIM))
    gd = FNET_GROUP_DIM
    return pl.pallas_call(
        _fnet_kernel,
        out_shape=jax.ShapeDtypeStruct((n_seq * t, D_MODEL), BF16),
        grid=(n_seq, FNET_GROUPS),
        in_specs=[pl.BlockSpec((t, gd), lambda b, g: (seq_block0 + b, g)),
                  pl.BlockSpec((t, t), lambda b, g: (0, 0)),
                  pl.BlockSpec((t, t), lambda b, g: (0, 0)),
                  pl.BlockSpec((gd, gd), lambda b, g: (0, 0)),
                  pl.BlockSpec((gd, gd), lambda b, g: (0, 0))],
        out_specs=pl.BlockSpec((t, gd), lambda b, g: (b, g)),
        compiler_params=_params(("parallel", "parallel"), 32),
        name="fourier_mix",
    )(u, ct, st, cc, sc)


def _conv_kernel(b_ref, c_ref, h_ref, w_ref, cb_ref, o_ref):
    z = c_ref[...].astype(F32) * h_ref[...].astype(F32)
    t = z.shape[0]
    row = lax.broadcasted_iota(jnp.int32, z.shape, 0)
    z_prev = jnp.where(row == 0, 0.0, pltpu.roll(z, 1, 0))
    z_next = jnp.where(row == t - 1, 0.0, pltpu.roll(z, t - 1, 0))
    conv = z_prev * w_ref[0:1, :] + z * w_ref[1:2, :] + z_next * w_ref[2:3, :] + cb_ref[...]
    o_ref[...] = (b_ref[...].astype(F32) * conv).astype(o_ref.dtype)


def gated_conv(h3, conv_w, conv_b, seq_block0, n_seq, t):
    tn = 512
    nj = D_MODEL // tn
    return pl.pallas_call(
        _conv_kernel,
        out_shape=jax.ShapeDtypeStruct((n_seq * t, D_MODEL), BF16),
        grid=(n_seq, nj),
        in_specs=[pl.BlockSpec((t, tn), lambda s, j: (seq_block0 + s, j)),
                  pl.BlockSpec((t, tn), lambda s, j: (seq_block0 + s, nj + j)),
                  pl.BlockSpec((t, tn), lambda s, j: (seq_block0 + s, 2 * nj + j)),
                  pl.BlockSpec((3, tn), lambda s, j: (0, j)),
                  pl.BlockSpec((1, tn), lambda s, j: (0, j))],
        out_specs=pl.BlockSpec((t, tn), lambda s, j: (s, j)),
        compiler_params=_params(("parallel", "parallel"), 32),
        name="gated_conv",
    )(h3, h3, h3, conv_w, conv_b.reshape(1, D_MODEL))


_N_RANKS = PEER_TOPK + 1
_CAND_PAIRS = [(a, b) for a in range(_N_RANKS) for b in range(_N_RANKS) if (a + 1) * (b + 1) <= _N_RANKS]


def _top_values(s, k):
    n = s.shape[0]
    idx = lax.broadcasted_iota(jnp.int32, s.shape, 0)
    out = []
    for it in range(k):
        m = jnp.max(s, axis=0, keepdims=True)
        out.append(m)
        if it + 1 < k:
            first = jnp.min(jnp.where(s == m, idx, n), axis=0, keepdims=True)
            s = jnp.where(idx == first, -jnp.inf, s)
    return out


def _route_kernel(q_ref, sk_ref, b_ref, th_ref, a_ref):
    tm = q_ref.shape[0]
    for h in range(PEER_HEADS):
        s = []
        for c in range(2):
            hc = 2 * h + c
            s.append(lax.dot_general(sk_ref[hc].astype(BF16), q_ref[:, hc * PEER_HALF:(hc + 1) * PEER_HALF],
                                     NT_DIMS, preferred_element_type=F32))
        v1 = _top_values(s[0], _N_RANKS)
        v2 = _top_values(s[1], _N_RANKS)
        n_cand = len(_CAND_PAIRS) + (-len(_CAND_PAIRS)) % 8
        cand_row = lax.broadcasted_iota(jnp.int32, (n_cand, tm), 0)
        cand = jnp.full((n_cand, tm), -jnp.inf, F32)
        for r, (a, b) in enumerate(_CAND_PAIRS):
            cand = jnp.where(cand_row == r, v1[a] + v2[b], cand)
        top = _top_values(cand, _N_RANKS)
        tau = 0.5 * (top[PEER_TOPK - 1] + top[PEER_TOPK])
        z = jnp.exp(top[0] - top[0])
        for kk in range(1, PEER_TOPK):
            z = z + jnp.exp(top[kk] - top[0])
        inv_z = 1.0 / z
        m1, m2 = v1[0], v2[0]
        b_ref[h * PEER_N_KEYS:(h + 1) * PEER_N_KEYS, :] = jnp.exp(s[1] - m2) * inv_z
        a_h = jnp.exp(s[0] - m1)
        th_h = jnp.exp((tau - m2) - s[0]) * inv_z
        a_ref[:, h, :, :] = a_h.reshape(PEER_N_KEYS // 8, 8, tm)
        th_ref[:, h, :, :] = th_h.reshape(PEER_N_KEYS // 8, 8, tm)


def peer_route(q, sub_keys):
    tm = 256
    sk = sub_keys.reshape(PEER_HEADS * 2, PEER_N_KEYS, PEER_HALF)
    tiles = PEER_N_KEYS // 8
    row_out = jax.ShapeDtypeStruct((tiles, PEER_HEADS, 8, N_TOK), F32)
    row_spec = pl.BlockSpec((tiles, PEER_HEADS, 8, tm), lambda i: (0, 0, 0, i))
    return pl.pallas_call(
        _route_kernel,
        out_shape=(jax.ShapeDtypeStruct((PEER_HEADS * PEER_N_KEYS, N_TOK), F32), row_out, row_out),
        grid=(N_TOK // tm,),
        in_specs=[pl.BlockSpec((tm, PEER_HEADS * 2 * PEER_HALF), lambda i: (i, 0)),
                  pl.BlockSpec((PEER_HEADS * 2, PEER_N_KEYS, PEER_HALF), lambda i: (0, 0, 0))],
        out_specs=(pl.BlockSpec((PEER_HEADS * PEER_N_KEYS, tm), lambda i: (0, i)), row_spec, row_spec),
        compiler_params=_params(("parallel",), 32),
        name="peer_route",
    )(q, sk)


PEER_TE = 512
PEER_TM = 512


def _gelu_exact(x):
    return 0.5 * x * (1.0 + lax.erf(x * math.sqrt(0.5)))


def _peer_kernel(x_ref, b_ref, th_ref, a_ref, u_ref, v_ref, o_ref):
    e = pl.program_id(1)
    keys_per_step = PEER_TE // PEER_N_KEYS

    @pl.when(e == 0)
    def _():
        o_ref[...] = jnp.zeros_like(o_ref)

    ht = lax.dot_general(u_ref[...].astype(BF16), x_ref[...], NT_DIMS, preferred_element_type=F32)
    k0 = (e % (8 // keys_per_step)) * keys_per_step
    parts = []
    for k in range(keys_per_step):
        g = None
        for h in range(PEER_HEADS):
            th = th_ref[0, h, pl.ds(k0 + k, 1), :]
            a = a_ref[0, h, pl.ds(k0 + k, 1), :]
            bh = b_ref[h * PEER_N_KEYS:(h + 1) * PEER_N_KEYS, :]
            contrib = jnp.where(bh >= th, bh, 0.0) * a
            g = contrib if g is None else g + contrib
        hk = ht[k * PEER_N_KEYS:(k + 1) * PEER_N_KEYS, :]
        parts.append((_gelu_exact(hk) * g).astype(BF16))
    at = jnp.concatenate(parts, axis=0)
    o_ref[...] += lax.dot_general(at, v_ref[...].astype(BF16), TN_DIMS, preferred_element_type=F32)


def peer_experts(u, b, th, a, exp_u, exp_v):
    tm, te = PEER_TM, PEER_TE
    steps_per_row_tile = 8 * PEER_N_KEYS // te
    row_spec = pl.BlockSpec((1, PEER_HEADS, 8, tm), lambda i, e: (e // steps_per_row_tile, 0, 0, i))
    return pl.pallas_call(
        _peer_kernel,
        out_shape=jax.ShapeDtypeStruct((N_TOK, D_MODEL), F32),
        grid=(N_TOK // tm, PEER_N_EXPERTS // te),
        in_specs=[pl.BlockSpec((tm, D_MODEL), lambda i, e: (i, 0)),
                  pl.BlockSpec((PEER_HEADS * PEER_N_KEYS, tm), lambda i, e: (0, i)),
                  row_spec, row_spec,
                  pl.BlockSpec((te, D_MODEL), lambda i, e: (e, 0)),
                  pl.BlockSpec((te, D_MODEL), lambda i, e: (e, 0))],
        out_specs=pl.BlockSpec((tm, D_MODEL), lambda i, e: (i, 0)),
        compiler_params=_params(("parallel", "arbitrary"), 52),
        name="peer_experts",
    )(u, b, th, a, exp_u, exp_v)


def peer_ffn(u, w_q, sub_keys, exp_u, exp_v):
    q = matmul(u, w_q, BF16, name="peer_query")
    b, th, a = peer_route(q, sub_keys)
    return peer_experts(u, b, th, a, exp_u, exp_v)


def mla_mixer(u, cache_ckv, cache_krope, w_dq, q_norm, w_uq, w_dkv, kv_norm, w_uk, w_uv, w_o):
    pad = (-(MLA_Q_LORA + MLA_KV_LORA + MLA_ROPE)) % 128
    w_cat = jnp.concatenate([w_dq, w_dkv, jnp.zeros((D_MODEL, pad), F32)], axis=1)
    cq, ckv, kr = mla_down(u, w_cat, q_norm, kv_norm)
    w_uq3 = w_uq.reshape(MLA_Q_LORA, MLA_HEADS, MLA_NOPE + MLA_ROPE)
    w_q_cat = jnp.concatenate([w_uq3[:, :, :MLA_NOPE].reshape(MLA_Q_LORA, -1),
                               w_uq3[:, :, MLA_NOPE:].reshape(MLA_Q_LORA, -1)], axis=1)
    q = matmul(cq, w_q_cat, BF16, name="mla_q_up")
    w_nope = MLA_HEADS * MLA_NOPE
    ckv_src = jnp.concatenate([ckv, cache_ckv.reshape(-1, MLA_KV_LORA)], axis=0).astype(BF16)
    kv_up = matmul(ckv_src, jnp.concatenate([w_uk, w_uv], axis=1), BF16, tm=512, name="mla_kv_up")
    cos, sin = (jnp.asarray(t, F32) for t in _rope_tables())
    w_rope = MLA_HEADS * MLA_ROPE
    qr_rot = rope_rows(q, N_PROMPT // 256, w_nope // w_rope, w_rope, N_SAMPLE, cos, sin, BF16)
    cos2, sin2 = (t.reshape(DEC_SEQ // 2, 128) for t in (cos[:, :MLA_ROPE], sin[:, :MLA_ROPE]))
    kr_pairs = kr[N_PROMPT:].reshape(N_SAMPLE // 2, 2 * MLA_ROPE)
    kr_rot = rope_rows(kr_pairs, 0, 0, 2 * MLA_ROPE, N_SAMPLE // 2, cos2, sin2, BF16).reshape(N_SAMPLE, MLA_ROPE)
    o_p = mla_attention_prompt(q, kv_up, kr)
    o_s = mla_attention_sample(q, qr_rot, kv_up, kr_rot, cache_krope.reshape(-1, MLA_ROPE))
    op = matmul(jnp.concatenate([o_p, o_s], axis=0), w_o, F32, name="mla_out")
    return op, ckv, kr


def gqa_mixer(u, cache_k, cache_v, w_qkv, sink, w_o):
    nq = GQA_HEADS * GQA_HEAD_DIM
    nk = GQA_KV_HEADS * GQA_HEAD_DIM
    q = matmul(u, w_qkv[:, :nq], BF16, name="gqa_q")
    kv = matmul(u, w_qkv[:, nq:], F32, name="gqa_kv")
    cos, sin = (jnp.asarray(t, F32) for t in _rope_tables())
    q_rot = rope_rows(q, N_PROMPT // 256, 0, nq, N_SAMPLE, cos, sin, BF16)
    k_rot = rope_rows(kv, N_PROMPT // 256, 0, nk, N_SAMPLE, cos, sin, BF16)
    o_p = gqa_attention_prompt(q, kv, sink)
    o_s = gqa_attention_sample(q_rot, k_rot, kv, cache_k.reshape(-1, nk), cache_v.reshape(-1, nk), sink)
    op = matmul(jnp.concatenate([o_p, o_s], axis=0), w_o, F32, name="gqa_out")
    return op, kv


def fnet_mixer(u, w_out):
    f_p = fourier_real_2d(u, 0, BATCH, SEQ)
    f_s = fourier_real_2d(u, N_PROMPT // DEC_SEQ, DEC_BATCH, DEC_SEQ)
    return matmul(jnp.concatenate([f_p, f_s], axis=0), w_out, F32, name="fnet_out")


def conv_mixer(u, w_in, conv_w, conv_b, w_out):
    h3 = matmul(u, w_in, BF16, name="conv_in")
    y_p = gated_conv(h3, conv_w, conv_b, 0, BATCH, SEQ)
    y_s = gated_conv(h3, conv_w, conv_b, N_PROMPT // DEC_SEQ, DEC_BATCH, DEC_SEQ)
    return matmul(jnp.concatenate([y_p, y_s], axis=0), w_out, F32, name="conv_out")


def kernel(x_prompt, x_sample, cache_l0_ckv, cache_l0_krope, cache_l1_k, cache_l1_v, c, c_ctx, ada_w, ada_b, ln1_g, ln1_b, ln2_g, ln2_b, mla_w_dq, mla_q_norm, mla_w_uq, mla_w_dkv, mla_kv_norm, mla_w_uk, mla_w_uv, mla_w_o, gqa_w_qkv, gqa_sink, gqa_w_o, fnet_w_out, conv_w_in, conv_w, conv_b, conv_w_out, peer_w_q, peer_sub_keys, peer_u, peer_v):
    x = jnp.concatenate([x_prompt.reshape(N_PROMPT, D_MODEL), x_sample.reshape(N_SAMPLE, D_MODEL)], axis=0)
    cond = jnp.concatenate([c_ctx[None, :], c, jnp.zeros((N_COND - 1 - DEC_BATCH, D_MODEL), F32)], axis=0)
    mods_all = ada_modulation_all(cond, ada_w, ada_b).reshape(DEPTH, N_COND, 6, 1, D_MODEL)

    u = modulate_tokens(x, mods_all[0])
    new_ckv = new_krope = new_kv = None
    for i in range(DEPTH):
        mods = mods_all[i]
        if i == 0:
            delta, new_ckv, new_krope = mla_mixer(u, cache_l0_ckv, cache_l0_krope, mla_w_dq, mla_q_norm, mla_w_uq,
                                                  mla_w_dkv, mla_kv_norm, mla_w_uk, mla_w_uv, mla_w_o)
        elif i == 1:
            delta, new_kv = gqa_mixer(u, cache_l1_k, cache_l1_v, gqa_w_qkv, gqa_sink, gqa_w_o)
        elif i == 2:
            delta = fnet_mixer(u, fnet_w_out)
        else:
            delta = conv_mixer(u, conv_w_in, conv_w, conv_b, conv_w_out)
        x, u = post_norm(x, delta, mods, 2, ln1_g[i], ln1_b[i], mods, 3)
        delta = peer_ffn(u, peer_w_q[i], peer_sub_keys[i], peer_u[i], peer_v[i])
        if i + 1 < DEPTH:
            x, u = post_norm(x, delta, mods, 5, ln2_g[i], ln2_b[i], mods_all[i + 1], 0)
        else:
            x = post_norm(x, delta, mods, 5, ln2_g[i], ln2_b[i])

    nk = GQA_KV_HEADS * GQA_HEAD_DIM
    return (x[:N_PROMPT].reshape(BATCH, SEQ, D_MODEL),
            x[N_PROMPT:].reshape(DEC_BATCH, DEC_SEQ, D_MODEL),
            new_ckv[:N_PROMPT].reshape(BATCH, SEQ, MLA_KV_LORA),
            new_krope[:N_PROMPT].reshape(BATCH, SEQ, MLA_ROPE),
            new_kv[:N_PROMPT, :nk].reshape(BATCH, SEQ, GQA_KV_HEADS, GQA_HEAD_DIM),
            new_kv[:N_PROMPT, nk:].reshape(BATCH, SEQ, GQA_KV_HEADS, GQA_HEAD_DIM))
```

```python
import functools
import math

import numpy as np
import jax
import jax.numpy as jnp
from jax import lax
from jax.experimental import pallas as pl
from jax.experimental.pallas import tpu as pltpu

F32 = jnp.float32
BF16 = jnp.bfloat16

D_MODEL = 2048
BATCH = 16
SEQ = 256
DEPTH = 4
DEC_BATCH = 2
DEC_SEQ = 1024
PAST_LEN = 256
GRID_W = 64
ROPE_BASE = 10000.0
LN_EPS = 1e-5
RMS_EPS = 1e-6
NEG_INF = -1e30
DEEPNORM_ALPHA = (2 * DEPTH) ** 0.25

MLA_HEADS = 16
MLA_Q_LORA = 512
MLA_KV_LORA = 512
MLA_NOPE = 128
MLA_ROPE = 64
MLA_V = 128
MLA_SCALE = (MLA_NOPE + MLA_ROPE) ** -0.5

GQA_HEADS = 32
GQA_KV_HEADS = 8
GQA_HEAD_DIM = 64
WINDOW = 128
GQA_SCALE = GQA_HEAD_DIM ** -0.5

FNET_GROUPS = 4
FNET_GROUP_DIM = D_MODEL // FNET_GROUPS

PEER_HEADS = 8
PEER_N_KEYS = 128
PEER_N_EXPERTS = PEER_N_KEYS * PEER_N_KEYS
PEER_HALF = 128
PEER_TOPK = 16

N_PROMPT = BATCH * SEQ
N_SAMPLE = DEC_BATCH * DEC_SEQ
N_TOK = N_PROMPT + N_SAMPLE
N_COND = 8

MIB = 1024 * 1024

NT_DIMS = (((1,), (1,)), ((), ()))
TN_DIMS = (((0,), (0,)), ((), ()))


def _params(sem, vmem_mib, flags=None):
    return pltpu.CompilerParams(dimension_semantics=sem, vmem_limit_bytes=vmem_mib * MIB, flags=flags)


def _group_of_tile(i, tm):
    return jnp.maximum((i * tm) // DEC_SEQ - (N_PROMPT // DEC_SEQ - 1), 0)


def _mod_spec(tm, which):
    return pl.BlockSpec((None, None, 1, D_MODEL), lambda i: (_group_of_tile(i, tm), which, 0, 0))


def _ada_kernel(c_ref, w_ref, b_ref, o_ref):
    c = c_ref[...]
    s = (c * jax.nn.sigmoid(c)).astype(BF16)
    o_ref[...] = jnp.dot(s, w_ref[...].astype(BF16), preferred_element_type=F32) + b_ref[...]


def ada_modulation_all(cond, ada_w, ada_b):
    tn = 1024
    n_out = 6 * D_MODEL
    return pl.pallas_call(
        _ada_kernel,
        out_shape=jax.ShapeDtypeStruct((DEPTH, N_COND, n_out), F32),
        grid=(DEPTH, n_out // tn),
        in_specs=[
            pl.BlockSpec((N_COND, D_MODEL), lambda l, j: (0, 0)),
            pl.BlockSpec((None, D_MODEL, tn), lambda l, j: (l, 0, j)),
            pl.BlockSpec((None, 1, tn), lambda l, j: (l, 0, j)),
        ],
        out_specs=pl.BlockSpec((None, N_COND, tn), lambda l, j: (l, 0, j)),
        compiler_params=_params(("parallel", "parallel"), 40),
        name="ada_modulation",
    )(cond, ada_w, ada_b.reshape(DEPTH, 1, n_out))


def _modulate_kernel(x_ref, sh_ref, sc_ref, u_ref):
    u_ref[...] = (x_ref[...] * (1.0 + sc_ref[...]) + sh_ref[...]).astype(u_ref.dtype)


def modulate_tokens(x, mods):
    tm = 256
    return pl.pallas_call(
        _modulate_kernel,
        out_shape=jax.ShapeDtypeStruct((N_TOK, D_MODEL), BF16),
        grid=(N_TOK // tm,),
        in_specs=[pl.BlockSpec((tm, D_MODEL), lambda i: (i, 0)), _mod_spec(tm, 0), _mod_spec(tm, 1)],
        out_specs=pl.BlockSpec((tm, D_MODEL), lambda i: (i, 0)),
        compiler_params=_params(("parallel",), 24),
        name="modulate",
    )(x, mods, mods)


def _layer_norm_rows(y, g, b):
    mu = jnp.mean(y, axis=-1, keepdims=True)
    yc = y - mu
    var = jnp.mean(yc * yc, axis=-1, keepdims=True)
    return yc * lax.rsqrt(var + LN_EPS) * g + b


def _post_norm_kernel(x_ref, d_ref, gate_ref, g_ref, b_ref, xo_ref):
    y = DEEPNORM_ALPHA * x_ref[...] + gate_ref[...] * d_ref[...].astype(F32)
    xo_ref[...] = _layer_norm_rows(y, g_ref[...], b_ref[...])


def _post_norm_mod_kernel(x_ref, d_ref, gate_ref, g_ref, b_ref, sh_ref, sc_ref, xo_ref, uo_ref):
    y = DEEPNORM_ALPHA * x_ref[...] + gate_ref[...] * d_ref[...].astype(F32)
    xn = _layer_norm_rows(y, g_ref[...], b_ref[...])
    xo_ref[...] = xn
    uo_ref[...] = (xn * (1.0 + sc_ref[...]) + sh_ref[...]).astype(uo_ref.dtype)


def post_norm(x, delta, mods, gate_idx, g, b, next_mods=None, next_idx=None):
    tm = 256
    row = pl.BlockSpec((tm, D_MODEL), lambda i: (i, 0))
    vec = pl.BlockSpec((1, D_MODEL), lambda i: (0, 0))
    ins = [x, delta, mods, g.reshape(1, D_MODEL), b.reshape(1, D_MODEL)]
    specs = [row, row, _mod_spec(tm, gate_idx), vec, vec]
    if next_mods is None:
        return pl.pallas_call(
            _post_norm_kernel,
            out_shape=jax.ShapeDtypeStruct((N_TOK, D_MODEL), F32),
            grid=(N_TOK // tm,), in_specs=specs, out_specs=row,
            compiler_params=_params(("parallel",), 32), name="post_norm",
        )(*ins)
    ins += [next_mods, next_mods]
    specs += [_mod_spec(tm, next_idx), _mod_spec(tm, next_idx + 1)]
    return pl.pallas_call(
        _post_norm_mod_kernel,
        out_shape=(jax.ShapeDtypeStruct((N_TOK, D_MODEL), F32), jax.ShapeDtypeStruct((N_TOK, D_MODEL), BF16)),
        grid=(N_TOK // tm,), in_specs=specs, out_specs=(row, row),
        compiler_params=_params(("parallel",), 32), name="post_norm_mod",
    )(*ins)


def _mm_kernel(x_ref, w_ref, o_ref):
    o_ref[...] = jnp.dot(x_ref[...], w_ref[...].astype(BF16), preferred_element_type=F32).astype(o_ref.dtype)


def matmul(x, w, out_dtype, tm=1024, tn=512, name="matmul"):
    m, k = x.shape
    n = w.shape[1]
    tm = min(tm, m)
    tn = min(tn, n)
    assert m % tm == 0 and n % tn == 0, (m, n, tm, tn)
    return pl.pallas_call(
        _mm_kernel,
        out_shape=jax.ShapeDtypeStruct((m, n), out_dtype),
        grid=(m // tm, n // tn),
        in_specs=[pl.BlockSpec((tm, k), lambda i, j: (i, 0)), pl.BlockSpec((k, tn), lambda i, j: (0, j))],
        out_specs=pl.BlockSpec((tm, tn), lambda i, j: (i, j)),
        compiler_params=_params(("parallel", "parallel"), 40),
        name=name,
    )(x, w)


def _rms_rows(y, g):
    return y * lax.rsqrt(jnp.mean(y * y, axis=-1, keepdims=True) + RMS_EPS) * g


def _mla_down_kernel(x_ref, w_ref, qn_ref, kvn_ref, cq_ref, ckv_ref, kr_ref):
    y = jnp.dot(x_ref[...], w_ref[...].astype(BF16), preferred_element_type=F32)
    cq_ref[...] = _rms_rows(y[:, :MLA_Q_LORA], qn_ref[...]).astype(cq_ref.dtype)
    ckv_ref[...] = _rms_rows(y[:, MLA_Q_LORA:MLA_Q_LORA + MLA_KV_LORA], kvn_ref[...])
    kr_ref[...] = y[:, MLA_Q_LORA + MLA_KV_LORA:MLA_Q_LORA + MLA_KV_LORA + MLA_ROPE]


def mla_down(u, w_cat, q_norm, kv_norm):
    tm = 512
    n = w_cat.shape[1]
    return pl.pallas_call(
        _mla_down_kernel,
        out_shape=(jax.ShapeDtypeStruct((N_TOK, MLA_Q_LORA), BF16),
                   jax.ShapeDtypeStruct((N_TOK, MLA_KV_LORA), F32),
                   jax.ShapeDtypeStruct((N_TOK, MLA_ROPE), F32)),
        grid=(N_TOK // tm,),
        in_specs=[pl.BlockSpec((tm, D_MODEL), lambda i: (i, 0)),
                  pl.BlockSpec((D_MODEL, n), lambda i: (0, 0)),
                  pl.BlockSpec((1, MLA_Q_LORA), lambda i: (0, 0)),
                  pl.BlockSpec((1, MLA_KV_LORA), lambda i: (0, 0))],
        out_specs=(pl.BlockSpec((tm, MLA_Q_LORA), lambda i: (i, 0)),
                   pl.BlockSpec((tm, MLA_KV_LORA), lambda i: (i, 0)),
                   pl.BlockSpec((tm, MLA_ROPE), lambda i: (i, 0))),
        compiler_params=_params(("parallel",), 48),
        name="mla_down",
    )(u, w_cat, q_norm.reshape(1, -1), kv_norm.reshape(1, -1))


def _rope_tables():
    t = np.arange(DEC_SEQ)
    quarter = MLA_ROPE // 4
    inv_freq = ROPE_BASE ** (-np.arange(quarter, dtype=np.float64) / quarter)
    ang_row = (t // GRID_W)[:, None] * inv_freq[None, :]
    ang_col = (t % GRID_W)[:, None] * inv_freq[None, :]
    cos = np.concatenate([np.cos(ang_row)] * 2 + [np.cos(ang_col)] * 2, -1)
    sin = np.concatenate([-np.sin(ang_row), np.sin(ang_row), -np.sin(ang_col), np.sin(ang_col)], -1)
    return np.tile(cos, (1, 2)), np.tile(sin, (1, 2))


def _rope_kernel(x_ref, c_ref, s_ref, o_ref):
    x = x_ref[...].astype(F32)
    w = x.shape[-1]
    reps = w // c_ref.shape[-1]
    cos = jnp.tile(c_ref[...], (1, reps))
    sin = jnp.tile(s_ref[...], (1, reps))
    lane = lax.broadcasted_iota(jnp.int32, x.shape, 1)
    partner = jnp.where((lane % 32) < 16, pltpu.roll(x, w - 16, 1), pltpu.roll(x, 16, 1))
    o_ref[...] = (x * cos + partner * sin).astype(o_ref.dtype)


def rope_rows(x, row_block0, col_block, width, n_rows, cos, sin, out_dtype, tr=256):
    t_blocks = cos.shape[0] // tr
    return pl.pallas_call(
        _rope_kernel,
        out_shape=jax.ShapeDtypeStruct((n_rows, width), out_dtype),
        grid=(n_rows // tr,),
        in_specs=[pl.BlockSpec((tr, width), lambda i: (row_block0 + i, col_block)),
                  pl.BlockSpec((tr, 128), lambda i: (i % t_blocks, 0)),
                  pl.BlockSpec((tr, 128), lambda i: (i % t_blocks, 0))],
        out_specs=pl.BlockSpec((tr, width), lambda i: (i, 0)),
        compiler_params=_params(("parallel",), 32),
        name="axial_rope",
    )(x, cos, sin)


def _softmax_parts(parts, sink):
    m = parts[0].max(axis=-1, keepdims=True)
    for s in parts[1:]:
        m = jnp.maximum(m, s.max(axis=-1, keepdims=True))
    if sink is not None:
        m = jnp.maximum(m, sink)
    ps = [jnp.exp(s - m) for s in parts]
    l = ps[0].sum(axis=-1, keepdims=True)
    for p in ps[1:]:
        l = l + p.sum(axis=-1, keepdims=True)
    if sink is not None:
        l = l + jnp.exp(sink - m)
    return ps, 1.0 / l


def _mla_attn_kernel(*refs, with_ctx):
    if with_ctx:
        qn_ref, qr_ref, kn_ref, v_ref, kr_ref, kn2_ref, v2_ref, kr2_ref, o_ref = refs
    else:
        qn_ref, qr_ref, kn_ref, v_ref, kr_ref, o_ref = refs
    kr = kr_ref[...].astype(BF16)
    kr2 = kr2_ref[...].astype(BF16) if with_ctx else None
    for h in range(MLA_HEADS):
        n0, n1 = h * MLA_NOPE, (h + 1) * MLA_NOPE
        qn = qn_ref[:, n0:n1]
        qr = qr_ref[:, h * MLA_ROPE:(h + 1) * MLA_ROPE]
        s = (lax.dot_general(qn, kn_ref[:, n0:n1], NT_DIMS, preferred_element_type=F32)
             + lax.dot_general(qr, kr, NT_DIMS, preferred_element_type=F32)) * MLA_SCALE
        parts = [s]
        if with_ctx:
            s2 = (lax.dot_general(qn, kn2_ref[:, n0:n1], NT_DIMS, preferred_element_type=F32)
                  + lax.dot_general(qr, kr2, NT_DIMS, preferred_element_type=F32)) * MLA_SCALE
            parts.append(s2)
        ps, inv_l = _softmax_parts(parts, None)
        o = jnp.dot(ps[0].astype(BF16), v_ref[:, n0:n1], preferred_element_type=F32)
        if with_ctx:
            o = o + jnp.dot(ps[1].astype(BF16), v2_ref[:, n0:n1], preferred_element_type=F32)
        o_ref[:, n0:n1] = (o * inv_l).astype(o_ref.dtype)


def mla_attention_prompt(q, kv_up, kr):
    w = MLA_HEADS * MLA_NOPE
    return pl.pallas_call(
        functools.partial(_mla_attn_kernel, with_ctx=False),
        out_shape=jax.ShapeDtypeStruct((N_PROMPT, w), BF16),
        grid=(BATCH,),
        in_specs=[pl.BlockSpec((SEQ, w), lambda b: (b, 0)),
                  pl.BlockSpec((SEQ, MLA_HEADS * MLA_ROPE), lambda b: (b, w // (MLA_HEADS * MLA_ROPE))),
                  pl.BlockSpec((SEQ, w), lambda b: (b, 0)),
                  pl.BlockSpec((SEQ, w), lambda b: (b, 1)),
                  pl.BlockSpec((SEQ, MLA_ROPE), lambda b: (b, 0))],
        out_specs=pl.BlockSpec((SEQ, w), lambda b: (b, 0)),
        compiler_params=_params(("parallel",), 32),
        name="mla_attention_prompt",
    )(q, q, kv_up, kv_up, kr)


def mla_attention_sample(qn, qr_rot, kv_up, kr_rot, kr_cache):
    w = MLA_HEADS * MLA_NOPE
    tq = 256
    nq = DEC_SEQ // tq
    q0 = N_PROMPT // tq
    lat0 = N_PROMPT // DEC_SEQ
    ctx0 = N_TOK // PAST_LEN
    return pl.pallas_call(
        functools.partial(_mla_attn_kernel, with_ctx=True),
        out_shape=jax.ShapeDtypeStruct((N_SAMPLE, w), BF16),
        grid=(DEC_BATCH, nq),
        in_specs=[pl.BlockSpec((tq, w), lambda b, i: (q0 + b * nq + i, 0)),
                  pl.BlockSpec((tq, MLA_HEADS * MLA_ROPE), lambda b, i: (b * nq + i, 0)),
                  pl.BlockSpec((DEC_SEQ, w), lambda b, i: (lat0 + b, 0)),
                  pl.BlockSpec((DEC_SEQ, w), lambda b, i: (lat0 + b, 1)),
                  pl.BlockSpec((DEC_SEQ, MLA_ROPE), lambda b, i: (b, 0)),
                  pl.BlockSpec((PAST_LEN, w), lambda b, i: (ctx0 + b, 0)),
                  pl.BlockSpec((PAST_LEN, w), lambda b, i: (ctx0 + b, 1)),
                  pl.BlockSpec((PAST_LEN, MLA_ROPE), lambda b, i: (b, 0))],
        out_specs=pl.BlockSpec((tq, w), lambda b, i: (b * nq + i, 0)),
        compiler_params=_params(("parallel", "parallel"), 56),
        name="mla_attention_sample",
    )(qn, qr_rot, kv_up, kv_up, kr_rot, kv_up, kv_up, kr_cache)


def _gqa_attn_kernel(*refs, with_ctx, tq):
    if with_ctx:
        sink_ref, q_ref, k_ref, v_ref, k2_ref, v2_ref, o_ref = refs
    else:
        sink_ref, q_ref, k_ref, v_ref, o_ref = refs
    group = GQA_HEADS // GQA_KV_HEADS
    d = GQA_HEAD_DIM
    if with_ctx:
        t = pl.program_id(1) * tq + lax.broadcasted_iota(jnp.int32, (tq, DEC_SEQ), 0)
        s_pos = lax.broadcasted_iota(jnp.int32, (tq, DEC_SEQ), 1)
        in_window = jnp.abs(t - s_pos) <= WINDOW
    for hk in range(GQA_KV_HEADS):
        k = k_ref[:, hk * d:(hk + 1) * d].astype(BF16)
        v = v_ref[:, hk * d:(hk + 1) * d].astype(BF16)
        if with_ctx:
            k2 = k2_ref[:, hk * d:(hk + 1) * d].astype(BF16)
            v2 = v2_ref[:, hk * d:(hk + 1) * d].astype(BF16)
        for g in range(group):
            h = hk * group + g
            q = q_ref[:, h * d:(h + 1) * d]
            s = lax.dot_general(q, k, NT_DIMS, preferred_element_type=F32) * GQA_SCALE
            parts = [s]
            if with_ctx:
                parts = [jnp.where(in_window, s, NEG_INF),
                         lax.dot_general(q, k2, NT_DIMS, preferred_element_type=F32) * GQA_SCALE]
            ps, inv_l = _softmax_parts(parts, sink_ref[h])
            o = jnp.dot(ps[0].astype(BF16), v, preferred_element_type=F32)
            if with_ctx:
                o = o + jnp.dot(ps[1].astype(BF16), v2, preferred_element_type=F32)
            o_ref[:, h * d:(h + 1) * d] = (o * inv_l).astype(o_ref.dtype)


def gqa_attention_prompt(q, kv, sink):
    wq = GQA_HEADS * GQA_HEAD_DIM
    wk = GQA_KV_HEADS * GQA_HEAD_DIM
    return pl.pallas_call(
        functools.partial(_gqa_attn_kernel, with_ctx=False, tq=SEQ),
        out_shape=jax.ShapeDtypeStruct((N_PROMPT, wq), BF16),
        grid=(BATCH,),
        in_specs=[pl.BlockSpec(memory_space=pltpu.SMEM),
                  pl.BlockSpec((SEQ, wq), lambda b: (b, 0)),
                  pl.BlockSpec((SEQ, wk), lambda b: (b, 0)),
                  pl.BlockSpec((SEQ, wk), lambda b: (b, 1))],
        out_specs=pl.BlockSpec((SEQ, wq), lambda b: (b, 0)),
        compiler_params=_params(("parallel",), 32),
        name="gqa_attention_prompt",
    )(sink, q, kv, kv)


def gqa_attention_sample(q_rot, k_rot, kv, k_cache, v_cache, sink):
    wq = GQA_HEADS * GQA_HEAD_DIM
    wk = GQA_KV_HEADS * GQA_HEAD_DIM
    tq = 256
    nq = DEC_SEQ // tq
    lat0 = N_PROMPT // DEC_SEQ
    return pl.pallas_call(
        functools.partial(_gqa_attn_kernel, with_ctx=True, tq=tq),
        out_shape=jax.ShapeDtypeStruct((N_SAMPLE, wq), BF16),
        grid=(DEC_BATCH, nq),
        in_specs=[pl.BlockSpec(memory_space=pltpu.SMEM),
                  pl.BlockSpec((tq, wq), lambda b, i: (b * nq + i, 0)),
                  pl.BlockSpec((DEC_SEQ, wk), lambda b, i: (b, 0)),
                  pl.BlockSpec((DEC_SEQ, wk), lambda b, i: (lat0 + b, 1)),
                  pl.BlockSpec((PAST_LEN, wk), lambda b, i: (b, 0)),
                  pl.BlockSpec((PAST_LEN, wk), lambda b, i: (b, 0))],
        out_specs=pl.BlockSpec((tq, wq), lambda b, i: (b * nq + i, 0)),
        compiler_params=_params(("parallel", "parallel"), 56),
        name="gqa_attention_sample",
    )(sink, q_rot, k_rot, kv, k_cache, v_cache)


def _dft_tables(n):
    jk = (np.arange(n)[:, None] * np.arange(n)[None, :]) % n
    ang = 2.0 * np.pi * jk / n
    return np.cos(ang) / math.sqrt(n), np.sin(ang) / math.sqrt(n)


def _fnet_kernel(x_ref, ct_ref, st_ref, cc_ref, sc_ref, o_ref):
    x = x_ref[...]
    y1 = jnp.dot(x, cc_ref[...], preferred_element_type=F32).astype(BF16)
    y2 = jnp.dot(x, sc_ref[...], preferred_element_type=F32).astype(BF16)
    o = (jnp.dot(ct_ref[...], y1, preferred_element_type=F32)
         - jnp.dot(st_ref[...], y2, preferred_element_type=F32))
    o_ref[...] = o.astype(o_ref.dtype)


def fourier_real_2d(u, seq_block0, n_seq, t):
    ct, st = (jnp.asarray(a, BF16) for a in _dft_tables(t))
    cc, sc = (jnp.asarray(a, BF16) for a in _dft_tables(FNET_GROUP_DIM))
    gd = FNET_GROUP_DIM
    return pl.pallas_call(
        _fnet_kernel,
        out_shape=jax.ShapeDtypeStruct((n_seq * t, D_MODEL), BF16),
        grid=(n_seq, FNET_GROUPS),
        in_specs=[pl.BlockSpec((t, gd), lambda b, g: (seq_block0 + b, g)),
                  pl.BlockSpec((t, t), lambda b, g: (0, 0)),
                  pl.BlockSpec((t, t), lambda b, g: (0, 0)),
                  pl.BlockSpec((gd, gd), lambda b, g: (0, 0)),
                  pl.BlockSpec((gd, gd), lambda b, g: (0, 0))],
        out_specs=pl.BlockSpec((t, gd), lambda b, g: (b, g)),
        compiler_params=_params(("parallel", "parallel"), 32),
        name="fourier_mix",
    )(u, ct, st, cc, sc)


def _conv_kernel(b_ref, c_ref, h_ref, w_ref, cb_ref, o_ref):
    z = c_ref[...].astype(F32) * h_ref[...].astype(F32)
    t = z.shape[0]
    row = lax.broadcasted_iota(jnp.int32, z.shape, 0)
    z_prev = jnp.where(row == 0, 0.0, pltpu.roll(z, 1, 0))
    z_next = jnp.where(row == t - 1, 0.0, pltpu.roll(z, t - 1, 0))
    conv = z_prev * w_ref[0:1, :] + z * w_ref[1:2, :] + z_next * w_ref[2:3, :] + cb_ref[...]
    o_ref[...] = (b_ref[...].astype(F32) * conv).astype(o_ref.dtype)


def gated_conv(h3, conv_w, conv_b, seq_block0, n_seq, t):
    tn = 512
    nj = D_MODEL // tn
    return pl.pallas_call(
        _conv_kernel,
        out_shape=jax.ShapeDtypeStruct((n_seq * t, D_MODEL), BF16),
        grid=(n_seq, nj),
        in_specs=[pl.BlockSpec((t, tn), lambda s, j: (seq_block0 + s, j)),
                  pl.BlockSpec((t, tn), lambda s, j: (seq_block0 + s, nj + j)),
                  pl.BlockSpec((t, tn), lambda s, j: (seq_block0 + s, 2 * nj + j)),
                  pl.BlockSpec((3, tn), lambda s, j: (0, j)),
                  pl.BlockSpec((1, tn), lambda s, j: (0, j))],
        out_specs=pl.BlockSpec((t, tn), lambda s, j: (s, j)),
        compiler_params=_params(("parallel", "parallel"), 32),
        name="gated_conv",
    )(h3, h3, h3, conv_w, conv_b.reshape(1, D_MODEL))


_N_RANKS = PEER_TOPK + 1
_CAND_PAIRS = [(a, b) for a in range(_N_RANKS) for b in range(_N_RANKS) if (a + 1) * (b + 1) <= _N_RANKS]


_NO_RANK = 127.0


def _top_values_exact(s, k, want_rank):
    n = s.shape[0]
    idx = lax.broadcasted_iota(jnp.int32, s.shape, 0)
    rank = jnp.full(s.shape, _NO_RANK, F32) if want_rank else None
    out = []
    for it in range(k):
        m = jnp.max(s, axis=0, keepdims=True)
        out.append(m)
        first = jnp.min(jnp.where(s == m, idx, n), axis=0, keepdims=True)
        hit = idx == first
        s = jnp.where(hit, -jnp.inf, s)
        if want_rank:
            rank = jnp.where(hit, float(it), rank)
    return out, rank, None


def _top_values_distinct(s, k, want_rank):
    rank = jnp.full(s.shape, _NO_RANK, F32) if want_rank else None
    out = []
    for it in range(k):
        m = jnp.max(s, axis=0, keepdims=True)
        out.append(m)
        hit = s == m
        s = jnp.where(hit, -jnp.inf, s)
        if want_rank:
            rank = jnp.where(hit, float(it), rank)
    removed = jnp.sum(jnp.where(s == -jnp.inf, 1.0, 0.0), axis=0, keepdims=True)
    return out, rank, jnp.min(jnp.where(removed == float(k), 1.0, 0.0)) > 0.5


def _dup_bf16_words(x):
    bits = pltpu.bitcast(x.astype(BF16).astype(F32), jnp.int32)
    return bits | lax.shift_right_logical(bits, jnp.int32(16))


def _route_head(h, s1, s2, top_values, bq_ref, r2_ref, aw_ref, cw_ref):
    tm = s1.shape[1]
    v1, _, ok1 = top_values(s1, _N_RANKS, False)
    v2, rank2, ok2 = top_values(s2, _N_RANKS, True)
    n_cand = len(_CAND_PAIRS) + (-len(_CAND_PAIRS)) % 8
    cand_row = lax.broadcasted_iota(jnp.int32, (n_cand, tm), 0)
    cand = jnp.full((n_cand, tm), -jnp.inf, F32)
    for r, (a, b) in enumerate(_CAND_PAIRS):
        cand = jnp.where(cand_row == r, v1[a] + v2[b], cand)
    top, _, _ = _top_values_exact(cand, _N_RANKS, False)
    tau = 0.5 * (top[PEER_TOPK - 1] + top[PEER_TOPK])
    z = jnp.exp(top[0] - top[0])
    for kk in range(1, PEER_TOPK):
        z = z + jnp.exp(top[kk] - top[0])
    inv_z = 1.0 / z
    count = jnp.zeros_like(s1)
    for b in range(PEER_TOPK):
        count = count + jnp.where(s1 >= tau - v2[b], 1.0, 0.0)
    rows = slice(h * PEER_N_KEYS, (h + 1) * PEER_N_KEYS)
    bq_ref[rows, :] = (jnp.exp(s2 - v2[0]) * inv_z).astype(BF16)
    r2_ref[rows, :] = rank2.astype(BF16)
    a_half = 0.5 * jnp.exp(s1 - v1[0])
    aw_ref[:, h, :, :] = _dup_bf16_words(a_half).reshape(PEER_N_KEYS // 8, 8, tm)
    cw_ref[:, h, :, :] = _dup_bf16_words(count).reshape(PEER_N_KEYS // 8, 8, tm)
    return None if ok1 is None else jnp.logical_and(ok1, ok2)


def _route_kernel(q_ref, sk_ref, bq_ref, r2_ref, aw_ref, cw_ref):
    outs = (bq_ref, r2_ref, aw_ref, cw_ref)
    for h in range(PEER_HEADS):
        s = []
        for c in range(2):
            hc = 2 * h + c
            s.append(lax.dot_general(sk_ref[hc].astype(BF16), q_ref[:, hc * PEER_HALF:(hc + 1) * PEER_HALF],
                                     NT_DIMS, preferred_element_type=F32))
        distinct = _route_head(h, s[0], s[1], _top_values_distinct, *outs)

        @pl.when(jnp.logical_not(distinct))
        def _():
            _route_head(h, s[0], s[1], _top_values_exact, *outs)


def peer_route(q, sub_keys):
    tm = 256
    n_tok = q.shape[0]
    sk = sub_keys.reshape(PEER_HEADS * 2, PEER_N_KEYS, PEER_HALF)
    tiles = PEER_N_KEYS // 8
    col_out = jax.ShapeDtypeStruct((PEER_HEADS * PEER_N_KEYS, n_tok), BF16)
    col_spec = pl.BlockSpec((PEER_HEADS * PEER_N_KEYS, tm), lambda i: (0, i))
    row_out = jax.ShapeDtypeStruct((tiles, PEER_HEADS, 8, n_tok), jnp.int32)
    row_spec = pl.BlockSpec((tiles, PEER_HEADS, 8, tm), lambda i: (0, 0, 0, i))
    return pl.pallas_call(
        _route_kernel,
        out_shape=(col_out, col_out, row_out, row_out),
        grid=(n_tok // tm,),
        in_specs=[pl.BlockSpec((tm, PEER_HEADS * 2 * PEER_HALF), lambda i: (i, 0)),
                  pl.BlockSpec((PEER_HEADS * 2, PEER_N_KEYS, PEER_HALF), lambda i: (0, 0, 0))],
        out_specs=(col_spec, col_spec, row_spec, row_spec),
        compiler_params=_params(("parallel",), 32),
        name="peer_route",
    )(q, sk)


PEER_TE = 512
PEER_TM = 1024
PEER_CHUNK = 512
PEER_N_TILES = PEER_N_EXPERTS // PEER_TE


def _bf16_rows(word_row, n_rows):
    return pltpu.bitcast(jnp.broadcast_to(word_row, (n_rows // 2, word_row.shape[1])), BF16)


PEER_LANES = 256
PEER_ROWS = 128


def _peer_gated_slab(k, k0, bq_ref, r2_ref, aw_ref, cw_ref, ht_ref, at_ref):
    for j in range(ht_ref.shape[1] // PEER_LANES):
        lanes = slice(j * PEER_LANES, (j + 1) * PEER_LANES)
        for r in range(PEER_N_KEYS // PEER_ROWS):
            g = [None, None]
            for h in range(PEER_HEADS):
                count = _bf16_rows(cw_ref[0, h, k0 + k:k0 + k + 1, lanes], PEER_ROWS)
                a_half = _bf16_rows(aw_ref[0, h, k0 + k:k0 + k + 1, lanes], PEER_ROWS)
                rows = slice(h * PEER_N_KEYS + r * PEER_ROWS, h * PEER_N_KEYS + (r + 1) * PEER_ROWS)
                bq = bq_ref[rows, lanes]
                contrib = jnp.where(r2_ref[rows, lanes] < count, bq, jnp.zeros_like(bq)) * a_half
                g[h % 2] = contrib if g[h % 2] is None else g[h % 2] + contrib
            krows = slice(k * PEER_N_KEYS + r * PEER_ROWS, k * PEER_N_KEYS + (r + 1) * PEER_ROWS)
            hk = ht_ref[krows, lanes]
            gelu2 = hk * (1.0 + lax.erf(hk * math.sqrt(0.5)))
            at_ref[krows, lanes] = gelu2.astype(BF16) * (g[0] + g[1])


def _peer_step(x_ref, bq_ref, r2_ref, aw_ref, cw_ref, u_ref, v_ref, o_ref, at_ref, ht_write, ht_read, k0):
    keys_per_tile = PEER_TE // PEER_N_KEYS
    keys_per_chunk = PEER_CHUNK // PEER_N_KEYS
    tm = x_ref.shape[0]
    tok = tm // keys_per_tile
    u = u_ref[...].astype(BF16) if ht_write is not None else None
    for k in range(keys_per_tile):
        if ht_read is not None:
            _peer_gated_slab(k, k0, bq_ref, r2_ref, aw_ref, cw_ref, ht_read, at_ref)
        if ht_write is not None:
            ht_write[:, k * tok:(k + 1) * tok] = lax.dot_general(
                u, x_ref[k * tok:(k + 1) * tok, :], NT_DIMS, preferred_element_type=F32)
        if ht_read is not None and (k + 1) % keys_per_chunk == 0:
            chunk = slice((k + 1) * PEER_N_KEYS - PEER_CHUNK, (k + 1) * PEER_N_KEYS)
            o_ref[...] += lax.dot_general(at_ref[chunk, :], v_ref[chunk, :].astype(BF16), TN_DIMS,
                                          preferred_element_type=F32)


def _peer_kernel(x_ref, bq_ref, r2_ref, aw_ref, cw_ref, u_ref, v_ref, o_ref, ht0_ref, ht1_ref, at_ref):
    e = pl.program_id(1)
    last = PEER_N_TILES
    keys_per_tile = PEER_TE // PEER_N_KEYS
    assert 8 // keys_per_tile == 2
    step = functools.partial(_peer_step, x_ref, bq_ref, r2_ref, aw_ref, cw_ref, u_ref, v_ref, o_ref, at_ref)
    even = e % 2 == 0

    @pl.when(e == 0)
    def _():
        o_ref[...] = jnp.zeros_like(o_ref)
        step(ht0_ref, None, None)

    @pl.when(jnp.logical_not(even))
    def _():
        step(ht1_ref, ht0_ref, 0)

    @pl.when(even & (e > 0) & (e < last))
    def _():
        step(ht0_ref, ht1_ref, keys_per_tile)

    @pl.when(e == last)
    def _():
        step(None, ht1_ref, keys_per_tile)


def peer_experts(u, bq, r2, aw, cw, exp_u, exp_v, layer):
    tm, te = PEER_TM, PEER_TE
    n_tok = u.shape[0]
    assert PEER_N_TILES % 2 == 0 and n_tok % tm == 0
    tiles_per_row_block = 8 * PEER_N_KEYS // te
    once = pl.Buffered(1)
    col_spec = pl.BlockSpec((PEER_HEADS * PEER_N_KEYS, tm), lambda i, e: (0, i), pipeline_mode=once)
    row_spec = pl.BlockSpec((1, PEER_HEADS, 8, tm),
                            lambda i, e: (jnp.maximum(e - 1, 0) // tiles_per_row_block, 0, 0, i))
    return pl.pallas_call(
        _peer_kernel,
        out_shape=jax.ShapeDtypeStruct((n_tok, D_MODEL), F32),
        grid=(n_tok // tm, PEER_N_TILES + 1),
        in_specs=[pl.BlockSpec((tm, D_MODEL), lambda i, e: (i, 0), pipeline_mode=once),
                  col_spec, col_spec,
                  row_spec, row_spec,
                  pl.BlockSpec((None, te, D_MODEL), lambda i, e: (layer, jnp.minimum(e, PEER_N_TILES - 1), 0)),
                  pl.BlockSpec((None, te, D_MODEL), lambda i, e: (layer, jnp.maximum(e - 1, 0), 0))],
        out_specs=pl.BlockSpec((tm, D_MODEL), lambda i, e: (i, 0), pipeline_mode=once),
        scratch_shapes=[pltpu.VMEM((te, tm), F32), pltpu.VMEM((te, tm), F32), pltpu.VMEM((te, tm), BF16)],
        compiler_params=_params(("parallel", "arbitrary"), 56),
        name="peer_experts",
    )(u, bq, r2, aw, cw, exp_u, exp_v)


def peer_ffn(u, w_q, sub_keys, exp_u, exp_v, layer):
    q = matmul(u, w_q, BF16, name="peer_query")
    bq, r2, aw, cw = peer_route(q, sub_keys)
    return peer_experts(u, bq, r2, aw, cw, exp_u, exp_v, layer)


def mla_mixer(u, cache_ckv, cache_krope, w_dq, q_norm, w_uq, w_dkv, kv_norm, w_uk, w_uv, w_o):
    pad = (-(MLA_Q_LORA + MLA_KV_LORA + MLA_ROPE)) % 128
    w_cat = jnp.concatenate([w_dq, w_dkv, jnp.zeros((D_MODEL, pad), F32)], axis=1)
    cq, ckv, kr = mla_down(u, w_cat, q_norm, kv_norm)
    w_uq3 = w_uq.reshape(MLA_Q_LORA, MLA_HEADS, MLA_NOPE + MLA_ROPE)
    w_q_cat = jnp.concatenate([w_uq3[:, :, :MLA_NOPE].reshape(MLA_Q_LORA, -1),
                               w_uq3[:, :, MLA_NOPE:].reshape(MLA_Q_LORA, -1)], axis=1)
    q = matmul(cq, w_q_cat, BF16, name="mla_q_up")
    w_nope = MLA_HEADS * MLA_NOPE
    ckv_src = jnp.concatenate([ckv, cache_ckv.reshape(-1, MLA_KV_LORA)], axis=0).astype(BF16)
    kv_up = matmul(ckv_src, jnp.concatenate([w_uk, w_uv], axis=1), BF16, tm=512, name="mla_kv_up")
    cos, sin = (jnp.asarray(t, F32) for t in _rope_tables())
    w_rope = MLA_HEADS * MLA_ROPE
    qr_rot = rope_rows(q, N_PROMPT // 256, w_nope // w_rope, w_rope, N_SAMPLE, cos, sin, BF16)
    cos2, sin2 = (t.reshape(DEC_SEQ // 2, 128) for t in (cos[:, :MLA_ROPE], sin[:, :MLA_ROPE]))
    kr_pairs = kr[N_PROMPT:].reshape(N_SAMPLE // 2, 2 * MLA_ROPE)
    kr_rot = rope_rows(kr_pairs, 0, 0, 2 * MLA_ROPE, N_SAMPLE // 2, cos2, sin2, BF16).reshape(N_SAMPLE, MLA_ROPE)
    o_p = mla_attention_prompt(q, kv_up, kr)
    o_s = mla_attention_sample(q, qr_rot, kv_up, kr_rot, cache_krope.reshape(-1, MLA_ROPE))
    op = matmul(jnp.concatenate([o_p, o_s], axis=0), w_o, F32, name="mla_out")
    return op, ckv, kr


def gqa_mixer(u, cache_k, cache_v, w_qkv, sink, w_o):
    nq = GQA_HEADS * GQA_HEAD_DIM
    nk = GQA_KV_HEADS * GQA_HEAD_DIM
    q = matmul(u, w_qkv[:, :nq], BF16, name="gqa_q")
    kv = matmul(u, w_qkv[:, nq:], F32, name="gqa_kv")
    cos, sin = (jnp.asarray(t, F32) for t in _rope_tables())
    q_rot = rope_rows(q, N_PROMPT // 256, 0, nq, N_SAMPLE, cos, sin, BF16)
    k_rot = rope_rows(kv, N_PROMPT // 256, 0, nk, N_SAMPLE, cos, sin, BF16)
    o_p = gqa_attention_prompt(q, kv, sink)
    o_s = gqa_attention_sample(q_rot, k_rot, kv, cache_k.reshape(-1, nk), cache_v.reshape(-1, nk), sink)
    op = matmul(jnp.concatenate([o_p, o_s], axis=0), w_o, F32, name="gqa_out")
    return op, kv


def fnet_mixer(u, w_out):
    f_p = fourier_real_2d(u, 0, BATCH, SEQ)
    f_s = fourier_real_2d(u, N_PROMPT // DEC_SEQ, DEC_BATCH, DEC_SEQ)
    return matmul(jnp.concatenate([f_p, f_s], axis=0), w_out, F32, name="fnet_out")


def conv_mixer(u, w_in, conv_w, conv_b, w_out):
    h3 = matmul(u, w_in, BF16, name="conv_in")
    y_p = gated_conv(h3, conv_w, conv_b, 0, BATCH, SEQ)
    y_s = gated_conv(h3, conv_w, conv_b, N_PROMPT // DEC_SEQ, DEC_BATCH, DEC_SEQ)
    return matmul(jnp.concatenate([y_p, y_s], axis=0), w_out, F32, name="conv_out")


def kernel(x_prompt, x_sample, cache_l0_ckv, cache_l0_krope, cache_l1_k, cache_l1_v, c, c_ctx, ada_w, ada_b, ln1_g, ln1_b, ln2_g, ln2_b, mla_w_dq, mla_q_norm, mla_w_uq, mla_w_dkv, mla_kv_norm, mla_w_uk, mla_w_uv, mla_w_o, gqa_w_qkv, gqa_sink, gqa_w_o, fnet_w_out, conv_w_in, conv_w, conv_b, conv_w_out, peer_w_q, peer_sub_keys, peer_u, peer_v):
    x = jnp.concatenate([x_prompt.reshape(N_PROMPT, D_MODEL), x_sample.reshape(N_SAMPLE, D_MODEL)], axis=0)
    cond = jnp.concatenate([c_ctx[None, :], c, jnp.zeros((N_COND - 1 - DEC_BATCH, D_MODEL), F32)], axis=0)
    mods_all = ada_modulation_all(cond, ada_w, ada_b).reshape(DEPTH, N_COND, 6, 1, D_MODEL)

    u = modulate_tokens(x, mods_all[0])
    new_ckv = new_krope = new_kv = None
    for i in range(DEPTH):
        mods = mods_all[i]
        if i == 0:
            delta, new_ckv, new_krope = mla_mixer(u, cache_l0_ckv, cache_l0_krope, mla_w_dq, mla_q_norm, mla_w_uq,
                                                  mla_w_dkv, mla_kv_norm, mla_w_uk, mla_w_uv, mla_w_o)
        elif i == 1:
            delta, new_kv = gqa_mixer(u, cache_l1_k, cache_l1_v, gqa_w_qkv, gqa_sink, gqa_w_o)
        elif i == 2:
            delta = fnet_mixer(u, fnet_w_out)
        else:
            delta = conv_mixer(u, conv_w_in, conv_w, conv_b, conv_w_out)
        x, u = post_norm(x, delta, mods, 2, ln1_g[i], ln1_b[i], mods, 3)
        delta = peer_ffn(u, peer_w_q[i], peer_sub_keys[i], peer_u, peer_v, i)
        if i + 1 < DEPTH:
            x, u = post_norm(x, delta, mods, 5, ln2_g[i], ln2_b[i], mods_all[i + 1], 0)
        else:
            x = post_norm(x, delta, mods, 5, ln2_g[i], ln2_b[i])

    nk = GQA_KV_HEADS * GQA_HEAD_DIM
    return (x[:N_PROMPT].reshape(BATCH, SEQ, D_MODEL),
            x[N_PROMPT:].reshape(DEC_BATCH, DEC_SEQ, D_MODEL),
            new_ckv[:N_PROMPT].reshape(BATCH, SEQ, MLA_KV_LORA),
            new_krope[:N_PROMPT].reshape(BATCH, SEQ, MLA_ROPE),
            new_kv[:N_PROMPT, :nk].reshape(BATCH, SEQ, GQA_KV_HEADS, GQA_HEAD_DIM),
            new_kv[:N_PROMPT, nk:].reshape(BATCH, SEQ, GQA_KV_HEADS, GQA_HEAD_DIM))
```

```python
import functools
import math

import numpy as np
import jax
import jax.numpy as jnp
from jax import lax
from jax.experimental import pallas as pl
from jax.experimental.pallas import tpu as pltpu

F32 = jnp.float32
BF16 = jnp.bfloat16

D_MODEL = 2048
BATCH = 16
SEQ = 256
DEPTH = 4
DEC_BATCH = 2
DEC_SEQ = 1024
PAST_LEN = 256
GRID_W = 64
ROPE_BASE = 10000.0
LN_EPS = 1e-5
RMS_EPS = 1e-6
NEG_INF = -1e30
DEEPNORM_ALPHA = (2 * DEPTH) ** 0.25

MLA_HEADS = 16
MLA_Q_LORA = 512
MLA_KV_LORA = 512
MLA_NOPE = 128
MLA_ROPE = 64
MLA_V = 128
MLA_SCALE = (MLA_NOPE + MLA_ROPE) ** -0.5

GQA_HEADS = 32
GQA_KV_HEADS = 8
GQA_HEAD_DIM = 64
WINDOW = 128
GQA_SCALE = GQA_HEAD_DIM ** -0.5

FNET_GROUPS = 4
FNET_GROUP_DIM = D_MODEL // FNET_GROUPS

PEER_HEADS = 8
PEER_N_KEYS = 128
PEER_N_EXPERTS = PEER_N_KEYS * PEER_N_KEYS
PEER_HALF = 128
PEER_TOPK = 16

N_PROMPT = BATCH * SEQ
N_SAMPLE = DEC_BATCH * DEC_SEQ
N_TOK = N_PROMPT + N_SAMPLE
N_COND = 8

MIB = 1024 * 1024

NT_DIMS = (((1,), (1,)), ((), ()))
TN_DIMS = (((0,), (0,)), ((), ()))


def _params(sem, vmem_mib, flags=None):
    return pltpu.CompilerParams(dimension_semantics=sem, vmem_limit_bytes=vmem_mib * MIB, flags=flags)


def _group_of_tile(i, tm):
    return jnp.maximum((i * tm) // DEC_SEQ - (N_PROMPT // DEC_SEQ - 1), 0)


def _mod_spec(tm, which):
    return pl.BlockSpec((None, None, 1, D_MODEL), lambda i: (_group_of_tile(i, tm), which, 0, 0))


def _ada_kernel(c_ref, w_ref, b_ref, o_ref):
    c = c_ref[...]
    s = (c * jax.nn.sigmoid(c)).astype(BF16)
    o_ref[...] = jnp.dot(s, w_ref[...].astype(BF16), preferred_element_type=F32) + b_ref[...]


def ada_modulation_all(cond, ada_w, ada_b):
    tn = 1024
    n_out = 6 * D_MODEL
    return pl.pallas_call(
        _ada_kernel,
        out_shape=jax.ShapeDtypeStruct((DEPTH, N_COND, n_out), F32),
        grid=(DEPTH, n_out // tn),
        in_specs=[
            pl.BlockSpec((N_COND, D_MODEL), lambda l, j: (0, 0)),
            pl.BlockSpec((None, D_MODEL, tn), lambda l, j: (l, 0, j)),
            pl.BlockSpec((None, 1, tn), lambda l, j: (l, 0, j)),
        ],
        out_specs=pl.BlockSpec((None, N_COND, tn), lambda l, j: (l, 0, j)),
        compiler_params=_params(("parallel", "parallel"), 40),
        name="ada_modulation",
    )(cond, ada_w, ada_b.reshape(DEPTH, 1, n_out))


def _modulate_kernel(x_ref, sh_ref, sc_ref, u_ref):
    u_ref[...] = (x_ref[...] * (1.0 + sc_ref[...]) + sh_ref[...]).astype(u_ref.dtype)


def modulate_tokens(x, mods):
    tm = 256
    return pl.pallas_call(
        _modulate_kernel,
        out_shape=jax.ShapeDtypeStruct((N_TOK, D_MODEL), BF16),
        grid=(N_TOK // tm,),
        in_specs=[pl.BlockSpec((tm, D_MODEL), lambda i: (i, 0)), _mod_spec(tm, 0), _mod_spec(tm, 1)],
        out_specs=pl.BlockSpec((tm, D_MODEL), lambda i: (i, 0)),
        compiler_params=_params(("parallel",), 24),
        name="modulate",
    )(x, mods, mods)


def _layer_norm_rows(y, g, b):
    mu = jnp.mean(y, axis=-1, keepdims=True)
    yc = y - mu
    var = jnp.mean(yc * yc, axis=-1, keepdims=True)
    return yc * lax.rsqrt(var + LN_EPS) * g + b


def _post_norm_kernel(x_ref, d_ref, gate_ref, g_ref, b_ref, xo_ref):
    y = DEEPNORM_ALPHA * x_ref[...] + gate_ref[...] * d_ref[...].astype(F32)
    xo_ref[...] = _layer_norm_rows(y, g_ref[...], b_ref[...])


def _post_norm_mod_kernel(x_ref, d_ref, gate_ref, g_ref, b_ref, sh_ref, sc_ref, xo_ref, uo_ref):
    y = DEEPNORM_ALPHA * x_ref[...] + gate_ref[...] * d_ref[...].astype(F32)
    xn = _layer_norm_rows(y, g_ref[...], b_ref[...])
    xo_ref[...] = xn
    uo_ref[...] = (xn * (1.0 + sc_ref[...]) + sh_ref[...]).astype(uo_ref.dtype)


def post_norm(x, delta, mods, gate_idx, g, b, next_mods=None, next_idx=None, rows=(0, N_TOK)):
    tm = 256
    row = pl.BlockSpec((tm, D_MODEL), lambda i: (i, 0))
    vec = pl.BlockSpec((1, D_MODEL), lambda i: (0, 0))
    ins = [x, delta, mods, g.reshape(1, D_MODEL), b.reshape(1, D_MODEL)]
    specs = [row, row, _mod_spec(tm, gate_idx), vec, vec]
    if next_mods is None:
        i0 = rows[0] // tm
        row_in = pl.BlockSpec((tm, D_MODEL), lambda i: (i0 + i, 0))
        gate = pl.BlockSpec((None, None, 1, D_MODEL), lambda i: (_group_of_tile(i0 + i, tm), gate_idx, 0, 0))
        return pl.pallas_call(
            _post_norm_kernel,
            out_shape=jax.ShapeDtypeStruct((rows[1], D_MODEL), F32),
            grid=(rows[1] // tm,), in_specs=[row_in, row_in, gate, vec, vec], out_specs=row,
            compiler_params=_params(("parallel",), 32), name="post_norm",
        )(*ins)
    ins += [next_mods, next_mods]
    specs += [_mod_spec(tm, next_idx), _mod_spec(tm, next_idx + 1)]
    return pl.pallas_call(
        _post_norm_mod_kernel,
        out_shape=(jax.ShapeDtypeStruct((N_TOK, D_MODEL), F32), jax.ShapeDtypeStruct((N_TOK, D_MODEL), BF16)),
        grid=(N_TOK // tm,), in_specs=specs, out_specs=(row, row),
        compiler_params=_params(("parallel",), 32), name="post_norm_mod",
    )(*ins)


def _mm_kernel(x_ref, w_ref, o_ref):
    o_ref[...] = jnp.dot(x_ref[...], w_ref[...].astype(BF16), preferred_element_type=F32).astype(o_ref.dtype)


def matmul(x, w, out_dtype, tm=1024, tn=512, name="matmul", layer=None, col0=0, n=None):
    m, k = x.shape
    n = w.shape[-1] - col0 if n is None else n
    tm = min(tm, m)
    tn = min(tn, n)
    assert m % tm == 0 and n % tn == 0 and col0 % tn == 0, (m, n, tm, tn, col0)
    j0 = col0 // tn
    if layer is None:
        w_spec = pl.BlockSpec((k, tn), lambda i, j: (0, j0 + j))
    else:
        w_spec = pl.BlockSpec((None, k, tn), lambda i, j: (layer, 0, j0 + j))
    return pl.pallas_call(
        _mm_kernel,
        out_shape=jax.ShapeDtypeStruct((m, n), out_dtype),
        grid=(m // tm, n // tn),
        in_specs=[pl.BlockSpec((tm, k), lambda i, j: (i, 0)), w_spec],
        out_specs=pl.BlockSpec((tm, tn), lambda i, j: (i, j)),
        compiler_params=_params(("parallel", "parallel"), 40),
        name=name,
    )(x, w)


def _rms_rows(y, g):
    return y * lax.rsqrt(jnp.mean(y * y, axis=-1, keepdims=True) + RMS_EPS) * g


def _mla_down_kernel(x_ref, w_ref, qn_ref, kvn_ref, cq_ref, ckv_ref, kr_ref):
    y = jnp.dot(x_ref[...], w_ref[...].astype(BF16), preferred_element_type=F32)
    cq_ref[...] = _rms_rows(y[:, :MLA_Q_LORA], qn_ref[...]).astype(cq_ref.dtype)
    ckv_ref[...] = _rms_rows(y[:, MLA_Q_LORA:MLA_Q_LORA + MLA_KV_LORA], kvn_ref[...])
    kr_ref[...] = y[:, MLA_Q_LORA + MLA_KV_LORA:MLA_Q_LORA + MLA_KV_LORA + MLA_ROPE]


def mla_down(u, w_cat, q_norm, kv_norm):
    tm = 512
    n = w_cat.shape[1]
    return pl.pallas_call(
        _mla_down_kernel,
        out_shape=(jax.ShapeDtypeStruct((N_TOK, MLA_Q_LORA), BF16),
                   jax.ShapeDtypeStruct((N_TOK, MLA_KV_LORA), F32),
                   jax.ShapeDtypeStruct((N_TOK, MLA_ROPE), F32)),
        grid=(N_TOK // tm,),
        in_specs=[pl.BlockSpec((tm, D_MODEL), lambda i: (i, 0)),
                  pl.BlockSpec((D_MODEL, n), lambda i: (0, 0)),
                  pl.BlockSpec((1, MLA_Q_LORA), lambda i: (0, 0)),
                  pl.BlockSpec((1, MLA_KV_LORA), lambda i: (0, 0))],
        out_specs=(pl.BlockSpec((tm, MLA_Q_LORA), lambda i: (i, 0)),
                   pl.BlockSpec((tm, MLA_KV_LORA), lambda i: (i, 0)),
                   pl.BlockSpec((tm, MLA_ROPE), lambda i: (i, 0))),
        compiler_params=_params(("parallel",), 48),
        name="mla_down",
    )(u, w_cat, q_norm.reshape(1, -1), kv_norm.reshape(1, -1))


def _rope_tables():
    t = np.arange(DEC_SEQ)
    quarter = MLA_ROPE // 4
    inv_freq = ROPE_BASE ** (-np.arange(quarter, dtype=np.float64) / quarter)
    ang_row = (t // GRID_W)[:, None] * inv_freq[None, :]
    ang_col = (t % GRID_W)[:, None] * inv_freq[None, :]
    cos = np.concatenate([np.cos(ang_row)] * 2 + [np.cos(ang_col)] * 2, -1)
    sin = np.concatenate([-np.sin(ang_row), np.sin(ang_row), -np.sin(ang_col), np.sin(ang_col)], -1)
    return np.tile(cos, (1, 2)), np.tile(sin, (1, 2))


def _rope_kernel(x_ref, c_ref, s_ref, o_ref):
    x = x_ref[...].astype(F32)
    w = x.shape[-1]
    reps = w // c_ref.shape[-1]
    cos = jnp.tile(c_ref[...], (1, reps))
    sin = jnp.tile(s_ref[...], (1, reps))
    lane = lax.broadcasted_iota(jnp.int32, x.shape, 1)
    partner = jnp.where((lane % 32) < 16, pltpu.roll(x, w - 16, 1), pltpu.roll(x, 16, 1))
    o_ref[...] = (x * cos + partner * sin).astype(o_ref.dtype)


def rope_rows(x, row_block0, col_block, width, n_rows, cos, sin, out_dtype, tr=256):
    t_blocks = cos.shape[0] // tr
    return pl.pallas_call(
        _rope_kernel,
        out_shape=jax.ShapeDtypeStruct((n_rows, width), out_dtype),
        grid=(n_rows // tr,),
        in_specs=[pl.BlockSpec((tr, width), lambda i: (row_block0 + i, col_block)),
                  pl.BlockSpec((tr, 128), lambda i: (i % t_blocks, 0)),
                  pl.BlockSpec((tr, 128), lambda i: (i % t_blocks, 0))],
        out_specs=pl.BlockSpec((tr, width), lambda i: (i, 0)),
        compiler_params=_params(("parallel",), 32),
        name="axial_rope",
    )(x, cos, sin)


def _softmax_parts(parts, sink):
    m = parts[0].max(axis=-1, keepdims=True)
    for s in parts[1:]:
        m = jnp.maximum(m, s.max(axis=-1, keepdims=True))
    if sink is not None:
        m = jnp.maximum(m, sink)
    ps = [jnp.exp(s - m) for s in parts]
    l = ps[0].sum(axis=-1, keepdims=True)
    for p in ps[1:]:
        l = l + p.sum(axis=-1, keepdims=True)
    if sink is not None:
        l = l + jnp.exp(sink - m)
    return ps, 1.0 / l


def _mla_attn_kernel(*refs, with_ctx):
    if with_ctx:
        qn_ref, qr_ref, kn_ref, v_ref, kr_ref, kn2_ref, v2_ref, kr2_ref, _, o_ref = refs
    else:
        qn_ref, qr_ref, kn_ref, v_ref, kr_ref, o_ref = refs
    kr = kr_ref[...].astype(BF16)
    kr2 = kr2_ref[...].astype(BF16) if with_ctx else None
    for h in range(MLA_HEADS):
        n0, n1 = h * MLA_NOPE, (h + 1) * MLA_NOPE
        qn = qn_ref[:, n0:n1]
        qr = qr_ref[:, h * MLA_ROPE:(h + 1) * MLA_ROPE]
        s = (lax.dot_general(qn, kn_ref[:, n0:n1], NT_DIMS, preferred_element_type=F32)
             + lax.dot_general(qr, kr, NT_DIMS, preferred_element_type=F32)) * MLA_SCALE
        parts = [s]
        if with_ctx:
            s2 = (lax.dot_general(qn, kn2_ref[:, n0:n1], NT_DIMS, preferred_element_type=F32)
                  + lax.dot_general(qr, kr2, NT_DIMS, preferred_element_type=F32)) * MLA_SCALE
            parts.append(s2)
        ps, inv_l = _softmax_parts(parts, None)
        o = jnp.dot(ps[0].astype(BF16), v_ref[:, n0:n1], preferred_element_type=F32)
        if with_ctx:
            o = o + jnp.dot(ps[1].astype(BF16), v2_ref[:, n0:n1], preferred_element_type=F32)
        o_ref[:, n0:n1] = (o * inv_l).astype(o_ref.dtype)


def mla_attention_prompt(q, kv_up, kr):
    w = MLA_HEADS * MLA_NOPE
    return pl.pallas_call(
        functools.partial(_mla_attn_kernel, with_ctx=False),
        out_shape=jax.ShapeDtypeStruct((N_TOK, w), BF16),
        grid=(BATCH,),
        in_specs=[pl.BlockSpec((SEQ, w), lambda b: (b, 0)),
                  pl.BlockSpec((SEQ, MLA_HEADS * MLA_ROPE), lambda b: (b, w // (MLA_HEADS * MLA_ROPE))),
                  pl.BlockSpec((SEQ, w), lambda b: (b, 0)),
                  pl.BlockSpec((SEQ, w), lambda b: (b, 1)),
                  pl.BlockSpec((SEQ, MLA_ROPE), lambda b: (b, 0))],
        out_specs=pl.BlockSpec((SEQ, w), lambda b: (b, 0)),
        compiler_params=_params(("parallel",), 32),
        name="mla_attention_prompt",
    )(q, q, kv_up, kv_up, kr)


def mla_attention_sample(qn, qr_rot, kv_up, kr_rot, kr_cache, o_prompt):
    w = MLA_HEADS * MLA_NOPE
    tq = 256
    nq = DEC_SEQ // tq
    q0 = N_PROMPT // tq
    lat0 = N_PROMPT // DEC_SEQ
    ctx0 = N_TOK // PAST_LEN
    return pl.pallas_call(
        functools.partial(_mla_attn_kernel, with_ctx=True),
        out_shape=jax.ShapeDtypeStruct((N_TOK, w), BF16),
        input_output_aliases={8: 0},
        grid=(DEC_BATCH, nq),
        in_specs=[pl.BlockSpec((tq, w), lambda b, i: (q0 + b * nq + i, 0)),
                  pl.BlockSpec((tq, MLA_HEADS * MLA_ROPE), lambda b, i: (b * nq + i, 0)),
                  pl.BlockSpec((DEC_SEQ, w), lambda b, i: (lat0 + b, 0)),
                  pl.BlockSpec((DEC_SEQ, w), lambda b, i: (lat0 + b, 1)),
                  pl.BlockSpec((DEC_SEQ, MLA_ROPE), lambda b, i: (b, 0)),
                  pl.BlockSpec((PAST_LEN, w), lambda b, i: (ctx0 + b, 0)),
                  pl.BlockSpec((PAST_LEN, w), lambda b, i: (ctx0 + b, 1)),
                  pl.BlockSpec((PAST_LEN, MLA_ROPE), lambda b, i: (b, 0)),
                  pl.BlockSpec(memory_space=pl.ANY)],
        out_specs=pl.BlockSpec((tq, w), lambda b, i: (q0 + b * nq + i, 0)),
        compiler_params=_params(("parallel", "parallel"), 56),
        name="mla_attention_sample",
    )(qn, qr_rot, kv_up, kv_up, kr_rot, kv_up, kv_up, kr_cache, o_prompt)


def _gqa_attn_kernel(*refs, with_ctx, tq):
    if with_ctx:
        sink_ref, q_ref, k_ref, v_ref, k2_ref, v2_ref, _, o_ref = refs
    else:
        sink_ref, q_ref, k_ref, v_ref, o_ref = refs
    group = GQA_HEADS // GQA_KV_HEADS
    d = GQA_HEAD_DIM
    if with_ctx:
        t = pl.program_id(1) * tq + lax.broadcasted_iota(jnp.int32, (group * tq, DEC_SEQ), 0) % tq
        s_pos = lax.broadcasted_iota(jnp.int32, (group * tq, DEC_SEQ), 1)
        in_window = jnp.abs(t - s_pos) <= WINDOW
    for hk in range(GQA_KV_HEADS):
        k = k_ref[:, hk * d:(hk + 1) * d].astype(BF16)
        v = v_ref[:, hk * d:(hk + 1) * d].astype(BF16)
        heads = range(hk * group, (hk + 1) * group)
        q = jnp.concatenate([q_ref[:, h * d:(h + 1) * d] for h in heads], axis=0)
        sink = jnp.concatenate([jnp.full((tq, 1), sink_ref[h], F32) for h in heads], axis=0)
        s = lax.dot_general(q, k, NT_DIMS, preferred_element_type=F32) * GQA_SCALE
        parts = [s]
        if with_ctx:
            k2 = k2_ref[:, hk * d:(hk + 1) * d].astype(BF16)
            v2 = v2_ref[:, hk * d:(hk + 1) * d].astype(BF16)
            parts = [jnp.where(in_window, s, NEG_INF),
                     lax.dot_general(q, k2, NT_DIMS, preferred_element_type=F32) * GQA_SCALE]
        ps, inv_l = _softmax_parts(parts, sink)
        o = jnp.dot(ps[0].astype(BF16), v, preferred_element_type=F32)
        if with_ctx:
            o = o + jnp.dot(ps[1].astype(BF16), v2, preferred_element_type=F32)
        o = (o * inv_l).astype(o_ref.dtype)
        for g, h in enumerate(heads):
            o_ref[:, h * d:(h + 1) * d] = o[g * tq:(g + 1) * tq, :]


def gqa_attention_prompt(q, kv, sink):
    wq = GQA_HEADS * GQA_HEAD_DIM
    wk = GQA_KV_HEADS * GQA_HEAD_DIM
    return pl.pallas_call(
        functools.partial(_gqa_attn_kernel, with_ctx=False, tq=SEQ),
        out_shape=jax.ShapeDtypeStruct((N_TOK, wq), BF16),
        grid=(BATCH,),
        in_specs=[pl.BlockSpec(memory_space=pltpu.SMEM),
                  pl.BlockSpec((SEQ, wq), lambda b: (b, 0)),
                  pl.BlockSpec((SEQ, wk), lambda b: (b, 0)),
                  pl.BlockSpec((SEQ, wk), lambda b: (b, 1))],
        out_specs=pl.BlockSpec((SEQ, wq), lambda b: (b, 0)),
        compiler_params=_params(("parallel",), 32),
        name="gqa_attention_prompt",
    )(sink, q, kv, kv)


def gqa_attention_sample(q_rot, k_rot, kv, k_cache, v_cache, sink, o_prompt):
    wq = GQA_HEADS * GQA_HEAD_DIM
    wk = GQA_KV_HEADS * GQA_HEAD_DIM
    tq = 128
    nq = DEC_SEQ // tq
    q0 = N_PROMPT // tq
    lat0 = N_PROMPT // DEC_SEQ
    return pl.pallas_call(
        functools.partial(_gqa_attn_kernel, with_ctx=True, tq=tq),
        out_shape=jax.ShapeDtypeStruct((N_TOK, wq), BF16),
        input_output_aliases={6: 0},
        grid=(DEC_BATCH, nq),
        in_specs=[pl.BlockSpec(memory_space=pltpu.SMEM),
                  pl.BlockSpec((tq, wq), lambda b, i: (b * nq + i, 0)),
                  pl.BlockSpec((DEC_SEQ, wk), lambda b, i: (b, 0)),
                  pl.BlockSpec((DEC_SEQ, wk), lambda b, i: (lat0 + b, 1)),
                  pl.BlockSpec((PAST_LEN, wk), lambda b, i: (b, 0)),
                  pl.BlockSpec((PAST_LEN, wk), lambda b, i: (b, 0)),
                  pl.BlockSpec(memory_space=pl.ANY)],
        out_specs=pl.BlockSpec((tq, wq), lambda b, i: (q0 + b * nq + i, 0)),
        compiler_params=_params(("parallel", "parallel"), 56),
        name="gqa_attention_sample",
    )(sink, q_rot, k_rot, kv, k_cache, v_cache, o_prompt)


def _dft_tables(n):
    jk = (np.arange(n)[:, None] * np.arange(n)[None, :]) % n
    ang = 2.0 * np.pi * jk / n
    return np.cos(ang) / math.sqrt(n), np.sin(ang) / math.sqrt(n)


def _fnet_kernel(x_ref, ct_ref, st_ref, cc_ref, sc_ref, *rest):
    o_ref = rest[-1]
    x = x_ref[...]
    y1 = jnp.dot(x, cc_ref[...], preferred_element_type=F32).astype(BF16)
    y2 = jnp.dot(x, sc_ref[...], preferred_element_type=F32).astype(BF16)
    o = (jnp.dot(ct_ref[...], y1, preferred_element_type=F32)
         - jnp.dot(st_ref[...], y2, preferred_element_type=F32))
    o_ref[...] = o.astype(o_ref.dtype)


def _fill_in_place(prev):
    if prev is None:
        return (), []
    return (prev,), [pl.BlockSpec(memory_space=pl.ANY)]


def fourier_real_2d(u, seq_block0, n_seq, t, prev=None):
    ct, st = (jnp.asarray(a, BF16) for a in _dft_tables(t))
    cc, sc = (jnp.asarray(a, BF16) for a in _dft_tables(FNET_GROUP_DIM))
    gd = FNET_GROUP_DIM
    extra, extra_specs = _fill_in_place(prev)
    return pl.pallas_call(
        _fnet_kernel,
        out_shape=jax.ShapeDtypeStruct((N_TOK, D_MODEL), BF16),
        input_output_aliases={5: 0} if prev is not None else {},
        grid=(n_seq, FNET_GROUPS),
        in_specs=[pl.BlockSpec((t, gd), lambda b, g: (seq_block0 + b, g)),
                  pl.BlockSpec((t, t), lambda b, g: (0, 0)),
                  pl.BlockSpec((t, t), lambda b, g: (0, 0)),
                  pl.BlockSpec((gd, gd), lambda b, g: (0, 0)),
                  pl.BlockSpec((gd, gd), lambda b, g: (0, 0))] + extra_specs,
        out_specs=pl.BlockSpec((t, gd), lambda b, g: (seq_block0 + b, g)),
        compiler_params=_params(("parallel", "parallel"), 32),
        name="fourier_mix",
    )(u, ct, st, cc, sc, *extra)


def _conv_kernel(b_ref, c_ref, h_ref, w_ref, cb_ref, *rest):
    o_ref = rest[-1]
    z = c_ref[...].astype(F32) * h_ref[...].astype(F32)
    t = z.shape[0]
    row = lax.broadcasted_iota(jnp.int32, z.shape, 0)
    z_prev = jnp.where(row == 0, 0.0, pltpu.roll(z, 1, 0))
    z_next = jnp.where(row == t - 1, 0.0, pltpu.roll(z, t - 1, 0))
    conv = z_prev * w_ref[0:1, :] + z * w_ref[1:2, :] + z_next * w_ref[2:3, :] + cb_ref[...]
    o_ref[...] = (b_ref[...].astype(F32) * conv).astype(o_ref.dtype)


def gated_conv(h3, conv_w, conv_b, seq_block0, n_seq, t, prev=None):
    tn = 512
    nj = D_MODEL // tn
    extra, extra_specs = _fill_in_place(prev)
    return pl.pallas_call(
        _conv_kernel,
        out_shape=jax.ShapeDtypeStruct((N_TOK, D_MODEL), BF16),
        input_output_aliases={5: 0} if prev is not None else {},
        grid=(n_seq, nj),
        in_specs=[pl.BlockSpec((t, tn), lambda s, j: (seq_block0 + s, j)),
                  pl.BlockSpec((t, tn), lambda s, j: (seq_block0 + s, nj + j)),
                  pl.BlockSpec((t, tn), lambda s, j: (seq_block0 + s, 2 * nj + j)),
                  pl.BlockSpec((3, tn), lambda s, j: (0, j)),
                  pl.BlockSpec((1, tn), lambda s, j: (0, j))] + extra_specs,
        out_specs=pl.BlockSpec((t, tn), lambda s, j: (seq_block0 + s, j)),
        compiler_params=_params(("parallel", "parallel"), 32),
        name="gated_conv",
    )(h3, h3, h3, conv_w, conv_b.reshape(1, D_MODEL), *extra)


_N_RANKS = PEER_TOPK + 1
_CAND_PAIRS = [(a, b) for a in range(_N_RANKS) for b in range(_N_RANKS) if (a + 1) * (b + 1) <= _N_RANKS]


_NO_RANK = 127.0


def _top_values_exact(s, k, want_rank):
    n = s.shape[0]
    idx = lax.broadcasted_iota(jnp.int32, s.shape, 0)
    rank = jnp.full(s.shape, _NO_RANK, F32) if want_rank else None
    out = []
    for it in range(k):
        m = jnp.max(s, axis=0, keepdims=True)
        out.append(m)
        first = jnp.min(jnp.where(s == m, idx, n), axis=0, keepdims=True)
        hit = idx == first
        s = jnp.where(hit, -jnp.inf, s)
        if want_rank:
            rank = jnp.where(hit, float(it), rank)
    return out, rank, None


def _top_values_distinct(s, k, want_rank):
    rank = jnp.full(s.shape, _NO_RANK, F32) if want_rank else None
    out = []
    for it in range(k):
        m = jnp.max(s, axis=0, keepdims=True)
        out.append(m)
        hit = s == m
        s = jnp.where(hit, -jnp.inf, s)
        if want_rank:
            rank = jnp.where(hit, float(it), rank)
    removed = jnp.sum(jnp.where(s == -jnp.inf, 1.0, 0.0), axis=0, keepdims=True)
    return out, rank, jnp.min(jnp.where(removed == float(k), 1.0, 0.0)) > 0.5


def _dup_bf16_words(x):
    bits = pltpu.bitcast(x.astype(BF16).astype(F32), jnp.int32)
    return bits | lax.shift_right_logical(bits, jnp.int32(16))


def _route_head(h, s1, s2, top_values, bq_ref, r2_ref, aw_ref, cw_ref):
    tm = s1.shape[1]
    v1, _, ok1 = top_values(s1, _N_RANKS, False)
    v2, rank2, ok2 = top_values(s2, _N_RANKS, True)
    n_cand = len(_CAND_PAIRS) + (-len(_CAND_PAIRS)) % 8
    cand_row = lax.broadcasted_iota(jnp.int32, (n_cand, tm), 0)
    cand = jnp.full((n_cand, tm), -jnp.inf, F32)
    for r, (a, b) in enumerate(_CAND_PAIRS):
        cand = jnp.where(cand_row == r, v1[a] + v2[b], cand)
    top, _, _ = _top_values_exact(cand, _N_RANKS, False)
    tau = 0.5 * (top[PEER_TOPK - 1] + top[PEER_TOPK])
    z = jnp.exp(top[0] - top[0])
    for kk in range(1, PEER_TOPK):
        z = z + jnp.exp(top[kk] - top[0])
    inv_z = 1.0 / z
    count = jnp.zeros_like(s1)
    for b in range(PEER_TOPK):
        count = jnp.where(s1 >= tau - v2[b], float(b + 1), count)
    rows = slice(h * PEER_N_KEYS, (h + 1) * PEER_N_KEYS)
    bq_ref[rows, :] = (jnp.exp(s2 - v2[0]) * inv_z).astype(BF16)
    r2_ref[rows, :] = rank2.astype(BF16)
    a_half = 0.5 * jnp.exp(s1 - v1[0])
    aw_ref[:, h, :, :] = _dup_bf16_words(a_half).reshape(PEER_N_KEYS // 8, 8, tm)
    cw_ref[:, h, :, :] = _dup_bf16_words(count).reshape(PEER_N_KEYS // 8, 8, tm)
    return None if ok1 is None else jnp.logical_and(ok1, ok2)


def _route_kernel(q_ref, sk_ref, bq_ref, r2_ref, aw_ref, cw_ref):
    outs = (bq_ref, r2_ref, aw_ref, cw_ref)
    for h in range(PEER_HEADS):
        s = []
        for c in range(2):
            hc = 2 * h + c
            s.append(lax.dot_general(sk_ref[hc].astype(BF16), q_ref[:, hc * PEER_HALF:(hc + 1) * PEER_HALF],
                                     NT_DIMS, preferred_element_type=F32))
        distinct = _route_head(h, s[0], s[1], _top_values_distinct, *outs)

        @pl.when(jnp.logical_not(distinct))
        def _():
            _route_head(h, s[0], s[1], _top_values_exact, *outs)


def peer_route(q, sub_keys, layer):
    tm = 256
    n_tok = q.shape[0]
    sk = sub_keys.reshape(-1, PEER_HEADS * 2, PEER_N_KEYS, PEER_HALF)
    tiles = PEER_N_KEYS // 8
    col_out = jax.ShapeDtypeStruct((PEER_HEADS * PEER_N_KEYS, n_tok), BF16)
    col_spec = pl.BlockSpec((PEER_HEADS * PEER_N_KEYS, tm), lambda i: (0, i))
    row_out = jax.ShapeDtypeStruct((tiles, PEER_HEADS, 8, n_tok), jnp.int32)
    row_spec = pl.BlockSpec((tiles, PEER_HEADS, 8, tm), lambda i: (0, 0, 0, i))
    return pl.pallas_call(
        _route_kernel,
        out_shape=(col_out, col_out, row_out, row_out),
        grid=(n_tok // tm,),
        in_specs=[pl.BlockSpec((tm, PEER_HEADS * 2 * PEER_HALF), lambda i: (i, 0)),
                  pl.BlockSpec((None, PEER_HEADS * 2, PEER_N_KEYS, PEER_HALF), lambda i: (layer, 0, 0, 0))],
        out_specs=(col_spec, col_spec, row_spec, row_spec),
        compiler_params=_params(("parallel",), 32),
        name="peer_route",
    )(q, sk)


PEER_TE = 512
PEER_TM = 1024
PEER_CHUNK = 512
PEER_N_TILES = PEER_N_EXPERTS // PEER_TE


def _bf16_rows(word_row, n_rows):
    return pltpu.bitcast(jnp.broadcast_to(word_row, (n_rows // 2, word_row.shape[1])), BF16)


PEER_LANES = 256
PEER_ROWS = 128


def _peer_gated_slab(k, k0, bq_ref, r2_ref, aw_ref, cw_ref, ht_ref, at_ref):
    for j in range(ht_ref.shape[1] // PEER_LANES):
        lanes = slice(j * PEER_LANES, (j + 1) * PEER_LANES)
        for r in range(PEER_N_KEYS // PEER_ROWS):
            g = [None, None]
            for h in range(PEER_HEADS):
                count = _bf16_rows(cw_ref[0, h, k0 + k:k0 + k + 1, lanes], PEER_ROWS)
                a_half = _bf16_rows(aw_ref[0, h, k0 + k:k0 + k + 1, lanes], PEER_ROWS)
                rows = slice(h * PEER_N_KEYS + r * PEER_ROWS, h * PEER_N_KEYS + (r + 1) * PEER_ROWS)
                bq = bq_ref[rows, lanes]
                contrib = jnp.where(r2_ref[rows, lanes] < count, bq, jnp.zeros_like(bq)) * a_half
                g[h % 2] = contrib if g[h % 2] is None else g[h % 2] + contrib
            krows = slice(k * PEER_N_KEYS + r * PEER_ROWS, k * PEER_N_KEYS + (r + 1) * PEER_ROWS)
            hk = ht_ref[krows, lanes]
            gelu2 = hk * (1.0 + lax.erf(hk * math.sqrt(0.5)))
            at_ref[krows, lanes] = gelu2.astype(BF16) * (g[0] + g[1])


def _peer_step(x_ref, bq_ref, r2_ref, aw_ref, cw_ref, u_ref, v_ref, o_ref, at_ref, ht_write, ht_read, k0):
    keys_per_tile = PEER_TE // PEER_N_KEYS
    keys_per_chunk = PEER_CHUNK // PEER_N_KEYS
    tm = x_ref.shape[0]
    tok = tm // keys_per_tile
    u = u_ref[...].astype(BF16) if ht_write is not None else None
    for k in range(keys_per_tile):
        if ht_read is not None:
            _peer_gated_slab(k, k0, bq_ref, r2_ref, aw_ref, cw_ref, ht_read, at_ref)
        if ht_write is not None:
            ht_write[:, k * tok:(k + 1) * tok] = lax.dot_general(
                u, x_ref[k * tok:(k + 1) * tok, :], NT_DIMS, preferred_element_type=F32)
        if ht_read is not None and (k + 1) % keys_per_chunk == 0:
            chunk = slice((k + 1) * PEER_N_KEYS - PEER_CHUNK, (k + 1) * PEER_N_KEYS)
            o_ref[...] += lax.dot_general(at_ref[chunk, :], v_ref[chunk, :].astype(BF16), TN_DIMS,
                                          preferred_element_type=F32)


def _peer_kernel(x_ref, bq_ref, r2_ref, aw_ref, cw_ref, u_ref, v_ref, o_ref, ht0_ref, ht1_ref, at_ref):
    e = pl.program_id(1)
    last = PEER_N_TILES
    keys_per_tile = PEER_TE // PEER_N_KEYS
    assert 8 // keys_per_tile == 2
    step = functools.partial(_peer_step, x_ref, bq_ref, r2_ref, aw_ref, cw_ref, u_ref, v_ref, o_ref, at_ref)
    even = e % 2 == 0

    @pl.when(e == 0)
    def _():
        o_ref[...] = jnp.zeros_like(o_ref)
        step(ht0_ref, None, None)

    @pl.when(jnp.logical_not(even))
    def _():
        step(ht1_ref, ht0_ref, 0)

    @pl.when(even & (e > 0) & (e < last))
    def _():
        step(ht0_ref, ht1_ref, keys_per_tile)

    @pl.when(e == last)
    def _():
        step(None, ht1_ref, keys_per_tile)


def peer_experts(u, bq, r2, aw, cw, exp_u, exp_v, layer):
    tm, te = PEER_TM, PEER_TE
    n_tok = u.shape[0]
    assert PEER_N_TILES % 2 == 0 and n_tok % tm == 0
    tiles_per_row_block = 8 * PEER_N_KEYS // te
    once = pl.Buffered(1)
    col_spec = pl.BlockSpec((PEER_HEADS * PEER_N_KEYS, tm), lambda i, e: (0, i), pipeline_mode=once)
    row_spec = pl.BlockSpec((1, PEER_HEADS, 8, tm),
                            lambda i, e: (jnp.maximum(e - 1, 0) // tiles_per_row_block, 0, 0, i))
    return pl.pallas_call(
        _peer_kernel,
        out_shape=jax.ShapeDtypeStruct((n_tok, D_MODEL), F32),
        grid=(n_tok // tm, PEER_N_TILES + 1),
        in_specs=[pl.BlockSpec((tm, D_MODEL), lambda i, e: (i, 0), pipeline_mode=once),
                  col_spec, col_spec,
                  row_spec, row_spec,
                  pl.BlockSpec((None, te, D_MODEL), lambda i, e: (layer, jnp.minimum(e, PEER_N_TILES - 1), 0)),
                  pl.BlockSpec((None, te, D_MODEL), lambda i, e: (layer, jnp.maximum(e - 1, 0), 0))],
        out_specs=pl.BlockSpec((tm, D_MODEL), lambda i, e: (i, 0), pipeline_mode=once),
        scratch_shapes=[pltpu.VMEM((te, tm), F32), pltpu.VMEM((te, tm), F32), pltpu.VMEM((te, tm), BF16)],
        compiler_params=_params(("parallel", "arbitrary"), 56),
        name="peer_experts",
    )(u, bq, r2, aw, cw, exp_u, exp_v)


def peer_ffn(u, w_q, sub_keys, exp_u, exp_v, layer):
    q = matmul(u, w_q, BF16, name="peer_query", layer=layer)
    bq, r2, aw, cw = peer_route(q, sub_keys, layer)
    return peer_experts(u, bq, r2, aw, cw, exp_u, exp_v, layer)


def mla_mixer(u, cache_ckv, cache_krope, w_dq, q_norm, w_uq, w_dkv, kv_norm, w_uk, w_uv, w_o):
    pad = (-(MLA_Q_LORA + MLA_KV_LORA + MLA_ROPE)) % 128
    w_cat = jnp.concatenate([w_dq, w_dkv, jnp.zeros((D_MODEL, pad), F32)], axis=1)
    cq, ckv, kr = mla_down(u, w_cat, q_norm, kv_norm)
    w_uq3 = w_uq.reshape(MLA_Q_LORA, MLA_HEADS, MLA_NOPE + MLA_ROPE)
    w_q_cat = jnp.concatenate([w_uq3[:, :, :MLA_NOPE].reshape(MLA_Q_LORA, -1),
                               w_uq3[:, :, MLA_NOPE:].reshape(MLA_Q_LORA, -1)], axis=1)
    q = matmul(cq, w_q_cat, BF16, name="mla_q_up")
    w_nope = MLA_HEADS * MLA_NOPE
    ckv_src = jnp.concatenate([ckv, cache_ckv.reshape(-1, MLA_KV_LORA)], axis=0).astype(BF16)
    kv_up = matmul(ckv_src, jnp.concatenate([w_uk, w_uv], axis=1), BF16, tm=512, tn=2048, name="mla_kv_up")
    cos, sin = (jnp.asarray(t, F32) for t in _rope_tables())
    w_rope = MLA_HEADS * MLA_ROPE
    qr_rot = rope_rows(q, N_PROMPT // 256, w_nope // w_rope, w_rope, N_SAMPLE, cos, sin, BF16)
    cos2, sin2 = (t.reshape(DEC_SEQ // 2, 128) for t in (cos[:, :MLA_ROPE], sin[:, :MLA_ROPE]))
    kr_pairs = kr[N_PROMPT:].reshape(N_SAMPLE // 2, 2 * MLA_ROPE)
    kr_rot = rope_rows(kr_pairs, 0, 0, 2 * MLA_ROPE, N_SAMPLE // 2, cos2, sin2, BF16).reshape(N_SAMPLE, MLA_ROPE)
    o = mla_attention_prompt(q, kv_up, kr)
    o = mla_attention_sample(q, qr_rot, kv_up, kr_rot, cache_krope.reshape(-1, MLA_ROPE), o)
    op = matmul(o, w_o, F32, name="mla_out")
    return op, ckv, kr


def gqa_mixer(u, cache_k, cache_v, w_qkv, sink, w_o):
    nq = GQA_HEADS * GQA_HEAD_DIM
    nk = GQA_KV_HEADS * GQA_HEAD_DIM
    q = matmul(u, w_qkv, BF16, name="gqa_q", n=nq)
    kv = matmul(u, w_qkv, F32, name="gqa_kv", col0=nq)
    cos, sin = (jnp.asarray(t, F32) for t in _rope_tables())
    q_rot = rope_rows(q, N_PROMPT // 256, 0, nq, N_SAMPLE, cos, sin, BF16)
    k_rot = rope_rows(kv, N_PROMPT // 256, 0, nk, N_SAMPLE, cos, sin, BF16)
    o = gqa_attention_prompt(q, kv, sink)
    o = gqa_attention_sample(q_rot, k_rot, kv, cache_k.reshape(-1, nk), cache_v.reshape(-1, nk), sink, o)
    op = matmul(o, w_o, F32, name="gqa_out")
    return op, kv


def fnet_mixer(u, w_out):
    f = fourier_real_2d(u, 0, BATCH, SEQ)
    f = fourier_real_2d(u, N_PROMPT // DEC_SEQ, DEC_BATCH, DEC_SEQ, prev=f)
    return matmul(f, w_out, F32, name="fnet_out")


def conv_mixer(u, w_in, conv_w, conv_b, w_out):
    h3 = matmul(u, w_in, BF16, name="conv_in")
    y = gated_conv(h3, conv_w, conv_b, 0, BATCH, SEQ)
    y = gated_conv(h3, conv_w, conv_b, N_PROMPT // DEC_SEQ, DEC_BATCH, DEC_SEQ, prev=y)
    return matmul(y, w_out, F32, name="conv_out")


def kernel(x_prompt, x_sample, cache_l0_ckv, cache_l0_krope, cache_l1_k, cache_l1_v, c, c_ctx, ada_w, ada_b, ln1_g, ln1_b, ln2_g, ln2_b, mla_w_dq, mla_q_norm, mla_w_uq, mla_w_dkv, mla_kv_norm, mla_w_uk, mla_w_uv, mla_w_o, gqa_w_qkv, gqa_sink, gqa_w_o, fnet_w_out, conv_w_in, conv_w, conv_b, conv_w_out, peer_w_q, peer_sub_keys, peer_u, peer_v):
    x = jnp.concatenate([x_prompt.reshape(N_PROMPT, D_MODEL), x_sample.reshape(N_SAMPLE, D_MODEL)], axis=0)
    cond = jnp.concatenate([c_ctx[None, :], c, jnp.zeros((N_COND - 1 - DEC_BATCH, D_MODEL), F32)], axis=0)
    mods_all = ada_modulation_all(cond, ada_w, ada_b).reshape(DEPTH, N_COND, 6, 1, D_MODEL)

    u = modulate_tokens(x, mods_all[0])
    new_ckv = new_krope = new_kv = None
    for i in range(DEPTH):
        mods = mods_all[i]
        if i == 0:
            delta, new_ckv, new_krope = mla_mixer(u, cache_l0_ckv, cache_l0_krope, mla_w_dq, mla_q_norm, mla_w_uq,
                                                  mla_w_dkv, mla_kv_norm, mla_w_uk, mla_w_uv, mla_w_o)
        elif i == 1:
            delta, new_kv = gqa_mixer(u, cache_l1_k, cache_l1_v, gqa_w_qkv, gqa_sink, gqa_w_o)
        elif i == 2:
            delta = fnet_mixer(u, fnet_w_out)
        else:
            delta = conv_mixer(u, conv_w_in, conv_w, conv_b, conv_w_out)
        x, u = post_norm(x, delta, mods, 2, ln1_g[i], ln1_b[i], mods, 3)
        delta = peer_ffn(u, peer_w_q, peer_sub_keys, peer_u, peer_v, i)
        if i + 1 < DEPTH:
            x, u = post_norm(x, delta, mods, 5, ln2_g[i], ln2_b[i], mods_all[i + 1], 0)
        else:
            y_prompt = post_norm(x, delta, mods, 5, ln2_g[i], ln2_b[i], rows=(0, N_PROMPT))
            y_sample = post_norm(x, delta, mods, 5, ln2_g[i], ln2_b[i], rows=(N_PROMPT, N_SAMPLE))

    nk = GQA_KV_HEADS * GQA_HEAD_DIM
    return (y_prompt.reshape(BATCH, SEQ, D_MODEL),
            y_sample.reshape(DEC_BATCH, DEC_SEQ, D_MODEL),
            new_ckv[:N_PROMPT].reshape(BATCH, SEQ, MLA_KV_LORA),
            new_krope[:N_PROMPT].reshape(BATCH, SEQ, MLA_ROPE),
            new_kv[:N_PROMPT, :nk].reshape(BATCH, SEQ, GQA_KV_HEADS, GQA_HEAD_DIM),
            new_kv[:N_PROMPT, nk:].reshape(BATCH, SEQ, GQA_KV_HEADS, GQA_HEAD_DIM))
```

```python
import functools
import math

import numpy as np
import jax
import jax.numpy as jnp
from jax import lax
from jax.experimental import pallas as pl
from jax.experimental.pallas import tpu as pltpu

F32 = jnp.float32
BF16 = jnp.bfloat16

D_MODEL = 2048
BATCH = 16
SEQ = 256
DEPTH = 4
DEC_BATCH = 2
DEC_SEQ = 1024
PAST_LEN = 256
GRID_W = 64
ROPE_BASE = 10000.0
LN_EPS = 1e-5
RMS_EPS = 1e-6
NEG_INF = -1e30
DEEPNORM_ALPHA = (2 * DEPTH) ** 0.25

MLA_HEADS = 16
MLA_Q_LORA = 512
MLA_KV_LORA = 512
MLA_NOPE = 128
MLA_ROPE = 64
MLA_V = 128
MLA_SCALE = (MLA_NOPE + MLA_ROPE) ** -0.5

GQA_HEADS = 32
GQA_KV_HEADS = 8
GQA_HEAD_DIM = 64
WINDOW = 128
GQA_SCALE = GQA_HEAD_DIM ** -0.5

FNET_GROUPS = 4
FNET_GROUP_DIM = D_MODEL // FNET_GROUPS

PEER_HEADS = 8
PEER_N_KEYS = 128
PEER_N_EXPERTS = PEER_N_KEYS * PEER_N_KEYS
PEER_HALF = 128
PEER_TOPK = 16

N_PROMPT = BATCH * SEQ
N_SAMPLE = DEC_BATCH * DEC_SEQ
N_TOK = N_PROMPT + N_SAMPLE
N_COND = 8

MIB = 1024 * 1024

NT_DIMS = (((1,), (1,)), ((), ()))
TN_DIMS = (((0,), (0,)), ((), ()))


def _params(sem, vmem_mib, flags=None):
    return pltpu.CompilerParams(dimension_semantics=sem, vmem_limit_bytes=vmem_mib * MIB, flags=flags)


def _group_of_tile(i, tm):
    return jnp.maximum((i * tm) // DEC_SEQ - (N_PROMPT // DEC_SEQ - 1), 0)


def _mod_spec(tm, which):
    return pl.BlockSpec((None, None, 1, D_MODEL), lambda i: (_group_of_tile(i, tm), which, 0, 0))


def _ada_kernel(c_ref, w_ref, b_ref, o_ref):
    c = c_ref[...]
    s = (c * jax.nn.sigmoid(c)).astype(BF16)
    o_ref[...] = jnp.dot(s, w_ref[...].astype(BF16), preferred_element_type=F32) + b_ref[...]


def ada_modulation_all(cond, ada_w, ada_b):
    tn = 1024
    n_out = 6 * D_MODEL
    return pl.pallas_call(
        _ada_kernel,
        out_shape=jax.ShapeDtypeStruct((DEPTH, N_COND, n_out), F32),
        grid=(DEPTH, n_out // tn),
        in_specs=[
            pl.BlockSpec((N_COND, D_MODEL), lambda l, j: (0, 0)),
            pl.BlockSpec((None, D_MODEL, tn), lambda l, j: (l, 0, j)),
            pl.BlockSpec((None, 1, tn), lambda l, j: (l, 0, j)),
        ],
        out_specs=pl.BlockSpec((None, N_COND, tn), lambda l, j: (l, 0, j)),
        compiler_params=_params(("parallel", "parallel"), 40),
        name="ada_modulation",
    )(cond, ada_w, ada_b.reshape(DEPTH, 1, n_out))


def _modulate_kernel(x_ref, sh_ref, sc_ref, u_ref):
    u_ref[...] = (x_ref[...] * (1.0 + sc_ref[...]) + sh_ref[...]).astype(u_ref.dtype)


def modulate_tokens(x, mods):
    tm = 256
    return pl.pallas_call(
        _modulate_kernel,
        out_shape=jax.ShapeDtypeStruct((N_TOK, D_MODEL), BF16),
        grid=(N_TOK // tm,),
        in_specs=[pl.BlockSpec((tm, D_MODEL), lambda i: (i, 0)), _mod_spec(tm, 0), _mod_spec(tm, 1)],
        out_specs=pl.BlockSpec((tm, D_MODEL), lambda i: (i, 0)),
        compiler_params=_params(("parallel",), 24),
        name="modulate",
    )(x, mods, mods)


def _layer_norm_rows(y, g, b):
    mu = jnp.mean(y, axis=-1, keepdims=True)
    yc = y - mu
    var = jnp.mean(yc * yc, axis=-1, keepdims=True)
    return yc * lax.rsqrt(var + LN_EPS) * g + b


def _post_norm_kernel(x_ref, d_ref, gate_ref, g_ref, b_ref, xo_ref):
    y = DEEPNORM_ALPHA * x_ref[...] + gate_ref[...] * d_ref[...].astype(F32)
    xo_ref[...] = _layer_norm_rows(y, g_ref[...], b_ref[...])


def _post_norm_mod_kernel(x_ref, d_ref, gate_ref, g_ref, b_ref, sh_ref, sc_ref, xo_ref, uo_ref):
    y = DEEPNORM_ALPHA * x_ref[...] + gate_ref[...] * d_ref[...].astype(F32)
    xn = _layer_norm_rows(y, g_ref[...], b_ref[...])
    xo_ref[...] = xn
    uo_ref[...] = (xn * (1.0 + sc_ref[...]) + sh_ref[...]).astype(uo_ref.dtype)


def post_norm(x, delta, mods, gate_idx, g, b, next_mods=None, next_idx=None, rows=(0, N_TOK)):
    tm = 256
    row = pl.BlockSpec((tm, D_MODEL), lambda i: (i, 0))
    vec = pl.BlockSpec((1, D_MODEL), lambda i: (0, 0))
    ins = [x, delta, mods, g.reshape(1, D_MODEL), b.reshape(1, D_MODEL)]
    specs = [row, row, _mod_spec(tm, gate_idx), vec, vec]
    if next_mods is None:
        i0 = rows[0] // tm
        row_in = pl.BlockSpec((tm, D_MODEL), lambda i: (i0 + i, 0))
        gate = pl.BlockSpec((None, None, 1, D_MODEL), lambda i: (_group_of_tile(i0 + i, tm), gate_idx, 0, 0))
        return pl.pallas_call(
            _post_norm_kernel,
            out_shape=jax.ShapeDtypeStruct((rows[1], D_MODEL), F32),
            grid=(rows[1] // tm,), in_specs=[row_in, row_in, gate, vec, vec], out_specs=row,
            compiler_params=_params(("parallel",), 32), name="post_norm",
        )(*ins)
    ins += [next_mods, next_mods]
    specs += [_mod_spec(tm, next_idx), _mod_spec(tm, next_idx + 1)]
    return pl.pallas_call(
        _post_norm_mod_kernel,
        out_shape=(jax.ShapeDtypeStruct((N_TOK, D_MODEL), F32), jax.ShapeDtypeStruct((N_TOK, D_MODEL), BF16)),
        grid=(N_TOK // tm,), in_specs=specs, out_specs=(row, row),
        compiler_params=_params(("parallel",), 32), name="post_norm_mod",
    )(*ins)


def _mm_kernel(x_ref, w_ref, o_ref):
    o_ref[...] = jnp.dot(x_ref[...], w_ref[...].astype(BF16), preferred_element_type=F32).astype(o_ref.dtype)


def _mm_groups_kernel(xp_ref, xs_ref, w_ref, o_ref, *, prompt_tiles):
    w = w_ref[...].astype(BF16)
    i = pl.program_id(0)

    @pl.when(i < prompt_tiles)
    def _():
        o_ref[...] = jnp.dot(xp_ref[...], w, preferred_element_type=F32).astype(o_ref.dtype)

    @pl.when(i >= prompt_tiles)
    def _():
        o_ref[...] = jnp.dot(xs_ref[...], w, preferred_element_type=F32).astype(o_ref.dtype)


def matmul_groups(x_prompt, x_sample, w, out_dtype, name):
    tm, tn = 1024, 512
    k = x_prompt.shape[1]
    n = w.shape[1]
    prompt_tiles = N_PROMPT // tm
    return pl.pallas_call(
        functools.partial(_mm_groups_kernel, prompt_tiles=prompt_tiles),
        out_shape=jax.ShapeDtypeStruct((N_TOK, n), out_dtype),
        grid=(N_TOK // tm, n // tn),
        in_specs=[pl.BlockSpec((tm, k), lambda i, j: (jnp.minimum(i, prompt_tiles - 1), 0)),
                  pl.BlockSpec((tm, k), lambda i, j: (jnp.maximum(i - prompt_tiles, 0), 0)),
                  pl.BlockSpec((k, tn), lambda i, j: (0, j))],
        out_specs=pl.BlockSpec((tm, tn), lambda i, j: (i, j)),
        compiler_params=_params(("parallel", "parallel"), 48),
        name=name,
    )(x_prompt, x_sample, w)


def matmul(x, w, out_dtype, tm=1024, tn=512, name="matmul", layer=None, col0=0, n=None):
    m, k = x.shape
    n = w.shape[-1] - col0 if n is None else n
    tm = min(tm, m)
    tn = min(tn, n)
    assert m % tm == 0 and n % tn == 0 and col0 % tn == 0, (m, n, tm, tn, col0)
    j0 = col0 // tn
    if layer is None:
        w_spec = pl.BlockSpec((k, tn), lambda i, j: (0, j0 + j))
    else:
        w_spec = pl.BlockSpec((None, k, tn), lambda i, j: (layer, 0, j0 + j))
    return pl.pallas_call(
        _mm_kernel,
        out_shape=jax.ShapeDtypeStruct((m, n), out_dtype),
        grid=(m // tm, n // tn),
        in_specs=[pl.BlockSpec((tm, k), lambda i, j: (i, 0)), w_spec],
        out_specs=pl.BlockSpec((tm, tn), lambda i, j: (i, j)),
        compiler_params=_params(("parallel", "parallel"), 40),
        name=name,
    )(x, w)


def _rms_rows(y, g):
    return y * lax.rsqrt(jnp.mean(y * y, axis=-1, keepdims=True) + RMS_EPS) * g


def _mla_down_kernel(x_ref, w_ref, qn_ref, kvn_ref, cq_ref, ckv_ref, kr_ref):
    y = jnp.dot(x_ref[...], w_ref[...].astype(BF16), preferred_element_type=F32)
    cq_ref[...] = _rms_rows(y[:, :MLA_Q_LORA], qn_ref[...]).astype(cq_ref.dtype)
    ckv_ref[...] = _rms_rows(y[:, MLA_Q_LORA:MLA_Q_LORA + MLA_KV_LORA], kvn_ref[...])
    kr_ref[...] = y[:, MLA_Q_LORA + MLA_KV_LORA:MLA_Q_LORA + MLA_KV_LORA + MLA_ROPE]


def mla_down(u, w_cat, q_norm, kv_norm):
    tm = 512
    n = w_cat.shape[1]
    return pl.pallas_call(
        _mla_down_kernel,
        out_shape=(jax.ShapeDtypeStruct((N_TOK, MLA_Q_LORA), BF16),
                   jax.ShapeDtypeStruct((N_TOK, MLA_KV_LORA), F32),
                   jax.ShapeDtypeStruct((N_TOK, MLA_ROPE), F32)),
        grid=(N_TOK // tm,),
        in_specs=[pl.BlockSpec((tm, D_MODEL), lambda i: (i, 0)),
                  pl.BlockSpec((D_MODEL, n), lambda i: (0, 0)),
                  pl.BlockSpec((1, MLA_Q_LORA), lambda i: (0, 0)),
                  pl.BlockSpec((1, MLA_KV_LORA), lambda i: (0, 0))],
        out_specs=(pl.BlockSpec((tm, MLA_Q_LORA), lambda i: (i, 0)),
                   pl.BlockSpec((tm, MLA_KV_LORA), lambda i: (i, 0)),
                   pl.BlockSpec((tm, MLA_ROPE), lambda i: (i, 0))),
        compiler_params=_params(("parallel",), 48),
        name="mla_down",
    )(u, w_cat, q_norm.reshape(1, -1), kv_norm.reshape(1, -1))


def _rope_tables():
    t = np.arange(DEC_SEQ)
    quarter = MLA_ROPE // 4
    inv_freq = ROPE_BASE ** (-np.arange(quarter, dtype=np.float64) / quarter)
    ang_row = (t // GRID_W)[:, None] * inv_freq[None, :]
    ang_col = (t % GRID_W)[:, None] * inv_freq[None, :]
    cos = np.concatenate([np.cos(ang_row)] * 2 + [np.cos(ang_col)] * 2, -1)
    sin = np.concatenate([-np.sin(ang_row), np.sin(ang_row), -np.sin(ang_col), np.sin(ang_col)], -1)
    return np.tile(cos, (1, 2)), np.tile(sin, (1, 2))


def _rope_kernel(x_ref, c_ref, s_ref, o_ref):
    x = x_ref[...].astype(F32)
    w = x.shape[-1]
    reps = w // c_ref.shape[-1]
    cos = jnp.tile(c_ref[...], (1, reps))
    sin = jnp.tile(s_ref[...], (1, reps))
    lane = lax.broadcasted_iota(jnp.int32, x.shape, 1)
    partner = jnp.where((lane % 32) < 16, pltpu.roll(x, w - 16, 1), pltpu.roll(x, 16, 1))
    o_ref[...] = (x * cos + partner * sin).astype(o_ref.dtype)


def rope_rows(x, row_block0, col_block, width, n_rows, cos, sin, out_dtype, tr=256):
    t_blocks = cos.shape[0] // tr
    return pl.pallas_call(
        _rope_kernel,
        out_shape=jax.ShapeDtypeStruct((n_rows, width), out_dtype),
        grid=(n_rows // tr,),
        in_specs=[pl.BlockSpec((tr, width), lambda i: (row_block0 + i, col_block)),
                  pl.BlockSpec((tr, 128), lambda i: (i % t_blocks, 0)),
                  pl.BlockSpec((tr, 128), lambda i: (i % t_blocks, 0))],
        out_specs=pl.BlockSpec((tr, width), lambda i: (i, 0)),
        compiler_params=_params(("parallel",), 32),
        name="axial_rope",
    )(x, cos, sin)


def _softmax_parts(parts, sink):
    m = parts[0].max(axis=-1, keepdims=True)
    for s in parts[1:]:
        m = jnp.maximum(m, s.max(axis=-1, keepdims=True))
    if sink is not None:
        m = jnp.maximum(m, sink)
    ps = [jnp.exp(s - m) for s in parts]
    l = ps[0].sum(axis=-1, keepdims=True)
    for p in ps[1:]:
        l = l + p.sum(axis=-1, keepdims=True)
    if sink is not None:
        l = l + jnp.exp(sink - m)
    return ps, 1.0 / l


def _mla_attn_kernel(*refs, with_ctx):
    if with_ctx:
        qn_ref, qr_ref, kn_ref, v_ref, kr_ref, kn2_ref, v2_ref, kr2_ref, o_ref = refs
    else:
        qn_ref, qr_ref, kn_ref, v_ref, kr_ref, o_ref = refs
    kr = kr_ref[...].astype(BF16)
    kr2 = kr2_ref[...].astype(BF16) if with_ctx else None
    for h in range(MLA_HEADS):
        n0, n1 = h * MLA_NOPE, (h + 1) * MLA_NOPE
        qn = qn_ref[:, n0:n1]
        qr = qr_ref[:, h * MLA_ROPE:(h + 1) * MLA_ROPE]
        s = (lax.dot_general(qn, kn_ref[:, n0:n1], NT_DIMS, preferred_element_type=F32)
             + lax.dot_general(qr, kr, NT_DIMS, preferred_element_type=F32)) * MLA_SCALE
        parts = [s]
        if with_ctx:
            s2 = (lax.dot_general(qn, kn2_ref[:, n0:n1], NT_DIMS, preferred_element_type=F32)
                  + lax.dot_general(qr, kr2, NT_DIMS, preferred_element_type=F32)) * MLA_SCALE
            parts.append(s2)
        ps, inv_l = _softmax_parts(parts, None)
        o = jnp.dot(ps[0].astype(BF16), v_ref[:, n0:n1], preferred_element_type=F32)
        if with_ctx:
            o = o + jnp.dot(ps[1].astype(BF16), v2_ref[:, n0:n1], preferred_element_type=F32)
        o_ref[:, n0:n1] = (o * inv_l).astype(o_ref.dtype)


def mla_attention_prompt(q, kv_up, kr):
    w = MLA_HEADS * MLA_NOPE
    return pl.pallas_call(
        functools.partial(_mla_attn_kernel, with_ctx=False),
        out_shape=jax.ShapeDtypeStruct((N_PROMPT, w), BF16),
        grid=(BATCH,),
        in_specs=[pl.BlockSpec((SEQ, w), lambda b: (b, 0)),
                  pl.BlockSpec((SEQ, MLA_HEADS * MLA_ROPE), lambda b: (b, w // (MLA_HEADS * MLA_ROPE))),
                  pl.BlockSpec((SEQ, w), lambda b: (b, 0)),
                  pl.BlockSpec((SEQ, w), lambda b: (b, 1)),
                  pl.BlockSpec((SEQ, MLA_ROPE), lambda b: (b, 0))],
        out_specs=pl.BlockSpec((SEQ, w), lambda b: (b, 0)),
        compiler_params=_params(("parallel",), 32),
        name="mla_attention_prompt",
    )(q, q, kv_up, kv_up, kr)


def mla_attention_sample(qn, qr_rot, kv_up, kr_rot, kr_cache):
    w = MLA_HEADS * MLA_NOPE
    tq = 256
    nq = DEC_SEQ // tq
    q0 = N_PROMPT // tq
    lat0 = N_PROMPT // DEC_SEQ
    ctx0 = N_TOK // PAST_LEN
    return pl.pallas_call(
        functools.partial(_mla_attn_kernel, with_ctx=True),
        out_shape=jax.ShapeDtypeStruct((N_SAMPLE, w), BF16),
        grid=(DEC_BATCH, nq),
        in_specs=[pl.BlockSpec((tq, w), lambda b, i: (q0 + b * nq + i, 0)),
                  pl.BlockSpec((tq, MLA_HEADS * MLA_ROPE), lambda b, i: (b * nq + i, 0)),
                  pl.BlockSpec((DEC_SEQ, w), lambda b, i: (lat0 + b, 0)),
                  pl.BlockSpec((DEC_SEQ, w), lambda b, i: (lat0 + b, 1)),
                  pl.BlockSpec((DEC_SEQ, MLA_ROPE), lambda b, i: (b, 0)),
                  pl.BlockSpec((PAST_LEN, w), lambda b, i: (ctx0 + b, 0)),
                  pl.BlockSpec((PAST_LEN, w), lambda b, i: (ctx0 + b, 1)),
                  pl.BlockSpec((PAST_LEN, MLA_ROPE), lambda b, i: (b, 0))],
        out_specs=pl.BlockSpec((tq, w), lambda b, i: (b * nq + i, 0)),
        compiler_params=_params(("parallel", "parallel"), 56),
        name="mla_attention_sample",
    )(qn, qr_rot, kv_up, kv_up, kr_rot, kv_up, kv_up, kr_cache)


def _gqa_attn_kernel(*refs, with_ctx, tq):
    if with_ctx:
        sink_ref, q_ref, k_ref, v_ref, k2_ref, v2_ref, o_ref = refs
    else:
        sink_ref, q_ref, k_ref, v_ref, o_ref = refs
    group = GQA_HEADS // GQA_KV_HEADS
    d = GQA_HEAD_DIM
    if with_ctx:
        t = pl.program_id(1) * tq + lax.broadcasted_iota(jnp.int32, (tq, DEC_SEQ), 0)
        s_pos = lax.broadcasted_iota(jnp.int32, (tq, DEC_SEQ), 1)
        in_window = jnp.abs(t - s_pos) <= WINDOW
    for hk in range(GQA_KV_HEADS):
        k = k_ref[:, hk * d:(hk + 1) * d].astype(BF16)
        v = v_ref[:, hk * d:(hk + 1) * d].astype(BF16)
        if with_ctx:
            k2 = k2_ref[:, hk * d:(hk + 1) * d].astype(BF16)
            v2 = v2_ref[:, hk * d:(hk + 1) * d].astype(BF16)
        for g in range(group):
            h = hk * group + g
            q = q_ref[:, h * d:(h + 1) * d]
            s = lax.dot_general(q, k, NT_DIMS, preferred_element_type=F32) * GQA_SCALE
            parts = [s]
            if with_ctx:
                parts = [jnp.where(in_window, s, NEG_INF),
                         lax.dot_general(q, k2, NT_DIMS, preferred_element_type=F32) * GQA_SCALE]
            ps, inv_l = _softmax_parts(parts, sink_ref[h])
            o = jnp.dot(ps[0].astype(BF16), v, preferred_element_type=F32)
            if with_ctx:
                o = o + jnp.dot(ps[1].astype(BF16), v2, preferred_element_type=F32)
            o_ref[:, h * d:(h + 1) * d] = (o * inv_l).astype(o_ref.dtype)


def gqa_attention_prompt(q, kv, sink):
    wq = GQA_HEADS * GQA_HEAD_DIM
    wk = GQA_KV_HEADS * GQA_HEAD_DIM
    return pl.pallas_call(
        functools.partial(_gqa_attn_kernel, with_ctx=False, tq=SEQ),
        out_shape=jax.ShapeDtypeStruct((N_PROMPT, wq), BF16),
        grid=(BATCH,),
        in_specs=[pl.BlockSpec(memory_space=pltpu.SMEM),
                  pl.BlockSpec((SEQ, wq), lambda b: (b, 0)),
                  pl.BlockSpec((SEQ, wk), lambda b: (b, 0)),
                  pl.BlockSpec((SEQ, wk), lambda b: (b, 1))],
        out_specs=pl.BlockSpec((SEQ, wq), lambda b: (b, 0)),
        compiler_params=_params(("parallel",), 32),
        name="gqa_attention_prompt",
    )(sink, q, kv, kv)


def gqa_attention_sample(q_rot, k_rot, kv, k_cache, v_cache, sink):
    wq = GQA_HEADS * GQA_HEAD_DIM
    wk = GQA_KV_HEADS * GQA_HEAD_DIM
    tq = 256
    nq = DEC_SEQ // tq
    lat0 = N_PROMPT // DEC_SEQ
    return pl.pallas_call(
        functools.partial(_gqa_attn_kernel, with_ctx=True, tq=tq),
        out_shape=jax.ShapeDtypeStruct((N_SAMPLE, wq), BF16),
        grid=(DEC_BATCH, nq),
        in_specs=[pl.BlockSpec(memory_space=pltpu.SMEM),
                  pl.BlockSpec((tq, wq), lambda b, i: (b * nq + i, 0)),
                  pl.BlockSpec((DEC_SEQ, wk), lambda b, i: (b, 0)),
                  pl.BlockSpec((DEC_SEQ, wk), lambda b, i: (lat0 + b, 1)),
                  pl.BlockSpec((PAST_LEN, wk), lambda b, i: (b, 0)),
                  pl.BlockSpec((PAST_LEN, wk), lambda b, i: (b, 0))],
        out_specs=pl.BlockSpec((tq, wq), lambda b, i: (b * nq + i, 0)),
        compiler_params=_params(("parallel", "parallel"), 56),
        name="gqa_attention_sample",
    )(sink, q_rot, k_rot, kv, k_cache, v_cache)


def _dft_tables(n):
    jk = (np.arange(n)[:, None] * np.arange(n)[None, :]) % n
    ang = 2.0 * np.pi * jk / n
    return np.cos(ang) / math.sqrt(n), np.sin(ang) / math.sqrt(n)


def _fnet_kernel(x_ref, ct_ref, st_ref, cc_ref, sc_ref, o_ref):
    x = x_ref[...]
    y1 = jnp.dot(x, cc_ref[...], preferred_element_type=F32).astype(BF16)
    y2 = jnp.dot(x, sc_ref[...], preferred_element_type=F32).astype(BF16)
    o = (jnp.dot(ct_ref[...], y1, preferred_element_type=F32)
         - jnp.dot(st_ref[...], y2, preferred_element_type=F32))
    o_ref[...] = o.astype(o_ref.dtype)


def fourier_real_2d(u, seq_block0, n_seq, t):
    ct, st = (jnp.asarray(a, BF16) for a in _dft_tables(t))
    cc, sc = (jnp.asarray(a, BF16) for a in _dft_tables(FNET_GROUP_DIM))
    gd = FNET_GROUP_DIM
    return pl.pallas_call(
        _fnet_kernel,
        out_shape=jax.ShapeDtypeStruct((n_seq * t, D_MODEL), BF16),
        grid=(n_seq, FNET_GROUPS),
        in_specs=[pl.BlockSpec((t, gd), lambda b, g: (seq_block0 + b, g)),
                  pl.BlockSpec((t, t), lambda b, g: (0, 0)),
                  pl.BlockSpec((t, t), lambda b, g: (0, 0)),
                  pl.BlockSpec((gd, gd), lambda b, g: (0, 0)),
                  pl.BlockSpec((gd, gd), lambda b, g: (0, 0))],
        out_specs=pl.BlockSpec((t, gd), lambda b, g: (b, g)),
        compiler_params=_params(("parallel", "parallel"), 32),
        name="fourier_mix",
    )(u, ct, st, cc, sc)


def _conv_kernel(b_ref, c_ref, h_ref, w_ref, cb_ref, o_ref):
    z = c_ref[...].astype(F32) * h_ref[...].astype(F32)
    t = z.shape[0]
    row = lax.broadcasted_iota(jnp.int32, z.shape, 0)
    z_prev = jnp.where(row == 0, 0.0, pltpu.roll(z, 1, 0))
    z_next = jnp.where(row == t - 1, 0.0, pltpu.roll(z, t - 1, 0))
    conv = z_prev * w_ref[0:1, :] + z * w_ref[1:2, :] + z_next * w_ref[2:3, :] + cb_ref[...]
    o_ref[...] = (b_ref[...].astype(F32) * conv).astype(o_ref.dtype)


def gated_conv(h3, conv_w, conv_b, seq_block0, n_seq, t):
    tn = 512
    nj = D_MODEL // tn
    return pl.pallas_call(
        _conv_kernel,
        out_shape=jax.ShapeDtypeStruct((n_seq * t, D_MODEL), BF16),
        grid=(n_seq, nj),
        in_specs=[pl.BlockSpec((t, tn), lambda s, j: (seq_block0 + s, j)),
                  pl.BlockSpec((t, tn), lambda s, j: (seq_block0 + s, nj + j)),
                  pl.BlockSpec((t, tn), lambda s, j: (seq_block0 + s, 2 * nj + j)),
                  pl.BlockSpec((3, tn), lambda s, j: (0, j)),
                  pl.BlockSpec((1, tn), lambda s, j: (0, j))],
        out_specs=pl.BlockSpec((t, tn), lambda s, j: (s, j)),
        compiler_params=_params(("parallel", "parallel"), 32),
        name="gated_conv",
    )(h3, h3, h3, conv_w, conv_b.reshape(1, D_MODEL))


_N_RANKS = PEER_TOPK + 1
_CAND_PAIRS = [(a, b) for a in range(_N_RANKS) for b in range(_N_RANKS) if (a + 1) * (b + 1) <= _N_RANKS]


_NO_RANK = 127.0


def _top_values_exact(s, k, want_rank):
    n = s.shape[0]
    idx = lax.broadcasted_iota(jnp.int32, s.shape, 0)
    rank = jnp.full(s.shape, _NO_RANK, F32) if want_rank else None
    out = []
    for it in range(k):
        m = jnp.max(s, axis=0, keepdims=True)
        out.append(m)
        first = jnp.min(jnp.where(s == m, idx, n), axis=0, keepdims=True)
        hit = idx == first
        s = jnp.where(hit, -jnp.inf, s)
        if want_rank:
            rank = jnp.where(hit, float(it), rank)
    return out, rank, None


def _top_values_distinct(s, k, want_rank):
    rank = jnp.full(s.shape, _NO_RANK, F32) if want_rank else None
    out = []
    for it in range(k):
        m = jnp.max(s, axis=0, keepdims=True)
        out.append(m)
        hit = s == m
        s = jnp.where(hit, -jnp.inf, s)
        if want_rank:
            rank = jnp.where(hit, float(it), rank)
    removed = jnp.sum(jnp.where(s == -jnp.inf, 1.0, 0.0), axis=0, keepdims=True)
    return out, rank, jnp.min(jnp.where(removed == float(k), 1.0, 0.0)) > 0.5


def _dup_bf16_words(x):
    bits = pltpu.bitcast(x.astype(BF16).astype(F32), jnp.int32)
    return bits | lax.shift_right_logical(bits, jnp.int32(16))


def _route_head(h, s1, s2, top_values, bq_ref, r2_ref, aw_ref, cw_ref):
    tm = s1.shape[1]
    v1, _, ok1 = top_values(s1, _N_RANKS, False)
    v2, rank2, ok2 = top_values(s2, _N_RANKS, True)
    n_cand = len(_CAND_PAIRS) + (-len(_CAND_PAIRS)) % 8
    cand_row = lax.broadcasted_iota(jnp.int32, (n_cand, tm), 0)
    cand = jnp.full((n_cand, tm), -jnp.inf, F32)
    for r, (a, b) in enumerate(_CAND_PAIRS):
        cand = jnp.where(cand_row == r, v1[a] + v2[b], cand)
    top, _, _ = _top_values_exact(cand, _N_RANKS, False)
    tau = 0.5 * (top[PEER_TOPK - 1] + top[PEER_TOPK])
    z = jnp.exp(top[0] - top[0])
    for kk in range(1, PEER_TOPK):
        z = z + jnp.exp(top[kk] - top[0])
    inv_z = 1.0 / z
    count = jnp.zeros_like(s1)
    for b in range(PEER_TOPK):
        count = jnp.where(s1 >= tau - v2[b], float(b + 1), count)
    rows = slice(h * PEER_N_KEYS, (h + 1) * PEER_N_KEYS)
    bq_ref[rows, :] = (jnp.exp(s2 - v2[0]) * inv_z).astype(BF16)
    r2_ref[rows, :] = rank2.astype(BF16)
    a_half = 0.5 * jnp.exp(s1 - v1[0])
    aw_ref[:, h, :, :] = _dup_bf16_words(a_half).reshape(PEER_N_KEYS // 8, 8, tm)
    cw_ref[:, h, :, :] = _dup_bf16_words(count).reshape(PEER_N_KEYS // 8, 8, tm)
    return None if ok1 is None else jnp.logical_and(ok1, ok2)


def _route_kernel(q_ref, sk_ref, bq_ref, r2_ref, aw_ref, cw_ref):
    outs = (bq_ref, r2_ref, aw_ref, cw_ref)
    for h in range(PEER_HEADS):
        s = []
        for c in range(2):
            hc = 2 * h + c
            s.append(lax.dot_general(sk_ref[hc].astype(BF16), q_ref[:, hc * PEER_HALF:(hc + 1) * PEER_HALF],
                                     NT_DIMS, preferred_element_type=F32))
        distinct = _route_head(h, s[0], s[1], _top_values_distinct, *outs)

        @pl.when(jnp.logical_not(distinct))
        def _():
            _route_head(h, s[0], s[1], _top_values_exact, *outs)


def peer_route(q, sub_keys, layer):
    tm = 256
    n_tok = q.shape[0]
    sk = sub_keys.reshape(-1, PEER_HEADS * 2, PEER_N_KEYS, PEER_HALF)
    tiles = PEER_N_KEYS // 8
    col_out = jax.ShapeDtypeStruct((PEER_HEADS * PEER_N_KEYS, n_tok), BF16)
    col_spec = pl.BlockSpec((PEER_HEADS * PEER_N_KEYS, tm), lambda i: (0, i))
    row_out = jax.ShapeDtypeStruct((tiles, PEER_HEADS, 8, n_tok), jnp.int32)
    row_spec = pl.BlockSpec((tiles, PEER_HEADS, 8, tm), lambda i: (0, 0, 0, i))
    return pl.pallas_call(
        _route_kernel,
        out_shape=(col_out, col_out, row_out, row_out),
        grid=(n_tok // tm,),
        in_specs=[pl.BlockSpec((tm, PEER_HEADS * 2 * PEER_HALF), lambda i: (i, 0)),
                  pl.BlockSpec((None, PEER_HEADS * 2, PEER_N_KEYS, PEER_HALF), lambda i: (layer, 0, 0, 0))],
        out_specs=(col_spec, col_spec, row_spec, row_spec),
        compiler_params=_params(("parallel",), 32),
        name="peer_route",
    )(q, sk)


PEER_TE = 512
PEER_TM = 1024
PEER_CHUNK = 512
PEER_N_TILES = PEER_N_EXPERTS // PEER_TE


def _bf16_rows(word_row, n_rows):
    return pltpu.bitcast(jnp.broadcast_to(word_row, (n_rows // 2, word_row.shape[1])), BF16)


PEER_LANES = 256
PEER_ROWS = 128


def _peer_gated_slab(k, k0, bq_ref, r2_ref, aw_ref, cw_ref, ht_ref, at_ref):
    for j in range(ht_ref.shape[1] // PEER_LANES):
        lanes = slice(j * PEER_LANES, (j + 1) * PEER_LANES)
        for r in range(PEER_N_KEYS // PEER_ROWS):
            g = [None, None]
            for h in range(PEER_HEADS):
                count = _bf16_rows(cw_ref[0, h, k0 + k:k0 + k + 1, lanes], PEER_ROWS)
                a_half = _bf16_rows(aw_ref[0, h, k0 + k:k0 + k + 1, lanes], PEER_ROWS)
                rows = slice(h * PEER_N_KEYS + r * PEER_ROWS, h * PEER_N_KEYS + (r + 1) * PEER_ROWS)
                bq = bq_ref[rows, lanes]
                contrib = jnp.where(r2_ref[rows, lanes] < count, bq, jnp.zeros_like(bq)) * a_half
                g[h % 2] = contrib if g[h % 2] is None else g[h % 2] + contrib
            krows = slice(k * PEER_N_KEYS + r * PEER_ROWS, k * PEER_N_KEYS + (r + 1) * PEER_ROWS)
            hk = ht_ref[krows, lanes]
            gelu2 = hk * (1.0 + lax.erf(hk * math.sqrt(0.5)))
            at_ref[krows, lanes] = gelu2.astype(BF16) * (g[0] + g[1])


def _peer_step(x_ref, bq_ref, r2_ref, aw_ref, cw_ref, u_ref, v_ref, o_ref, at_ref, ht_write, ht_read, k0):
    keys_per_tile = PEER_TE // PEER_N_KEYS
    keys_per_chunk = PEER_CHUNK // PEER_N_KEYS
    tm = x_ref.shape[0]
    tok = tm // keys_per_tile
    u = u_ref[...].astype(BF16) if ht_write is not None else None
    for k in range(keys_per_tile):
        if ht_read is not None:
            _peer_gated_slab(k, k0, bq_ref, r2_ref, aw_ref, cw_ref, ht_read, at_ref)
        if ht_write is not None:
            ht_write[:, k * tok:(k + 1) * tok] = lax.dot_general(
                u, x_ref[k * tok:(k + 1) * tok, :], NT_DIMS, preferred_element_type=F32)
        if ht_read is not None and (k + 1) % keys_per_chunk == 0:
            chunk = slice((k + 1) * PEER_N_KEYS - PEER_CHUNK, (k + 1) * PEER_N_KEYS)
            o_ref[...] += lax.dot_general(at_ref[chunk, :], v_ref[chunk, :].astype(BF16), TN_DIMS,
                                          preferred_element_type=F32)


def _peer_kernel(x_ref, bq_ref, r2_ref, aw_ref, cw_ref, u_ref, v_ref, o_ref, ht0_ref, ht1_ref, at_ref):
    e = pl.program_id(1)
    last = PEER_N_TILES
    keys_per_tile = PEER_TE // PEER_N_KEYS
    assert 8 // keys_per_tile == 2
    step = functools.partial(_peer_step, x_ref, bq_ref, r2_ref, aw_ref, cw_ref, u_ref, v_ref, o_ref, at_ref)
    even = e % 2 == 0

    @pl.when(e == 0)
    def _():
        o_ref[...] = jnp.zeros_like(o_ref)
        step(ht0_ref, None, None)

    @pl.when(jnp.logical_not(even))
    def _():
        step(ht1_ref, ht0_ref, 0)

    @pl.when(even & (e > 0) & (e < last))
    def _():
        step(ht0_ref, ht1_ref, keys_per_tile)

    @pl.when(e == last)
    def _():
        step(None, ht1_ref, keys_per_tile)


def peer_experts(u, bq, r2, aw, cw, exp_u, exp_v, layer):
    tm, te = PEER_TM, PEER_TE
    n_tok = u.shape[0]
    assert PEER_N_TILES % 2 == 0 and n_tok % tm == 0
    tiles_per_row_block = 8 * PEER_N_KEYS // te
    once = pl.Buffered(1)
    col_spec = pl.BlockSpec((PEER_HEADS * PEER_N_KEYS, tm), lambda i, e: (0, i), pipeline_mode=once)
    row_spec = pl.BlockSpec((1, PEER_HEADS, 8, tm),
                            lambda i, e: (jnp.maximum(e - 1, 0) // tiles_per_row_block, 0, 0, i))
    return pl.pallas_call(
        _peer_kernel,
        out_shape=jax.ShapeDtypeStruct((n_tok, D_MODEL), F32),
        grid=(n_tok // tm, PEER_N_TILES + 1),
        in_specs=[pl.BlockSpec((tm, D_MODEL), lambda i, e: (i, 0), pipeline_mode=once),
                  col_spec, col_spec,
                  row_spec, row_spec,
                  pl.BlockSpec((None, te, D_MODEL), lambda i, e: (layer, jnp.minimum(e, PEER_N_TILES - 1), 0)),
                  pl.BlockSpec((None, te, D_MODEL), lambda i, e: (layer, jnp.maximum(e - 1, 0), 0))],
        out_specs=pl.BlockSpec((tm, D_MODEL), lambda i, e: (i, 0), pipeline_mode=once),
        scratch_shapes=[pltpu.VMEM((te, tm), F32), pltpu.VMEM((te, tm), F32), pltpu.VMEM((te, tm), BF16)],
        compiler_params=_params(("parallel", "arbitrary"), 56),
        name="peer_experts",
    )(u, bq, r2, aw, cw, exp_u, exp_v)


def peer_ffn(u, w_q, sub_keys, exp_u, exp_v, layer):
    q = matmul(u, w_q, BF16, name="peer_query", layer=layer)
    bq, r2, aw, cw = peer_route(q, sub_keys, layer)
    return peer_experts(u, bq, r2, aw, cw, exp_u, exp_v, layer)


def mla_mixer(u, cache_ckv, cache_krope, w_dq, q_norm, w_uq, w_dkv, kv_norm, w_uk, w_uv, w_o):
    pad = (-(MLA_Q_LORA + MLA_KV_LORA + MLA_ROPE)) % 128
    w_cat = jnp.concatenate([w_dq, w_dkv, jnp.zeros((D_MODEL, pad), F32)], axis=1)
    cq, ckv, kr = mla_down(u, w_cat, q_norm, kv_norm)
    w_uq3 = w_uq.reshape(MLA_Q_LORA, MLA_HEADS, MLA_NOPE + MLA_ROPE)
    w_q_cat = jnp.concatenate([w_uq3[:, :, :MLA_NOPE].reshape(MLA_Q_LORA, -1),
                               w_uq3[:, :, MLA_NOPE:].reshape(MLA_Q_LORA, -1)], axis=1)
    q = matmul(cq, w_q_cat, BF16, name="mla_q_up")
    w_nope = MLA_HEADS * MLA_NOPE
    ckv_src = jnp.concatenate([ckv, cache_ckv.reshape(-1, MLA_KV_LORA)], axis=0).astype(BF16)
    kv_up = matmul(ckv_src, jnp.concatenate([w_uk, w_uv], axis=1), BF16, tm=512, tn=2048, name="mla_kv_up")
    cos, sin = (jnp.asarray(t, F32) for t in _rope_tables())
    w_rope = MLA_HEADS * MLA_ROPE
    qr_rot = rope_rows(q, N_PROMPT // 256, w_nope // w_rope, w_rope, N_SAMPLE, cos, sin, BF16)
    cos2, sin2 = (t.reshape(DEC_SEQ // 2, 128) for t in (cos[:, :MLA_ROPE], sin[:, :MLA_ROPE]))
    kr_pairs = kr[N_PROMPT:].reshape(N_SAMPLE // 2, 2 * MLA_ROPE)
    kr_rot = rope_rows(kr_pairs, 0, 0, 2 * MLA_ROPE, N_SAMPLE // 2, cos2, sin2, BF16).reshape(N_SAMPLE, MLA_ROPE)
    o_p = mla_attention_prompt(q, kv_up, kr)
    o_s = mla_attention_sample(q, qr_rot, kv_up, kr_rot, cache_krope.reshape(-1, MLA_ROPE))
    op = matmul_groups(o_p, o_s, w_o, F32, name="mla_out")
    return op, ckv, kr


def gqa_mixer(u, cache_k, cache_v, w_qkv, sink, w_o):
    nq = GQA_HEADS * GQA_HEAD_DIM
    nk = GQA_KV_HEADS * GQA_HEAD_DIM
    q = matmul(u, w_qkv, BF16, name="gqa_q", n=nq)
    kv = matmul(u, w_qkv, F32, name="gqa_kv", col0=nq)
    cos, sin = (jnp.asarray(t, F32) for t in _rope_tables())
    q_rot = rope_rows(q, N_PROMPT // 256, 0, nq, N_SAMPLE, cos, sin, BF16)
    k_rot = rope_rows(kv, N_PROMPT // 256, 0, nk, N_SAMPLE, cos, sin, BF16)
    o_p = gqa_attention_prompt(q, kv, sink)
    o_s = gqa_attention_sample(q_rot, k_rot, kv, cache_k.reshape(-1, nk), cache_v.reshape(-1, nk), sink)
    op = matmul_groups(o_p, o_s, w_o, F32, name="gqa_out")
    return op, kv


def fnet_mixer(u, w_out):
    f_p = fourier_real_2d(u, 0, BATCH, SEQ)
    f_s = fourier_real_2d(u, N_PROMPT // DEC_SEQ, DEC_BATCH, DEC_SEQ)
    return matmul_groups(f_p, f_s, w_out, F32, name="fnet_out")


def conv_mixer(u, w_in, conv_w, conv_b, w_out):
    h3 = matmul(u, w_in, BF16, name="conv_in")
    y_p = gated_conv(h3, conv_w, conv_b, 0, BATCH, SEQ)
    y_s = gated_conv(h3, conv_w, conv_b, N_PROMPT // DEC_SEQ, DEC_BATCH, DEC_SEQ)
    return matmul_groups(y_p, y_s, w_out, F32, name="conv_out")


def kernel(x_prompt, x_sample, cache_l0_ckv, cache_l0_krope, cache_l1_k, cache_l1_v, c, c_ctx, ada_w, ada_b, ln1_g, ln1_b, ln2_g, ln2_b, mla_w_dq, mla_q_norm, mla_w_uq, mla_w_dkv, mla_kv_norm, mla_w_uk, mla_w_uv, mla_w_o, gqa_w_qkv, gqa_sink, gqa_w_o, fnet_w_out, conv_w_in, conv_w, conv_b, conv_w_out, peer_w_q, peer_sub_keys, peer_u, peer_v):
    x = jnp.concatenate([x_prompt.reshape(N_PROMPT, D_MODEL), x_sample.reshape(N_SAMPLE, D_MODEL)], axis=0)
    cond = jnp.concatenate([c_ctx[None, :], c, jnp.zeros((N_COND - 1 - DEC_BATCH, D_MODEL), F32)], axis=0)
    mods_all = ada_modulation_all(cond, ada_w, ada_b).reshape(DEPTH, N_COND, 6, 1, D_MODEL)

    u = modulate_tokens(x, mods_all[0])
    new_ckv = new_krope = new_kv = None
    for i in range(DEPTH):
        mods = mods_all[i]
        if i == 0:
            delta, new_ckv, new_krope = mla_mixer(u, cache_l0_ckv, cache_l0_krope, mla_w_dq, mla_q_norm, mla_w_uq,
                                                  mla_w_dkv, mla_kv_norm, mla_w_uk, mla_w_uv, mla_w_o)
        elif i == 1:
            delta, new_kv = gqa_mixer(u, cache_l1_k, cache_l1_v, gqa_w_qkv, gqa_sink, gqa_w_o)
        elif i == 2:
            delta = fnet_mixer(u, fnet_w_out)
        else:
            delta = conv_mixer(u, conv_w_in, conv_w, conv_b, conv_w_out)
        x, u = post_norm(x, delta, mods, 2, ln1_g[i], ln1_b[i], mods, 3)
        delta = peer_ffn(u, peer_w_q, peer_sub_keys, peer_u, peer_v, i)
        if i + 1 < DEPTH:
            x, u = post_norm(x, delta, mods, 5, ln2_g[i], ln2_b[i], mods_all[i + 1], 0)
        else:
            y_prompt = post_norm(x, delta, mods, 5, ln2_g[i], ln2_b[i], rows=(0, N_PROMPT))
            y_sample = post_norm(x, delta, mods, 5, ln2_g[i], ln2_b[i], rows=(N_PROMPT, N_SAMPLE))

    nk = GQA_KV_HEADS * GQA_HEAD_DIM
    return (y_prompt.reshape(BATCH, SEQ, D_MODEL),
            y_sample.reshape(DEC_BATCH, DEC_SEQ, D_MODEL),
            new_ckv[:N_PROMPT].reshape(BATCH, SEQ, MLA_KV_LORA),
            new_krope[:N_PROMPT].reshape(BATCH, SEQ, MLA_ROPE),
            new_kv[:N_PROMPT, :nk].reshape(BATCH, SEQ, GQA_KV_HEADS, GQA_HEAD_DIM),
            new_kv[:N_PROMPT, nk:].reshape(BATCH, SEQ, GQA_KV_HEADS, GQA_HEAD_DIM))
```

```python
import functools
import math

import numpy as np
import jax
import jax.numpy as jnp
from jax import lax
from jax.experimental import pallas as pl
from jax.experimental.pallas import tpu as pltpu

F32 = jnp.float32
BF16 = jnp.bfloat16

D_MODEL = 2048
BATCH = 16
SEQ = 256
DEPTH = 4
DEC_BATCH = 2
DEC_SEQ = 1024
PAST_LEN = 256
GRID_W = 64
ROPE_BASE = 10000.0
LN_EPS = 1e-5
RMS_EPS = 1e-6
NEG_INF = -1e30
DEEPNORM_ALPHA = (2 * DEPTH) ** 0.25

MLA_HEADS = 16
MLA_Q_LORA = 512
MLA_KV_LORA = 512
MLA_NOPE = 128
MLA_ROPE = 64
MLA_V = 128
MLA_SCALE = (MLA_NOPE + MLA_ROPE) ** -0.5

GQA_HEADS = 32
GQA_KV_HEADS = 8
GQA_HEAD_DIM = 64
WINDOW = 128
GQA_SCALE = GQA_HEAD_DIM ** -0.5

FNET_GROUPS = 4
FNET_GROUP_DIM = D_MODEL // FNET_GROUPS

PEER_HEADS = 8
PEER_N_KEYS = 128
PEER_N_EXPERTS = PEER_N_KEYS * PEER_N_KEYS
PEER_HALF = 128
PEER_TOPK = 16

N_PROMPT = BATCH * SEQ
N_SAMPLE = DEC_BATCH * DEC_SEQ
N_TOK = N_PROMPT + N_SAMPLE
N_COND = 8

MIB = 1024 * 1024

NT_DIMS = (((1,), (1,)), ((), ()))
TN_DIMS = (((0,), (0,)), ((), ()))


def _params(sem, vmem_mib, flags=None):
    return pltpu.CompilerParams(dimension_semantics=sem, vmem_limit_bytes=vmem_mib * MIB, flags=flags)


def _group_of_tile(i, tm):
    return jnp.maximum((i * tm) // DEC_SEQ - (N_PROMPT // DEC_SEQ - 1), 0)


def _mod_spec(tm, which):
    return pl.BlockSpec((None, None, 1, D_MODEL), lambda i: (_group_of_tile(i, tm), which, 0, 0))


def _ada_kernel(c_ref, w_ref, b_ref, o_ref):
    c = c_ref[...]
    s = (c * jax.nn.sigmoid(c)).astype(BF16)
    o_ref[...] = jnp.dot(s, w_ref[...].astype(BF16), preferred_element_type=F32) + b_ref[...]


def ada_modulation_all(cond, ada_w, ada_b):
    tn = 1024
    n_out = 6 * D_MODEL
    return pl.pallas_call(
        _ada_kernel,
        out_shape=jax.ShapeDtypeStruct((DEPTH, N_COND, n_out), F32),
        grid=(DEPTH, n_out // tn),
        in_specs=[
            pl.BlockSpec((N_COND, D_MODEL), lambda l, j: (0, 0)),
            pl.BlockSpec((None, D_MODEL, tn), lambda l, j: (l, 0, j)),
            pl.BlockSpec((None, 1, tn), lambda l, j: (l, 0, j)),
        ],
        out_specs=pl.BlockSpec((None, N_COND, tn), lambda l, j: (l, 0, j)),
        compiler_params=_params(("parallel", "parallel"), 40),
        name="ada_modulation",
    )(cond, ada_w, ada_b.reshape(DEPTH, 1, n_out))


def _modulate_kernel(x_ref, sh_ref, sc_ref, u_ref):
    u_ref[...] = (x_ref[...] * (1.0 + sc_ref[...]) + sh_ref[...]).astype(u_ref.dtype)


def modulate_tokens(x, mods):
    tm = 256
    return pl.pallas_call(
        _modulate_kernel,
        out_shape=jax.ShapeDtypeStruct((N_TOK, D_MODEL), BF16),
        grid=(N_TOK // tm,),
        in_specs=[pl.BlockSpec((tm, D_MODEL), lambda i: (i, 0)), _mod_spec(tm, 0), _mod_spec(tm, 1)],
        out_specs=pl.BlockSpec((tm, D_MODEL), lambda i: (i, 0)),
        compiler_params=_params(("parallel",), 24),
        name="modulate",
    )(x, mods, mods)


def _layer_norm_rows(y, g, b):
    mu = jnp.mean(y, axis=-1, keepdims=True)
    yc = y - mu
    var = jnp.mean(yc * yc, axis=-1, keepdims=True)
    return yc * lax.rsqrt(var + LN_EPS) * g + b


def _post_norm_kernel(x_ref, d_ref, gate_ref, g_ref, b_ref, xo_ref):
    y = DEEPNORM_ALPHA * x_ref[...] + gate_ref[...] * d_ref[...].astype(F32)
    xo_ref[...] = _layer_norm_rows(y, g_ref[...], b_ref[...])


def _post_norm_mod_kernel(x_ref, d_ref, gate_ref, g_ref, b_ref, sh_ref, sc_ref, xo_ref, uo_ref):
    y = DEEPNORM_ALPHA * x_ref[...] + gate_ref[...] * d_ref[...].astype(F32)
    xn = _layer_norm_rows(y, g_ref[...], b_ref[...])
    xo_ref[...] = xn
    uo_ref[...] = (xn * (1.0 + sc_ref[...]) + sh_ref[...]).astype(uo_ref.dtype)


def post_norm(x, delta, mods, gate_idx, g, b, next_mods=None, next_idx=None, rows=(0, N_TOK)):
    tm = 256
    row = pl.BlockSpec((tm, D_MODEL), lambda i: (i, 0))
    vec = pl.BlockSpec((1, D_MODEL), lambda i: (0, 0))
    ins = [x, delta, mods, g.reshape(1, D_MODEL), b.reshape(1, D_MODEL)]
    specs = [row, row, _mod_spec(tm, gate_idx), vec, vec]
    if next_mods is None:
        i0 = rows[0] // tm
        row_in = pl.BlockSpec((tm, D_MODEL), lambda i: (i0 + i, 0))
        gate = pl.BlockSpec((None, None, 1, D_MODEL), lambda i: (_group_of_tile(i0 + i, tm), gate_idx, 0, 0))
        return pl.pallas_call(
            _post_norm_kernel,
            out_shape=jax.ShapeDtypeStruct((rows[1], D_MODEL), F32),
            grid=(rows[1] // tm,), in_specs=[row_in, row_in, gate, vec, vec], out_specs=row,
            compiler_params=_params(("parallel",), 32), name="post_norm",
        )(*ins)
    ins += [next_mods, next_mods]
    specs += [_mod_spec(tm, next_idx), _mod_spec(tm, next_idx + 1)]
    return pl.pallas_call(
        _post_norm_mod_kernel,
        out_shape=(jax.ShapeDtypeStruct((N_TOK, D_MODEL), F32), jax.ShapeDtypeStruct((N_TOK, D_MODEL), BF16)),
        grid=(N_TOK // tm,), in_specs=specs, out_specs=(row, row),
        compiler_params=_params(("parallel",), 32), name="post_norm_mod",
    )(*ins)


def _mm_kernel(x_ref, w_ref, o_ref):
    o_ref[...] = jnp.dot(x_ref[...], w_ref[...].astype(BF16), preferred_element_type=F32).astype(o_ref.dtype)


def _mm_groups_kernel(xp_ref, xs_ref, w_ref, o_ref, *, prompt_tiles):
    w = w_ref[...].astype(BF16)
    i = pl.program_id(0)

    @pl.when(i < prompt_tiles)
    def _():
        o_ref[...] = jnp.dot(xp_ref[...], w, preferred_element_type=F32).astype(o_ref.dtype)

    @pl.when(i >= prompt_tiles)
    def _():
        o_ref[...] = jnp.dot(xs_ref[...], w, preferred_element_type=F32).astype(o_ref.dtype)


def matmul_groups(x_prompt, x_sample, w, out_dtype, name):
    tm, tn = 1024, 512
    k = x_prompt.shape[1]
    n = w.shape[1]
    prompt_tiles = N_PROMPT // tm
    return pl.pallas_call(
        functools.partial(_mm_groups_kernel, prompt_tiles=prompt_tiles),
        out_shape=jax.ShapeDtypeStruct((N_TOK, n), out_dtype),
        grid=(N_TOK // tm, n // tn),
        in_specs=[pl.BlockSpec((tm, k), lambda i, j: (jnp.minimum(i, prompt_tiles - 1), 0)),
                  pl.BlockSpec((tm, k), lambda i, j: (jnp.maximum(i - prompt_tiles, 0), 0)),
                  pl.BlockSpec((k, tn), lambda i, j: (0, j))],
        out_specs=pl.BlockSpec((tm, tn), lambda i, j: (i, j)),
        compiler_params=_params(("parallel", "parallel"), 48),
        name=name,
    )(x_prompt, x_sample, w)


def matmul(x, w, out_dtype, tm=2048, tn=512, name="matmul", layer=None, col0=0, n=None):
    m, k = x.shape
    n = w.shape[-1] - col0 if n is None else n
    tm = min(tm, m)
    tn = min(tn, n)
    assert m % tm == 0 and n % tn == 0 and col0 % tn == 0, (m, n, tm, tn, col0)
    j0 = col0 // tn
    if layer is None:
        w_spec = pl.BlockSpec((k, tn), lambda i, j: (0, j0 + j))
    else:
        w_spec = pl.BlockSpec((None, k, tn), lambda i, j: (layer, 0, j0 + j))
    return pl.pallas_call(
        _mm_kernel,
        out_shape=jax.ShapeDtypeStruct((m, n), out_dtype),
        grid=(m // tm, n // tn),
        in_specs=[pl.BlockSpec((tm, k), lambda i, j: (i, 0)), w_spec],
        out_specs=pl.BlockSpec((tm, tn), lambda i, j: (i, j)),
        compiler_params=_params(("parallel", "parallel"), 48),
        name=name,
    )(x, w)


def _rms_rows(y, g):
    return y * lax.rsqrt(jnp.mean(y * y, axis=-1, keepdims=True) + RMS_EPS) * g


def _mla_down_kernel(x_ref, w_ref, qn_ref, kvn_ref, cq_ref, ckv_ref, kr_ref):
    y = jnp.dot(x_ref[...], w_ref[...].astype(BF16), preferred_element_type=F32)
    cq_ref[...] = _rms_rows(y[:, :MLA_Q_LORA], qn_ref[...]).astype(cq_ref.dtype)
    ckv_ref[...] = _rms_rows(y[:, MLA_Q_LORA:MLA_Q_LORA + MLA_KV_LORA], kvn_ref[...])
    kr_ref[...] = y[:, MLA_Q_LORA + MLA_KV_LORA:MLA_Q_LORA + MLA_KV_LORA + MLA_ROPE]


def mla_down(u, w_cat, q_norm, kv_norm):
    tm = 512
    n = w_cat.shape[1]
    return pl.pallas_call(
        _mla_down_kernel,
        out_shape=(jax.ShapeDtypeStruct((N_TOK, MLA_Q_LORA), BF16),
                   jax.ShapeDtypeStruct((N_TOK, MLA_KV_LORA), F32),
                   jax.ShapeDtypeStruct((N_TOK, MLA_ROPE), F32)),
        grid=(N_TOK // tm,),
        in_specs=[pl.BlockSpec((tm, D_MODEL), lambda i: (i, 0)),
                  pl.BlockSpec((D_MODEL, n), lambda i: (0, 0)),
                  pl.BlockSpec((1, MLA_Q_LORA), lambda i: (0, 0)),
                  pl.BlockSpec((1, MLA_KV_LORA), lambda i: (0, 0))],
        out_specs=(pl.BlockSpec((tm, MLA_Q_LORA), lambda i: (i, 0)),
                   pl.BlockSpec((tm, MLA_KV_LORA), lambda i: (i, 0)),
                   pl.BlockSpec((tm, MLA_ROPE), lambda i: (i, 0))),
        compiler_params=_params(("parallel",), 48),
        name="mla_down",
    )(u, w_cat, q_norm.reshape(1, -1), kv_norm.reshape(1, -1))


def _rope_tables():
    t = np.arange(DEC_SEQ)
    quarter = MLA_ROPE // 4
    inv_freq = ROPE_BASE ** (-np.arange(quarter, dtype=np.float64) / quarter)
    ang_row = (t // GRID_W)[:, None] * inv_freq[None, :]
    ang_col = (t % GRID_W)[:, None] * inv_freq[None, :]
    cos = np.concatenate([np.cos(ang_row)] * 2 + [np.cos(ang_col)] * 2, -1)
    sin = np.concatenate([-np.sin(ang_row), np.sin(ang_row), -np.sin(ang_col), np.sin(ang_col)], -1)
    return np.tile(cos, (1, 2)), np.tile(sin, (1, 2))


def _rope_kernel(x_ref, c_ref, s_ref, o_ref):
    x = x_ref[...].astype(F32)
    w = x.shape[-1]
    reps = w // c_ref.shape[-1]
    cos = jnp.tile(c_ref[...], (1, reps))
    sin = jnp.tile(s_ref[...], (1, reps))
    lane = lax.broadcasted_iota(jnp.int32, x.shape, 1)
    partner = jnp.where((lane % 32) < 16, pltpu.roll(x, w - 16, 1), pltpu.roll(x, 16, 1))
    o_ref[...] = (x * cos + partner * sin).astype(o_ref.dtype)


def rope_rows(x, row_block0, col_block, width, n_rows, cos, sin, out_dtype, tr=256):
    t_blocks = cos.shape[0] // tr
    return pl.pallas_call(
        _rope_kernel,
        out_shape=jax.ShapeDtypeStruct((n_rows, width), out_dtype),
        grid=(n_rows // tr,),
        in_specs=[pl.BlockSpec((tr, width), lambda i: (row_block0 + i, col_block)),
                  pl.BlockSpec((tr, 128), lambda i: (i % t_blocks, 0)),
                  pl.BlockSpec((tr, 128), lambda i: (i % t_blocks, 0))],
        out_specs=pl.BlockSpec((tr, width), lambda i: (i, 0)),
        compiler_params=_params(("parallel",), 32),
        name="axial_rope",
    )(x, cos, sin)


def _softmax_parts(parts, sink):
    m = parts[0].max(axis=-1, keepdims=True)
    for s in parts[1:]:
        m = jnp.maximum(m, s.max(axis=-1, keepdims=True))
    if sink is not None:
        m = jnp.maximum(m, sink)
    ps = [jnp.exp(s - m) for s in parts]
    l = ps[0].sum(axis=-1, keepdims=True)
    for p in ps[1:]:
        l = l + p.sum(axis=-1, keepdims=True)
    if sink is not None:
        l = l + jnp.exp(sink - m)
    return ps, 1.0 / l


def _mla_attn_kernel(*refs, with_ctx):
    if with_ctx:
        qn_ref, qr_ref, kn_ref, v_ref, kr_ref, kn2_ref, v2_ref, kr2_ref, o_ref = refs
    else:
        qn_ref, qr_ref, kn_ref, v_ref, kr_ref, o_ref = refs
    kr = kr_ref[...].astype(BF16)
    kr2 = kr2_ref[...].astype(BF16) if with_ctx else None
    for h in range(MLA_HEADS):
        n0, n1 = h * MLA_NOPE, (h + 1) * MLA_NOPE
        qn = qn_ref[:, n0:n1]
        qr = qr_ref[:, h * MLA_ROPE:(h + 1) * MLA_ROPE]
        s = (lax.dot_general(qn, kn_ref[:, n0:n1], NT_DIMS, preferred_element_type=F32)
             + lax.dot_general(qr, kr, NT_DIMS, preferred_element_type=F32)) * MLA_SCALE
        parts = [s]
        if with_ctx:
            s2 = (lax.dot_general(qn, kn2_ref[:, n0:n1], NT_DIMS, preferred_element_type=F32)
                  + lax.dot_general(qr, kr2, NT_DIMS, preferred_element_type=F32)) * MLA_SCALE
            parts.append(s2)
        ps, inv_l = _softmax_parts(parts, None)
        o = jnp.dot(ps[0].astype(BF16), v_ref[:, n0:n1], preferred_element_type=F32)
        if with_ctx:
            o = o + jnp.dot(ps[1].astype(BF16), v2_ref[:, n0:n1], preferred_element_type=F32)
        o_ref[:, n0:n1] = (o * inv_l).astype(o_ref.dtype)


def mla_attention_prompt(q, kv_up, kr):
    w = MLA_HEADS * MLA_NOPE
    return pl.pallas_call(
        functools.partial(_mla_attn_kernel, with_ctx=False),
        out_shape=jax.ShapeDtypeStruct((N_PROMPT, w), BF16),
        grid=(BATCH,),
        in_specs=[pl.BlockSpec((SEQ, w), lambda b: (b, 0)),
                  pl.BlockSpec((SEQ, MLA_HEADS * MLA_ROPE), lambda b: (b, w // (MLA_HEADS * MLA_ROPE))),
                  pl.BlockSpec((SEQ, w), lambda b: (b, 0)),
                  pl.BlockSpec((SEQ, w), lambda b: (b, 1)),
                  pl.BlockSpec((SEQ, MLA_ROPE), lambda b: (b, 0))],
        out_specs=pl.BlockSpec((SEQ, w), lambda b: (b, 0)),
        compiler_params=_params(("parallel",), 32),
        name="mla_attention_prompt",
    )(q, q, kv_up, kv_up, kr)


def mla_attention_sample(qn, qr_rot, kv_up, kr_rot, kr_cache):
    w = MLA_HEADS * MLA_NOPE
    tq = 256
    nq = DEC_SEQ // tq
    q0 = N_PROMPT // tq
    lat0 = N_PROMPT // DEC_SEQ
    ctx0 = N_TOK // PAST_LEN
    return pl.pallas_call(
        functools.partial(_mla_attn_kernel, with_ctx=True),
        out_shape=jax.ShapeDtypeStruct((N_SAMPLE, w), BF16),
        grid=(DEC_BATCH, nq),
        in_specs=[pl.BlockSpec((tq, w), lambda b, i: (q0 + b * nq + i, 0)),
                  pl.BlockSpec((tq, MLA_HEADS * MLA_ROPE), lambda b, i: (b * nq + i, 0)),
                  pl.BlockSpec((DEC_SEQ, w), lambda b, i: (lat0 + b, 0)),
                  pl.BlockSpec((DEC_SEQ, w), lambda b, i: (lat0 + b, 1)),
                  pl.BlockSpec((DEC_SEQ, MLA_ROPE), lambda b, i: (b, 0)),
                  pl.BlockSpec((PAST_LEN, w), lambda b, i: (ctx0 + b, 0)),
                  pl.BlockSpec((PAST_LEN, w), lambda b, i: (ctx0 + b, 1)),
                  pl.BlockSpec((PAST_LEN, MLA_ROPE), lambda b, i: (b, 0))],
        out_specs=pl.BlockSpec((tq, w), lambda b, i: (b * nq + i, 0)),
        compiler_params=_params(("parallel", "parallel"), 56),
        name="mla_attention_sample",
    )(qn, qr_rot, kv_up, kv_up, kr_rot, kv_up, kv_up, kr_cache)


def _gqa_attn_kernel(*refs, with_ctx, tq):
    if with_ctx:
        sink_ref, q_ref, k_ref, v_ref, k2_ref, v2_ref, o_ref = refs
    else:
        sink_ref, q_ref, k_ref, v_ref, o_ref = refs
    group = GQA_HEADS // GQA_KV_HEADS
    d = GQA_HEAD_DIM
    if with_ctx:
        t = pl.program_id(1) * tq + lax.broadcasted_iota(jnp.int32, (tq, DEC_SEQ), 0)
        s_pos = lax.broadcasted_iota(jnp.int32, (tq, DEC_SEQ), 1)
        in_window = jnp.abs(t - s_pos) <= WINDOW
    for hk in range(GQA_KV_HEADS):
        k = k_ref[:, hk * d:(hk + 1) * d].astype(BF16)
        v = v_ref[:, hk * d:(hk + 1) * d].astype(BF16)
        if with_ctx:
            k2 = k2_ref[:, hk * d:(hk + 1) * d].astype(BF16)
            v2 = v2_ref[:, hk * d:(hk + 1) * d].astype(BF16)
        for g in range(group):
            h = hk * group + g
            q = q_ref[:, h * d:(h + 1) * d]
            s = lax.dot_general(q, k, NT_DIMS, preferred_element_type=F32) * GQA_SCALE
            parts = [s]
            if with_ctx:
                parts = [jnp.where(in_window, s, NEG_INF),
                         lax.dot_general(q, k2, NT_DIMS, preferred_element_type=F32) * GQA_SCALE]
            ps, inv_l = _softmax_parts(parts, sink_ref[h])
            o = jnp.dot(ps[0].astype(BF16), v, preferred_element_type=F32)
            if with_ctx:
                o = o + jnp.dot(ps[1].astype(BF16), v2, preferred_element_type=F32)
            o_ref[:, h * d:(h + 1) * d] = (o * inv_l).astype(o_ref.dtype)


def gqa_attention_prompt(q, kv, sink):
    wq = GQA_HEADS * GQA_HEAD_DIM
    wk = GQA_KV_HEADS * GQA_HEAD_DIM
    return pl.pallas_call(
        functools.partial(_gqa_attn_kernel, with_ctx=False, tq=SEQ),
        out_shape=jax.ShapeDtypeStruct((N_PROMPT, wq), BF16),
        grid=(BATCH,),
        in_specs=[pl.BlockSpec(memory_space=pltpu.SMEM),
                  pl.BlockSpec((SEQ, wq), lambda b: (b, 0)),
                  pl.BlockSpec((SEQ, wk), lambda b: (b, 0)),
                  pl.BlockSpec((SEQ, wk), lambda b: (b, 1))],
        out_specs=pl.BlockSpec((SEQ, wq), lambda b: (b, 0)),
        compiler_params=_params(("parallel",), 32),
        name="gqa_attention_prompt",
    )(sink, q, kv, kv)


def gqa_attention_sample(q_rot, k_rot, kv, k_cache, v_cache, sink):
    wq = GQA_HEADS * GQA_HEAD_DIM
    wk = GQA_KV_HEADS * GQA_HEAD_DIM
    tq = 256
    nq = DEC_SEQ // tq
    lat0 = N_PROMPT // DEC_SEQ
    return pl.pallas_call(
        functools.partial(_gqa_attn_kernel, with_ctx=True, tq=tq),
        out_shape=jax.ShapeDtypeStruct((N_SAMPLE, wq), BF16),
        grid=(DEC_BATCH, nq),
        in_specs=[pl.BlockSpec(memory_space=pltpu.SMEM),
                  pl.BlockSpec((tq, wq), lambda b, i: (b * nq + i, 0)),
                  pl.BlockSpec((DEC_SEQ, wk), lambda b, i: (b, 0)),
                  pl.BlockSpec((DEC_SEQ, wk), lambda b, i: (lat0 + b, 1)),
                  pl.BlockSpec((PAST_LEN, wk), lambda b, i: (b, 0)),
                  pl.BlockSpec((PAST_LEN, wk), lambda b, i: (b, 0))],
        out_specs=pl.BlockSpec((tq, wq), lambda b, i: (b * nq + i, 0)),
        compiler_params=_params(("parallel", "parallel"), 56),
        name="gqa_attention_sample",
    )(sink, q_rot, k_rot, kv, k_cache, v_cache)


def _dft_tables(n):
    jk = (np.arange(n)[:, None] * np.arange(n)[None, :]) % n
    ang = 2.0 * np.pi * jk / n
    return np.cos(ang) / math.sqrt(n), np.sin(ang) / math.sqrt(n)


def _fnet_kernel(x_ref, ct_ref, st_ref, cc_ref, sc_ref, o_ref):
    x = x_ref[...]
    y1 = jnp.dot(x, cc_ref[...], preferred_element_type=F32).astype(BF16)
    y2 = jnp.dot(x, sc_ref[...], preferred_element_type=F32).astype(BF16)
    o = (jnp.dot(ct_ref[...], y1, preferred_element_type=F32)
         - jnp.dot(st_ref[...], y2, preferred_element_type=F32))
    o_ref[...] = o.astype(o_ref.dtype)


def fourier_real_2d(u, seq_block0, n_seq, t):
    ct, st = (jnp.asarray(a, BF16) for a in _dft_tables(t))
    cc, sc = (jnp.asarray(a, BF16) for a in _dft_tables(FNET_GROUP_DIM))
    gd = FNET_GROUP_DIM
    return pl.pallas_call(
        _fnet_kernel,
        out_shape=jax.ShapeDtypeStruct((n_seq * t, D_MODEL), BF16),
        grid=(n_seq, FNET_GROUPS),
        in_specs=[pl.BlockSpec((t, gd), lambda b, g: (seq_block0 + b, g)),
                  pl.BlockSpec((t, t), lambda b, g: (0, 0)),
                  pl.BlockSpec((t, t), lambda b, g: (0, 0)),
                  pl.BlockSpec((gd, gd), lambda b, g: (0, 0)),
                  pl.BlockSpec((gd, gd), lambda b, g: (0, 0))],
        out_specs=pl.BlockSpec((t, gd), lambda b, g: (b, g)),
        compiler_params=_params(("parallel", "parallel"), 32),
        name="fourier_mix",
    )(u, ct, st, cc, sc)


def _conv_kernel(b_ref, c_ref, h_ref, w_ref, cb_ref, o_ref):
    z = c_ref[...].astype(F32) * h_ref[...].astype(F32)
    t = z.shape[0]
    row = lax.broadcasted_iota(jnp.int32, z.shape, 0)
    z_prev = jnp.where(row == 0, 0.0, pltpu.roll(z, 1, 0))
    z_next = jnp.where(row == t - 1, 0.0, pltpu.roll(z, t - 1, 0))
    conv = z_prev * w_ref[0:1, :] + z * w_ref[1:2, :] + z_next * w_ref[2:3, :] + cb_ref[...]
    o_ref[...] = (b_ref[...].astype(F32) * conv).astype(o_ref.dtype)


def gated_conv(h3, conv_w, conv_b, seq_block0, n_seq, t):
    tn = 512
    nj = D_MODEL // tn
    return pl.pallas_call(
        _conv_kernel,
        out_shape=jax.ShapeDtypeStruct((n_seq * t, D_MODEL), BF16),
        grid=(n_seq, nj),
        in_specs=[pl.BlockSpec((t, tn), lambda s, j: (seq_block0 + s, j)),
                  pl.BlockSpec((t, tn), lambda s, j: (seq_block0 + s, nj + j)),
                  pl.BlockSpec((t, tn), lambda s, j: (seq_block0 + s, 2 * nj + j)),
                  pl.BlockSpec((3, tn), lambda s, j: (0, j)),
                  pl.BlockSpec((1, tn), lambda s, j: (0, j))],
        out_specs=pl.BlockSpec((t, tn), lambda s, j: (s, j)),
        compiler_params=_params(("parallel", "parallel"), 32),
        name="gated_conv",
    )(h3, h3, h3, conv_w, conv_b.reshape(1, D_MODEL))


_N_RANKS = PEER_TOPK + 1
_CAND_PAIRS = [(a, b) for a in range(_N_RANKS) for b in range(_N_RANKS) if (a + 1) * (b + 1) <= _N_RANKS]


_NO_RANK = 127.0


def _top_values_exact(s, k, want_rank):
    n = s.shape[0]
    idx = lax.broadcasted_iota(jnp.int32, s.shape, 0)
    rank = jnp.full(s.shape, _NO_RANK, F32) if want_rank else None
    out = []
    for it in range(k):
        m = jnp.max(s, axis=0, keepdims=True)
        out.append(m)
        first = jnp.min(jnp.where(s == m, idx, n), axis=0, keepdims=True)
        hit = idx == first
        s = jnp.where(hit, -jnp.inf, s)
        if want_rank:
            rank = jnp.where(hit, float(it), rank)
    return out, rank


def _all_equal(x, value):
    return jnp.min(jnp.where(x == value, 1.0, 0.0)) > 0.5


def _top_values_distinct(s, k, n_pad):
    out = []
    for it in range(k):
        m = jnp.max(s, axis=0, keepdims=True)
        out.append(m)
        s = jnp.where(s == m, -jnp.inf, s)
    removed = jnp.sum(jnp.where(s == -jnp.inf, 1.0, 0.0), axis=0, keepdims=True)
    return out, _all_equal(removed, float(k + n_pad))


def _batcher_network(n):
    def merge(lo, hi, r):
        step = r * 2
        if step < hi - lo:
            yield from merge(lo, hi, step)
            yield from merge(lo + r, hi, step)
            yield from [(i, i + r) for i in range(lo + r, hi - r, step)]
        else:
            yield (lo, lo + r)

    def sort(lo, hi):
        if hi - lo >= 1:
            mid = lo + (hi - lo) // 2
            yield from sort(lo, mid)
            yield from sort(mid + 1, hi)
            yield from merge(lo, hi, 1)

    return list(sort(0, n - 1))


_SUBLANES = 8
_KEY_TILES = PEER_N_KEYS // _SUBLANES
_SORT_KEY_TILES = _batcher_network(_KEY_TILES)


def _top_values_sorted(s, k):
    tiles = [s[j * _SUBLANES:(j + 1) * _SUBLANES, :] for j in range(_KEY_TILES)]
    for i, j in _SORT_KEY_TILES:
        tiles[i], tiles[j] = jnp.maximum(tiles[i], tiles[j]), jnp.minimum(tiles[i], tiles[j])
    out = []
    pops = jnp.zeros_like(tiles[0])
    for it in range(k):
        m = jnp.max(tiles[0], axis=0, keepdims=True)
        out.append(m)
        hit = tiles[0] == m
        pops = pops + jnp.where(hit, 1.0, 0.0)
        depth = k - 1 - it
        for j in range(min(depth, _KEY_TILES - 1)):
            tiles[j] = jnp.where(hit, tiles[j + 1], tiles[j])
        if depth >= _KEY_TILES - 1:
            tiles[-1] = jnp.where(hit, -jnp.inf, tiles[-1])
    return out, _all_equal(jnp.sum(pops, axis=0, keepdims=True), float(k))


def _ranks_from_values(s, values):
    rank = jnp.full(s.shape, _NO_RANK, F32)
    for r in reversed(range(len(values))):
        rank = jnp.where(s >= values[r], float(r), rank)
    return rank


def _dup_bf16_words(x):
    bits = pltpu.bitcast(x.astype(BF16).astype(F32), jnp.int32)
    return bits | lax.shift_right_logical(bits, jnp.int32(16))


def _route_head(h, s1, s2, fast, bq_ref, r2_ref, aw_ref, cw_ref):
    tm = s1.shape[1]
    if fast:
        v1, ok1 = _top_values_sorted(s1, _N_RANKS)
        v2, ok2 = _top_values_sorted(s2, _N_RANKS)
        rank2 = _ranks_from_values(s2, v2)
    else:
        v1, _ = _top_values_exact(s1, _N_RANKS, False)
        v2, rank2 = _top_values_exact(s2, _N_RANKS, True)
    n_pad = (-len(_CAND_PAIRS)) % 8
    n_cand = len(_CAND_PAIRS) + n_pad
    cand_row = lax.broadcasted_iota(jnp.int32, (n_cand, tm), 0)
    cand = jnp.full((n_cand, tm), -jnp.inf, F32)
    for r, (a, b) in enumerate(_CAND_PAIRS):
        cand = jnp.where(cand_row == r, v1[a] + v2[b], cand)
    if fast:
        top, ok3 = _top_values_distinct(cand, _N_RANKS, n_pad)
    else:
        top, _ = _top_values_exact(cand, _N_RANKS, False)
    tau = 0.5 * (top[PEER_TOPK - 1] + top[PEER_TOPK])
    z = jnp.exp(top[0] - top[0])
    for kk in range(1, PEER_TOPK):
        z = z + jnp.exp(top[kk] - top[0])
    inv_z = 1.0 / z
    count = jnp.zeros_like(s1)
    for b in range(PEER_TOPK):
        count = jnp.where(s1 >= tau - v2[b], float(b + 1), count)
    rows = slice(h * PEER_N_KEYS, (h + 1) * PEER_N_KEYS)
    bq_ref[rows, :] = (jnp.exp(s2 - v2[0]) * inv_z).astype(BF16)
    r2_ref[rows, :] = rank2.astype(BF16)
    a_half = 0.5 * jnp.exp(s1 - v1[0])
    aw_ref[:, h, :, :] = _dup_bf16_words(a_half).reshape(PEER_N_KEYS // 8, 8, tm)
    cw_ref[:, h, :, :] = _dup_bf16_words(count).reshape(PEER_N_KEYS // 8, 8, tm)
    return jnp.logical_and(jnp.logical_and(ok1, ok2), ok3) if fast else None


def _route_kernel(q_ref, sk_ref, bq_ref, r2_ref, aw_ref, cw_ref):
    outs = (bq_ref, r2_ref, aw_ref, cw_ref)
    for h in range(PEER_HEADS):
        s = []
        for c in range(2):
            hc = 2 * h + c
            s.append(lax.dot_general(sk_ref[hc].astype(BF16), q_ref[:, hc * PEER_HALF:(hc + 1) * PEER_HALF],
                                     NT_DIMS, preferred_element_type=F32))
        no_ties = _route_head(h, s[0], s[1], True, *outs)

        @pl.when(jnp.logical_not(no_ties))
        def _():
            _route_head(h, s[0], s[1], False, *outs)


def peer_route(q, sub_keys, layer):
    tm = 256
    n_tok = q.shape[0]
    sk = sub_keys.reshape(-1, PEER_HEADS * 2, PEER_N_KEYS, PEER_HALF)
    tiles = PEER_N_KEYS // 8
    col_out = jax.ShapeDtypeStruct((PEER_HEADS * PEER_N_KEYS, n_tok), BF16)
    col_spec = pl.BlockSpec((PEER_HEADS * PEER_N_KEYS, tm), lambda i: (0, i))
    row_out = jax.ShapeDtypeStruct((tiles, PEER_HEADS, 8, n_tok), jnp.int32)
    row_spec = pl.BlockSpec((tiles, PEER_HEADS, 8, tm), lambda i: (0, 0, 0, i))
    return pl.pallas_call(
        _route_kernel,
        out_shape=(col_out, col_out, row_out, row_out),
        grid=(n_tok // tm,),
        in_specs=[pl.BlockSpec((tm, PEER_HEADS * 2 * PEER_HALF), lambda i: (i, 0)),
                  pl.BlockSpec((None, PEER_HEADS * 2, PEER_N_KEYS, PEER_HALF), lambda i: (layer, 0, 0, 0))],
        out_specs=(col_spec, col_spec, row_spec, row_spec),
        compiler_params=_params(("parallel",), 32),
        name="peer_route",
    )(q, sk)


PEER_TE = 512
PEER_TM = 1024
PEER_CHUNK = 512
PEER_N_TILES = PEER_N_EXPERTS // PEER_TE


def _bf16_rows(word_row, n_rows):
    return pltpu.bitcast(jnp.broadcast_to(word_row, (n_rows // 2, word_row.shape[1])), BF16)


PEER_LANES = 256
PEER_ROWS = 128


def _peer_gated_slab(k, k0, bq_ref, r2_ref, aw_ref, cw_ref, ht_ref, at_ref):
    for j in range(ht_ref.shape[1] // PEER_LANES):
        lanes = slice(j * PEER_LANES, (j + 1) * PEER_LANES)
        for r in range(PEER_N_KEYS // PEER_ROWS):
            g = [None, None]
            for h in range(PEER_HEADS):
                count = _bf16_rows(cw_ref[0, h, k0 + k:k0 + k + 1, lanes], PEER_ROWS)
                a_half = _bf16_rows(aw_ref[0, h, k0 + k:k0 + k + 1, lanes], PEER_ROWS)
                rows = slice(h * PEER_N_KEYS + r * PEER_ROWS, h * PEER_N_KEYS + (r + 1) * PEER_ROWS)
                bq = bq_ref[rows, lanes]
                contrib = jnp.where(r2_ref[rows, lanes] < count, bq, jnp.zeros_like(bq)) * a_half
                g[h % 2] = contrib if g[h % 2] is None else g[h % 2] + contrib
            krows = slice(k * PEER_N_KEYS + r * PEER_ROWS, k * PEER_N_KEYS + (r + 1) * PEER_ROWS)
            hk = ht_ref[krows, lanes]
            gelu2 = hk * (1.0 + lax.erf(hk * math.sqrt(0.5)))
            at_ref[krows, lanes] = gelu2.astype(BF16) * (g[0] + g[1])


def _peer_step(x_ref, bq_ref, r2_ref, aw_ref, cw_ref, u_ref, v_ref, o_ref, at_ref, ht_write, ht_read, k0):
    keys_per_tile = PEER_TE // PEER_N_KEYS
    keys_per_chunk = PEER_CHUNK // PEER_N_KEYS
    tm = x_ref.shape[0]
    tok = tm // keys_per_tile
    u = u_ref[...].astype(BF16) if ht_write is not None else None
    for k in range(keys_per_tile):
        if ht_read is not None:
            _peer_gated_slab(k, k0, bq_ref, r2_ref, aw_ref, cw_ref, ht_read, at_ref)
        if ht_write is not None:
            ht_write[:, k * tok:(k + 1) * tok] = lax.dot_general(
                u, x_ref[k * tok:(k + 1) * tok, :], NT_DIMS, preferred_element_type=F32)
        if ht_read is not None and (k + 1) % keys_per_chunk == 0:
            chunk = slice((k + 1) * PEER_N_KEYS - PEER_CHUNK, (k + 1) * PEER_N_KEYS)
            o_ref[...] += lax.dot_general(at_ref[chunk, :], v_ref[chunk, :].astype(BF16), TN_DIMS,
                                          preferred_element_type=F32)


def _peer_kernel(x_ref, bq_ref, r2_ref, aw_ref, cw_ref, u_ref, v_ref, o_ref, ht0_ref, ht1_ref, at_ref):
    e = pl.program_id(1)
    last = PEER_N_TILES
    keys_per_tile = PEER_TE // PEER_N_KEYS
    assert 8 // keys_per_tile == 2
    step = functools.partial(_peer_step, x_ref, bq_ref, r2_ref, aw_ref, cw_ref, u_ref, v_ref, o_ref, at_ref)
    even = e % 2 == 0

    @pl.when(e == 0)
    def _():
        o_ref[...] = jnp.zeros_like(o_ref)
        step(ht0_ref, None, None)

    @pl.when(jnp.logical_not(even))
    def _():
        step(ht1_ref, ht0_ref, 0)

    @pl.when(even & (e > 0) & (e < last))
    def _():
        step(ht0_ref, ht1_ref, keys_per_tile)

    @pl.when(e == last)
    def _():
        step(None, ht1_ref, keys_per_tile)


def peer_experts(u, bq, r2, aw, cw, exp_u, exp_v, layer):
    tm, te = PEER_TM, PEER_TE
    n_tok = u.shape[0]
    assert PEER_N_TILES % 2 == 0 and n_tok % tm == 0
    tiles_per_row_block = 8 * PEER_N_KEYS // te
    once = pl.Buffered(1)
    col_spec = pl.BlockSpec((PEER_HEADS * PEER_N_KEYS, tm), lambda i, e: (0, i), pipeline_mode=once)
    row_spec = pl.BlockSpec((1, PEER_HEADS, 8, tm),
                            lambda i, e: (jnp.maximum(e - 1, 0) // tiles_per_row_block, 0, 0, i))
    return pl.pallas_call(
        _peer_kernel,
        out_shape=jax.ShapeDtypeStruct((n_tok, D_MODEL), F32),
        grid=(n_tok // tm, PEER_N_TILES + 1),
        in_specs=[pl.BlockSpec((tm, D_MODEL), lambda i, e: (i, 0), pipeline_mode=once),
                  col_spec, col_spec,
                  row_spec, row_spec,
                  pl.BlockSpec((None, te, D_MODEL), lambda i, e: (layer, jnp.minimum(e, PEER_N_TILES - 1), 0)),
                  pl.BlockSpec((None, te, D_MODEL), lambda i, e: (layer, jnp.maximum(e - 1, 0), 0))],
        out_specs=pl.BlockSpec((tm, D_MODEL), lambda i, e: (i, 0), pipeline_mode=once),
        scratch_shapes=[pltpu.VMEM((te, tm), F32), pltpu.VMEM((te, tm), F32), pltpu.VMEM((te, tm), BF16)],
        compiler_params=_params(("parallel", "arbitrary"), 56),
        name="peer_experts",
    )(u, bq, r2, aw, cw, exp_u, exp_v)


def peer_ffn(u, w_q, sub_keys, exp_u, exp_v, layer):
    q = matmul(u, w_q, BF16, name="peer_query", layer=layer)
    bq, r2, aw, cw = peer_route(q, sub_keys, layer)
    return peer_experts(u, bq, r2, aw, cw, exp_u, exp_v, layer)


def mla_mixer(u, cache_ckv, cache_krope, w_dq, q_norm, w_uq, w_dkv, kv_norm, w_uk, w_uv, w_o):
    pad = (-(MLA_Q_LORA + MLA_KV_LORA + MLA_ROPE)) % 128
    w_cat = jnp.concatenate([w_dq, w_dkv, jnp.zeros((D_MODEL, pad), F32)], axis=1)
    cq, ckv, kr = mla_down(u, w_cat, q_norm, kv_norm)
    w_uq3 = w_uq.reshape(MLA_Q_LORA, MLA_HEADS, MLA_NOPE + MLA_ROPE)
    w_q_cat = jnp.concatenate([w_uq3[:, :, :MLA_NOPE].reshape(MLA_Q_LORA, -1),
                               w_uq3[:, :, MLA_NOPE:].reshape(MLA_Q_LORA, -1)], axis=1)
    q = matmul(cq, w_q_cat, BF16, name="mla_q_up")
    w_nope = MLA_HEADS * MLA_NOPE
    ckv_src = jnp.concatenate([ckv, cache_ckv.reshape(-1, MLA_KV_LORA)], axis=0).astype(BF16)
    kv_up = matmul(ckv_src, jnp.concatenate([w_uk, w_uv], axis=1), BF16, tm=512, tn=2048, name="mla_kv_up")
    cos, sin = (jnp.asarray(t, F32) for t in _rope_tables())
    w_rope = MLA_HEADS * MLA_ROPE
    qr_rot = rope_rows(q, N_PROMPT // 256, w_nope // w_rope, w_rope, N_SAMPLE, cos, sin, BF16)
    cos2, sin2 = (t.reshape(DEC_SEQ // 2, 128) for t in (cos[:, :MLA_ROPE], sin[:, :MLA_ROPE]))
    kr_pairs = kr[N_PROMPT:].reshape(N_SAMPLE // 2, 2 * MLA_ROPE)
    kr_rot = rope_rows(kr_pairs, 0, 0, 2 * MLA_ROPE, N_SAMPLE // 2, cos2, sin2, BF16).reshape(N_SAMPLE, MLA_ROPE)
    o_p = mla_attention_prompt(q, kv_up, kr)
    o_s = mla_attention_sample(q, qr_rot, kv_up, kr_rot, cache_krope.reshape(-1, MLA_ROPE))
    op = matmul_groups(o_p, o_s, w_o, F32, name="mla_out")
    return op, ckv, kr


def gqa_mixer(u, cache_k, cache_v, w_qkv, sink, w_o):
    nq = GQA_HEADS * GQA_HEAD_DIM
    nk = GQA_KV_HEADS * GQA_HEAD_DIM
    q = matmul(u, w_qkv, BF16, name="gqa_q", n=nq)
    kv = matmul(u, w_qkv, F32, name="gqa_kv", col0=nq)
    cos, sin = (jnp.asarray(t, F32) for t in _rope_tables())
    q_rot = rope_rows(q, N_PROMPT // 256, 0, nq, N_SAMPLE, cos, sin, BF16)
    k_rot = rope_rows(kv, N_PROMPT // 256, 0, nk, N_SAMPLE, cos, sin, BF16)
    o_p = gqa_attention_prompt(q, kv, sink)
    o_s = gqa_attention_sample(q_rot, k_rot, kv, cache_k.reshape(-1, nk), cache_v.reshape(-1, nk), sink)
    op = matmul_groups(o_p, o_s, w_o, F32, name="gqa_out")
    return op, kv


def fnet_mixer(u, w_out):
    f_p = fourier_real_2d(u, 0, BATCH, SEQ)
    f_s = fourier_real_2d(u, N_PROMPT // DEC_SEQ, DEC_BATCH, DEC_SEQ)
    return matmul_groups(f_p, f_s, w_out, F32, name="fnet_out")


def conv_mixer(u, w_in, conv_w, conv_b, w_out):
    h3 = matmul(u, w_in, BF16, name="conv_in")
    y_p = gated_conv(h3, conv_w, conv_b, 0, BATCH, SEQ)
    y_s = gated_conv(h3, conv_w, conv_b, N_PROMPT // DEC_SEQ, DEC_BATCH, DEC_SEQ)
    return matmul_groups(y_p, y_s, w_out, F32, name="conv_out")


def kernel(x_prompt, x_sample, cache_l0_ckv, cache_l0_krope, cache_l1_k, cache_l1_v, c, c_ctx, ada_w, ada_b, ln1_g, ln1_b, ln2_g, ln2_b, mla_w_dq, mla_q_norm, mla_w_uq, mla_w_dkv, mla_kv_norm, mla_w_uk, mla_w_uv, mla_w_o, gqa_w_qkv, gqa_sink, gqa_w_o, fnet_w_out, conv_w_in, conv_w, conv_b, conv_w_out, peer_w_q, peer_sub_keys, peer_u, peer_v):
    x = jnp.concatenate([x_prompt.reshape(N_PROMPT, D_MODEL), x_sample.reshape(N_SAMPLE, D_MODEL)], axis=0)
    cond = jnp.concatenate([c_ctx[None, :], c, jnp.zeros((N_COND - 1 - DEC_BATCH, D_MODEL), F32)], axis=0)
    mods_all = ada_modulation_all(cond, ada_w, ada_b).reshape(DEPTH, N_COND, 6, 1, D_MODEL)

    u = modulate_tokens(x, mods_all[0])
    new_ckv = new_krope = new_kv = None
    for i in range(DEPTH):
        mods = mods_all[i]
        if i == 0:
            delta, new_ckv, new_krope = mla_mixer(u, cache_l0_ckv, cache_l0_krope, mla_w_dq, mla_q_norm, mla_w_uq,
                                                  mla_w_dkv, mla_kv_norm, mla_w_uk, mla_w_uv, mla_w_o)
        elif i == 1:
            delta, new_kv = gqa_mixer(u, cache_l1_k, cache_l1_v, gqa_w_qkv, gqa_sink, gqa_w_o)
        elif i == 2:
            delta = fnet_mixer(u, fnet_w_out)
        else:
            delta = conv_mixer(u, conv_w_in, conv_w, conv_b, conv_w_out)
        x, u = post_norm(x, delta, mods, 2, ln1_g[i], ln1_b[i], mods, 3)
        delta = peer_ffn(u, peer_w_q, peer_sub_keys, peer_u, peer_v, i)
        if i + 1 < DEPTH:
            x, u = post_norm(x, delta, mods, 5, ln2_g[i], ln2_b[i], mods_all[i + 1], 0)
        else:
            y_prompt = post_norm(x, delta, mods, 5, ln2_g[i], ln2_b[i], rows=(0, N_PROMPT))
            y_sample = post_norm(x, delta, mods, 5, ln2_g[i], ln2_b[i], rows=(N_PROMPT, N_SAMPLE))

    nk = GQA_KV_HEADS * GQA_HEAD_DIM
    return (y_prompt.reshape(BATCH, SEQ, D_MODEL),
            y_sample.reshape(DEC_BATCH, DEC_SEQ, D_MODEL),
            new_ckv[:N_PROMPT].reshape(BATCH, SEQ, MLA_KV_LORA),
            new_krope[:N_PROMPT].reshape(BATCH, SEQ, MLA_ROPE),
            new_kv[:N_PROMPT, :nk].reshape(BATCH, SEQ, GQA_KV_HEADS, GQA_HEAD_DIM),
            new_kv[:N_PROMPT, nk:].reshape(BATCH, SEQ, GQA_KV_HEADS, GQA_HEAD_DIM))
```

```python
import functools
import math

import numpy as np
import jax
import jax.numpy as jnp
from jax import lax
from jax.experimental import pallas as pl
from jax.experimental.pallas import tpu as pltpu

F32 = jnp.float32
BF16 = jnp.bfloat16

D_MODEL = 2048
BATCH = 16
SEQ = 256
DEPTH = 4
DEC_BATCH = 2
DEC_SEQ = 1024
PAST_LEN = 256
GRID_W = 64
ROPE_BASE = 10000.0
LN_EPS = 1e-5
RMS_EPS = 1e-6
NEG_INF = -1e30
DEEPNORM_ALPHA = (2 * DEPTH) ** 0.25

MLA_HEADS = 16
MLA_Q_LORA = 512
MLA_KV_LORA = 512
MLA_NOPE = 128
MLA_ROPE = 64
MLA_V = 128
MLA_SCALE = (MLA_NOPE + MLA_ROPE) ** -0.5

GQA_HEADS = 32
GQA_KV_HEADS = 8
GQA_HEAD_DIM = 64
WINDOW = 128
GQA_SCALE = GQA_HEAD_DIM ** -0.5

FNET_GROUPS = 4
FNET_GROUP_DIM = D_MODEL // FNET_GROUPS

PEER_HEADS = 8
PEER_N_KEYS = 128
PEER_N_EXPERTS = PEER_N_KEYS * PEER_N_KEYS
PEER_HALF = 128
PEER_TOPK = 16

N_PROMPT = BATCH * SEQ
N_SAMPLE = DEC_BATCH * DEC_SEQ
N_TOK = N_PROMPT + N_SAMPLE
N_COND = 8

MIB = 1024 * 1024

NT_DIMS = (((1,), (1,)), ((), ()))
TN_DIMS = (((0,), (0,)), ((), ()))


def _params(sem, vmem_mib, flags=None):
    return pltpu.CompilerParams(dimension_semantics=sem, vmem_limit_bytes=vmem_mib * MIB, flags=flags)


def _group_of_tile(i, tm):
    return jnp.maximum((i * tm) // DEC_SEQ - (N_PROMPT // DEC_SEQ - 1), 0)


def _mod_spec(tm, which):
    return pl.BlockSpec((None, None, 1, D_MODEL), lambda i: (_group_of_tile(i, tm), which, 0, 0))


def _ada_kernel(c_ref, w_ref, b_ref, o_ref):
    c = c_ref[...]
    s = (c * jax.nn.sigmoid(c)).astype(BF16)
    o_ref[...] = jnp.dot(s, w_ref[...].astype(BF16), preferred_element_type=F32) + b_ref[...]


def ada_modulation_all(cond, ada_w, ada_b):
    tn = 1024
    n_out = 6 * D_MODEL
    return pl.pallas_call(
        _ada_kernel,
        out_shape=jax.ShapeDtypeStruct((DEPTH, N_COND, n_out), F32),
        grid=(DEPTH, n_out // tn),
        in_specs=[
            pl.BlockSpec((N_COND, D_MODEL), lambda l, j: (0, 0)),
            pl.BlockSpec((None, D_MODEL, tn), lambda l, j: (l, 0, j)),
            pl.BlockSpec((None, 1, tn), lambda l, j: (l, 0, j)),
        ],
        out_specs=pl.BlockSpec((None, N_COND, tn), lambda l, j: (l, 0, j)),
        compiler_params=_params(("parallel", "parallel"), 40),
        name="ada_modulation",
    )(cond, ada_w, ada_b.reshape(DEPTH, 1, n_out))


def _modulate_kernel(x_ref, sh_ref, sc_ref, u_ref):
    u_ref[...] = (x_ref[...] * (1.0 + sc_ref[...]) + sh_ref[...]).astype(u_ref.dtype)


def modulate_tokens(x, mods):
    tm = 256
    return pl.pallas_call(
        _modulate_kernel,
        out_shape=jax.ShapeDtypeStruct((N_TOK, D_MODEL), BF16),
        grid=(N_TOK // tm,),
        in_specs=[pl.BlockSpec((tm, D_MODEL), lambda i: (i, 0)), _mod_spec(tm, 0), _mod_spec(tm, 1)],
        out_specs=pl.BlockSpec((tm, D_MODEL), lambda i: (i, 0)),
        compiler_params=_params(("parallel",), 24),
        name="modulate",
    )(x, mods, mods)


def _layer_norm_rows(y, g, b):
    mu = jnp.mean(y, axis=-1, keepdims=True)
    yc = y - mu
    var = jnp.mean(yc * yc, axis=-1, keepdims=True)
    return yc * lax.rsqrt(var + LN_EPS) * g + b


def _post_norm_kernel(x_ref, d_ref, gate_ref, g_ref, b_ref, xo_ref):
    y = DEEPNORM_ALPHA * x_ref[...] + gate_ref[...] * d_ref[...].astype(F32)
    xo_ref[...] = _layer_norm_rows(y, g_ref[...], b_ref[...])


def _post_norm_mod_kernel(x_ref, d_ref, gate_ref, g_ref, b_ref, sh_ref, sc_ref, xo_ref, uo_ref):
    y = DEEPNORM_ALPHA * x_ref[...] + gate_ref[...] * d_ref[...].astype(F32)
    xn = _layer_norm_rows(y, g_ref[...], b_ref[...])
    xo_ref[...] = xn
    uo_ref[...] = (xn * (1.0 + sc_ref[...]) + sh_ref[...]).astype(uo_ref.dtype)


def post_norm(x, delta, mods, gate_idx, g, b, next_mods=None, next_idx=None, rows=(0, N_TOK)):
    tm = 256
    row = pl.BlockSpec((tm, D_MODEL), lambda i: (i, 0))
    vec = pl.BlockSpec((1, D_MODEL), lambda i: (0, 0))
    ins = [x, delta, mods, g.reshape(1, D_MODEL), b.reshape(1, D_MODEL)]
    specs = [row, row, _mod_spec(tm, gate_idx), vec, vec]
    if next_mods is None:
        i0 = rows[0] // tm
        row_in = pl.BlockSpec((tm, D_MODEL), lambda i: (i0 + i, 0))
        gate = pl.BlockSpec((None, None, 1, D_MODEL), lambda i: (_group_of_tile(i0 + i, tm), gate_idx, 0, 0))
        return pl.pallas_call(
            _post_norm_kernel,
            out_shape=jax.ShapeDtypeStruct((rows[1], D_MODEL), F32),
            grid=(rows[1] // tm,), in_specs=[row_in, row_in, gate, vec, vec], out_specs=row,
            compiler_params=_params(("parallel",), 32), name="post_norm",
        )(*ins)
    ins += [next_mods, next_mods]
    specs += [_mod_spec(tm, next_idx), _mod_spec(tm, next_idx + 1)]
    return pl.pallas_call(
        _post_norm_mod_kernel,
        out_shape=(jax.ShapeDtypeStruct((N_TOK, D_MODEL), F32), jax.ShapeDtypeStruct((N_TOK, D_MODEL), BF16)),
        grid=(N_TOK // tm,), in_specs=specs, out_specs=(row, row),
        compiler_params=_params(("parallel",), 32), name="post_norm_mod",
    )(*ins)


def _mm_kernel(x_ref, w_ref, o_ref):
    o_ref[...] = jnp.dot(x_ref[...], w_ref[...].astype(BF16), preferred_element_type=F32).astype(o_ref.dtype)


def _mm_resident_kernel(*refs, prompt_tiles):
    if prompt_tiles is None:
        x_ref, w_ref, o_ref, wb_ref = refs
    else:
        x_ref, xs_ref, w_ref, o_ref, wb_ref = refs
    i = pl.program_id(0)

    @pl.when(i == 0)
    def _():
        wb_ref[...] = w_ref[...].astype(BF16)

    if prompt_tiles is None:
        o_ref[...] = jnp.dot(x_ref[...], wb_ref[...], preferred_element_type=F32).astype(o_ref.dtype)
    else:
        @pl.when(i < prompt_tiles)
        def _():
            o_ref[...] = jnp.dot(x_ref[...], wb_ref[...], preferred_element_type=F32).astype(o_ref.dtype)

        @pl.when(i >= prompt_tiles)
        def _():
            o_ref[...] = jnp.dot(xs_ref[...], wb_ref[...], preferred_element_type=F32).astype(o_ref.dtype)


def matmul_resident(x, w, out_dtype, name, x_sample=None, layer=None, n=None):
    tm = 512
    k = x.shape[1]
    n = w.shape[-1] if n is None else n
    rows = x.shape[0] + (0 if x_sample is None else x_sample.shape[0])
    once = pl.Buffered(1)
    if layer is None:
        w_spec = pl.BlockSpec((k, n), lambda i: (0, 0), pipeline_mode=once)
    else:
        w_spec = pl.BlockSpec((None, k, n), lambda i: (layer, 0, 0), pipeline_mode=once)
    if x_sample is None:
        prompt_tiles = None
        xs, x_specs = (x,), [pl.BlockSpec((tm, k), lambda i: (i, 0))]
    else:
        prompt_tiles = x.shape[0] // tm
        xs = (x, x_sample)
        x_specs = [pl.BlockSpec((tm, k), lambda i: (jnp.minimum(i, prompt_tiles - 1), 0)),
                   pl.BlockSpec((tm, k), lambda i: (jnp.maximum(i - prompt_tiles, 0), 0))]
    return pl.pallas_call(
        functools.partial(_mm_resident_kernel, prompt_tiles=prompt_tiles),
        out_shape=jax.ShapeDtypeStruct((rows, n), out_dtype),
        grid=(rows // tm,),
        in_specs=x_specs + [w_spec],
        out_specs=pl.BlockSpec((tm, n), lambda i: (i, 0)),
        scratch_shapes=[pltpu.VMEM((k, n), BF16)],
        compiler_params=_params(("arbitrary",), 48),
        name=name,
    )(*xs, w)


def matmul(x, w, out_dtype, tm=2048, tn=512, name="matmul", layer=None, col0=0, n=None):
    m, k = x.shape
    n = w.shape[-1] - col0 if n is None else n
    tm = min(tm, m)
    tn = min(tn, n)
    assert m % tm == 0 and n % tn == 0 and col0 % tn == 0, (m, n, tm, tn, col0)
    j0 = col0 // tn
    if layer is None:
        w_spec = pl.BlockSpec((k, tn), lambda i, j: (0, j0 + j))
    else:
        w_spec = pl.BlockSpec((None, k, tn), lambda i, j: (layer, 0, j0 + j))
    return pl.pallas_call(
        _mm_kernel,
        out_shape=jax.ShapeDtypeStruct((m, n), out_dtype),
        grid=(m // tm, n // tn),
        in_specs=[pl.BlockSpec((tm, k), lambda i, j: (i, 0)), w_spec],
        out_specs=pl.BlockSpec((tm, tn), lambda i, j: (i, j)),
        compiler_params=_params(("parallel", "parallel"), 48),
        name=name,
    )(x, w)


def _rms_rows(y, g):
    return y * lax.rsqrt(jnp.mean(y * y, axis=-1, keepdims=True) + RMS_EPS) * g


def _mla_down_kernel(x_ref, w_ref, qn_ref, kvn_ref, cq_ref, ckv_ref, kr_ref):
    y = jnp.dot(x_ref[...], w_ref[...].astype(BF16), preferred_element_type=F32)
    cq_ref[...] = _rms_rows(y[:, :MLA_Q_LORA], qn_ref[...]).astype(cq_ref.dtype)
    ckv_ref[...] = _rms_rows(y[:, MLA_Q_LORA:MLA_Q_LORA + MLA_KV_LORA], kvn_ref[...])
    kr_ref[...] = y[:, MLA_Q_LORA + MLA_KV_LORA:MLA_Q_LORA + MLA_KV_LORA + MLA_ROPE]


def mla_down(u, w_cat, q_norm, kv_norm):
    tm = 512
    n = w_cat.shape[1]
    return pl.pallas_call(
        _mla_down_kernel,
        out_shape=(jax.ShapeDtypeStruct((N_TOK, MLA_Q_LORA), BF16),
                   jax.ShapeDtypeStruct((N_TOK, MLA_KV_LORA), F32),
                   jax.ShapeDtypeStruct((N_TOK, MLA_ROPE), F32)),
        grid=(N_TOK // tm,),
        in_specs=[pl.BlockSpec((tm, D_MODEL), lambda i: (i, 0)),
                  pl.BlockSpec((D_MODEL, n), lambda i: (0, 0)),
                  pl.BlockSpec((1, MLA_Q_LORA), lambda i: (0, 0)),
                  pl.BlockSpec((1, MLA_KV_LORA), lambda i: (0, 0))],
        out_specs=(pl.BlockSpec((tm, MLA_Q_LORA), lambda i: (i, 0)),
                   pl.BlockSpec((tm, MLA_KV_LORA), lambda i: (i, 0)),
                   pl.BlockSpec((tm, MLA_ROPE), lambda i: (i, 0))),
        compiler_params=_params(("parallel",), 48),
        name="mla_down",
    )(u, w_cat, q_norm.reshape(1, -1), kv_norm.reshape(1, -1))


def _rope_tables():
    t = np.arange(DEC_SEQ)
    quarter = MLA_ROPE // 4
    inv_freq = ROPE_BASE ** (-np.arange(quarter, dtype=np.float64) / quarter)
    ang_row = (t // GRID_W)[:, None] * inv_freq[None, :]
    ang_col = (t % GRID_W)[:, None] * inv_freq[None, :]
    cos = np.concatenate([np.cos(ang_row)] * 2 + [np.cos(ang_col)] * 2, -1)
    sin = np.concatenate([-np.sin(ang_row), np.sin(ang_row), -np.sin(ang_col), np.sin(ang_col)], -1)
    return np.tile(cos, (1, 2)), np.tile(sin, (1, 2))


def _rope_kernel(x_ref, c_ref, s_ref, o_ref):
    x = x_ref[...].astype(F32)
    w = x.shape[-1]
    reps = w // c_ref.shape[-1]
    cos = jnp.tile(c_ref[...], (1, reps))
    sin = jnp.tile(s_ref[...], (1, reps))
    lane = lax.broadcasted_iota(jnp.int32, x.shape, 1)
    partner = jnp.where((lane % 32) < 16, pltpu.roll(x, w - 16, 1), pltpu.roll(x, 16, 1))
    o_ref[...] = (x * cos + partner * sin).astype(o_ref.dtype)


def rope_rows(x, row_block0, col_block, width, n_rows, cos, sin, out_dtype, tr=256):
    t_blocks = cos.shape[0] // tr
    return pl.pallas_call(
        _rope_kernel,
        out_shape=jax.ShapeDtypeStruct((n_rows, width), out_dtype),
        grid=(n_rows // tr,),
        in_specs=[pl.BlockSpec((tr, width), lambda i: (row_block0 + i, col_block)),
                  pl.BlockSpec((tr, 128), lambda i: (i % t_blocks, 0)),
                  pl.BlockSpec((tr, 128), lambda i: (i % t_blocks, 0))],
        out_specs=pl.BlockSpec((tr, width), lambda i: (i, 0)),
        compiler_params=_params(("parallel",), 32),
        name="axial_rope",
    )(x, cos, sin)


def _softmax_parts(parts, sink):
    m = parts[0].max(axis=-1, keepdims=True)
    for s in parts[1:]:
        m = jnp.maximum(m, s.max(axis=-1, keepdims=True))
    if sink is not None:
        m = jnp.maximum(m, sink)
    ps = [jnp.exp(s - m) for s in parts]
    l = ps[0].sum(axis=-1, keepdims=True)
    for p in ps[1:]:
        l = l + p.sum(axis=-1, keepdims=True)
    if sink is not None:
        l = l + jnp.exp(sink - m)
    return ps, 1.0 / l


def _mla_attn_kernel(*refs, with_ctx):
    if with_ctx:
        qn_ref, qr_ref, kn_ref, v_ref, kr_ref, kn2_ref, v2_ref, kr2_ref, o_ref = refs
    else:
        qn_ref, qr_ref, kn_ref, v_ref, kr_ref, o_ref = refs
    kr = kr_ref[...].astype(BF16)
    kr2 = kr2_ref[...].astype(BF16) if with_ctx else None
    for h in range(MLA_HEADS):
        n0, n1 = h * MLA_NOPE, (h + 1) * MLA_NOPE
        qn = qn_ref[:, n0:n1]
        qr = qr_ref[:, h * MLA_ROPE:(h + 1) * MLA_ROPE]
        s = (lax.dot_general(qn, kn_ref[:, n0:n1], NT_DIMS, preferred_element_type=F32)
             + lax.dot_general(qr, kr, NT_DIMS, preferred_element_type=F32)) * MLA_SCALE
        parts = [s]
        if with_ctx:
            s2 = (lax.dot_general(qn, kn2_ref[:, n0:n1], NT_DIMS, preferred_element_type=F32)
                  + lax.dot_general(qr, kr2, NT_DIMS, preferred_element_type=F32)) * MLA_SCALE
            parts.append(s2)
        ps, inv_l = _softmax_parts(parts, None)
        o = jnp.dot(ps[0].astype(BF16), v_ref[:, n0:n1], preferred_element_type=F32)
        if with_ctx:
            o = o + jnp.dot(ps[1].astype(BF16), v2_ref[:, n0:n1], preferred_element_type=F32)
        o_ref[:, n0:n1] = (o * inv_l).astype(o_ref.dtype)


def mla_attention_prompt(q, kv_up, kr):
    w = MLA_HEADS * MLA_NOPE
    return pl.pallas_call(
        functools.partial(_mla_attn_kernel, with_ctx=False),
        out_shape=jax.ShapeDtypeStruct((N_PROMPT, w), BF16),
        grid=(BATCH,),
        in_specs=[pl.BlockSpec((SEQ, w), lambda b: (b, 0)),
                  pl.BlockSpec((SEQ, MLA_HEADS * MLA_ROPE), lambda b: (b, w // (MLA_HEADS * MLA_ROPE))),
                  pl.BlockSpec((SEQ, w), lambda b: (b, 0)),
                  pl.BlockSpec((SEQ, w), lambda b: (b, 1)),
                  pl.BlockSpec((SEQ, MLA_ROPE), lambda b: (b, 0))],
        out_specs=pl.BlockSpec((SEQ, w), lambda b: (b, 0)),
        compiler_params=_params(("parallel",), 32),
        name="mla_attention_prompt",
    )(q, q, kv_up, kv_up, kr)


def mla_attention_sample(qn, qr_rot, kv_up, kr_rot, kr_cache):
    w = MLA_HEADS * MLA_NOPE
    tq = 256
    nq = DEC_SEQ // tq
    q0 = N_PROMPT // tq
    lat0 = N_PROMPT // DEC_SEQ
    ctx0 = N_TOK // PAST_LEN
    return pl.pallas_call(
        functools.partial(_mla_attn_kernel, with_ctx=True),
        out_shape=jax.ShapeDtypeStruct((N_SAMPLE, w), BF16),
        grid=(DEC_BATCH, nq),
        in_specs=[pl.BlockSpec((tq, w), lambda b, i: (q0 + b * nq + i, 0)),
                  pl.BlockSpec((tq, MLA_HEADS * MLA_ROPE), lambda b, i: (b * nq + i, 0)),
                  pl.BlockSpec((DEC_SEQ, w), lambda b, i: (lat0 + b, 0)),
                  pl.BlockSpec((DEC_SEQ, w), lambda b, i: (lat0 + b, 1)),
                  pl.BlockSpec((DEC_SEQ, MLA_ROPE), lambda b, i: (b, 0)),
                  pl.BlockSpec((PAST_LEN, w), lambda b, i: (ctx0 + b, 0)),
                  pl.BlockSpec((PAST_LEN, w), lambda b, i: (ctx0 + b, 1)),
                  pl.BlockSpec((PAST_LEN, MLA_ROPE), lambda b, i: (b, 0))],
        out_specs=pl.BlockSpec((tq, w), lambda b, i: (b * nq + i, 0)),
        compiler_params=_params(("parallel", "parallel"), 56),
        name="mla_attention_sample",
    )(qn, qr_rot, kv_up, kv_up, kr_rot, kv_up, kv_up, kr_cache)


def _gqa_attn_kernel(*refs, with_ctx, tq):
    if with_ctx:
        sink_ref, q_ref, k_ref, v_ref, k2_ref, v2_ref, o_ref = refs
    else:
        sink_ref, q_ref, k_ref, v_ref, o_ref = refs
    group = GQA_HEADS // GQA_KV_HEADS
    d = GQA_HEAD_DIM
    if with_ctx:
        t = pl.program_id(1) * tq + lax.broadcasted_iota(jnp.int32, (tq, DEC_SEQ), 0)
        s_pos = lax.broadcasted_iota(jnp.int32, (tq, DEC_SEQ), 1)
        in_window = jnp.abs(t - s_pos) <= WINDOW
    for hk in range(GQA_KV_HEADS):
        k = k_ref[:, hk * d:(hk + 1) * d].astype(BF16)
        v = v_ref[:, hk * d:(hk + 1) * d].astype(BF16)
        if with_ctx:
            k2 = k2_ref[:, hk * d:(hk + 1) * d].astype(BF16)
            v2 = v2_ref[:, hk * d:(hk + 1) * d].astype(BF16)
        for g in range(group):
            h = hk * group + g
            q = q_ref[:, h * d:(h + 1) * d]
            s = lax.dot_general(q, k, NT_DIMS, preferred_element_type=F32) * GQA_SCALE
            parts = [s]
            if with_ctx:
                parts = [jnp.where(in_window, s, NEG_INF),
                         lax.dot_general(q, k2, NT_DIMS, preferred_element_type=F32) * GQA_SCALE]
            ps, inv_l = _softmax_parts(parts, sink_ref[h])
            o = jnp.dot(ps[0].astype(BF16), v, preferred_element_type=F32)
            if with_ctx:
                o = o + jnp.dot(ps[1].astype(BF16), v2, preferred_element_type=F32)
            o_ref[:, h * d:(h + 1) * d] = (o * inv_l).astype(o_ref.dtype)


def gqa_attention_prompt(q, kv, sink):
    wq = GQA_HEADS * GQA_HEAD_DIM
    wk = GQA_KV_HEADS * GQA_HEAD_DIM
    return pl.pallas_call(
        functools.partial(_gqa_attn_kernel, with_ctx=False, tq=SEQ),
        out_shape=jax.ShapeDtypeStruct((N_PROMPT, wq), BF16),
        grid=(BATCH,),
        in_specs=[pl.BlockSpec(memory_space=pltpu.SMEM),
                  pl.BlockSpec((SEQ, wq), lambda b: (b, 0)),
                  pl.BlockSpec((SEQ, wk), lambda b: (b, 0)),
                  pl.BlockSpec((SEQ, wk), lambda b: (b, 1))],
        out_specs=pl.BlockSpec((SEQ, wq), lambda b: (b, 0)),
        compiler_params=_params(("parallel",), 32),
        name="gqa_attention_prompt",
    )(sink, q, kv, kv)


def gqa_attention_sample(q_rot, k_rot, kv, k_cache, v_cache, sink):
    wq = GQA_HEADS * GQA_HEAD_DIM
    wk = GQA_KV_HEADS * GQA_HEAD_DIM
    tq = 256
    nq = DEC_SEQ // tq
    lat0 = N_PROMPT // DEC_SEQ
    return pl.pallas_call(
        functools.partial(_gqa_attn_kernel, with_ctx=True, tq=tq),
        out_shape=jax.ShapeDtypeStruct((N_SAMPLE, wq), BF16),
        grid=(DEC_BATCH, nq),
        in_specs=[pl.BlockSpec(memory_space=pltpu.SMEM),
                  pl.BlockSpec((tq, wq), lambda b, i: (b * nq + i, 0)),
                  pl.BlockSpec((DEC_SEQ, wk), lambda b, i: (b, 0)),
                  pl.BlockSpec((DEC_SEQ, wk), lambda b, i: (lat0 + b, 1)),
                  pl.BlockSpec((PAST_LEN, wk), lambda b, i: (b, 0)),
                  pl.BlockSpec((PAST_LEN, wk), lambda b, i: (b, 0))],
        out_specs=pl.BlockSpec((tq, wq), lambda b, i: (b * nq + i, 0)),
        compiler_params=_params(("parallel", "parallel"), 56),
        name="gqa_attention_sample",
    )(sink, q_rot, k_rot, kv, k_cache, v_cache)


def _dft_tables(n):
    jk = (np.arange(n)[:, None] * np.arange(n)[None, :]) % n
    ang = 2.0 * np.pi * jk / n
    return np.cos(ang) / math.sqrt(n), np.sin(ang) / math.sqrt(n)


def _fnet_kernel(x_ref, ct_ref, st_ref, cc_ref, sc_ref, o_ref):
    x = x_ref[...]
    y1 = jnp.dot(x, cc_ref[...], preferred_element_type=F32).astype(BF16)
    y2 = jnp.dot(x, sc_ref[...], preferred_element_type=F32).astype(BF16)
    o = (jnp.dot(ct_ref[...], y1, preferred_element_type=F32)
         - jnp.dot(st_ref[...], y2, preferred_element_type=F32))
    o_ref[...] = o.astype(o_ref.dtype)


def fourier_real_2d(u, seq_block0, n_seq, t):
    ct, st = (jnp.asarray(a, BF16) for a in _dft_tables(t))
    cc, sc = (jnp.asarray(a, BF16) for a in _dft_tables(FNET_GROUP_DIM))
    gd = FNET_GROUP_DIM
    return pl.pallas_call(
        _fnet_kernel,
        out_shape=jax.ShapeDtypeStruct((n_seq * t, D_MODEL), BF16),
        grid=(n_seq, FNET_GROUPS),
        in_specs=[pl.BlockSpec((t, gd), lambda b, g: (seq_block0 + b, g)),
                  pl.BlockSpec((t, t), lambda b, g: (0, 0)),
                  pl.BlockSpec((t, t), lambda b, g: (0, 0)),
                  pl.BlockSpec((gd, gd), lambda b, g: (0, 0)),
                  pl.BlockSpec((gd, gd), lambda b, g: (0, 0))],
        out_specs=pl.BlockSpec((t, gd), lambda b, g: (b, g)),
        compiler_params=_params(("parallel", "parallel"), 32),
        name="fourier_mix",
    )(u, ct, st, cc, sc)


def _conv_kernel(b_ref, c_ref, h_ref, w_ref, cb_ref, o_ref):
    z = c_ref[...].astype(F32) * h_ref[...].astype(F32)
    t = z.shape[0]
    row = lax.broadcasted_iota(jnp.int32, z.shape, 0)
    z_prev = jnp.where(row == 0, 0.0, pltpu.roll(z, 1, 0))
    z_next = jnp.where(row == t - 1, 0.0, pltpu.roll(z, t - 1, 0))
    conv = z_prev * w_ref[0:1, :] + z * w_ref[1:2, :] + z_next * w_ref[2:3, :] + cb_ref[...]
    o_ref[...] = (b_ref[...].astype(F32) * conv).astype(o_ref.dtype)


def gated_conv(h3, conv_w, conv_b, seq_block0, n_seq, t):
    tn = 512
    nj = D_MODEL // tn
    return pl.pallas_call(
        _conv_kernel,
        out_shape=jax.ShapeDtypeStruct((n_seq * t, D_MODEL), BF16),
        grid=(n_seq, nj),
        in_specs=[pl.BlockSpec((t, tn), lambda s, j: (seq_block0 + s, j)),
                  pl.BlockSpec((t, tn), lambda s, j: (seq_block0 + s, nj + j)),
                  pl.BlockSpec((t, tn), lambda s, j: (seq_block0 + s, 2 * nj + j)),
                  pl.BlockSpec((3, tn), lambda s, j: (0, j)),
                  pl.BlockSpec((1, tn), lambda s, j: (0, j))],
        out_specs=pl.BlockSpec((t, tn), lambda s, j: (s, j)),
        compiler_params=_params(("parallel", "parallel"), 32),
        name="gated_conv",
    )(h3, h3, h3, conv_w, conv_b.reshape(1, D_MODEL))


_N_RANKS = PEER_TOPK + 1
_CAND_PAIRS = [(a, b) for a in range(_N_RANKS) for b in range(_N_RANKS) if (a + 1) * (b + 1) <= _N_RANKS]


_NO_RANK = 127.0


def _top_values_exact(s, k, want_rank):
    n = s.shape[0]
    idx = lax.broadcasted_iota(jnp.int32, s.shape, 0)
    rank = jnp.full(s.shape, _NO_RANK, F32) if want_rank else None
    out = []
    for it in range(k):
        m = jnp.max(s, axis=0, keepdims=True)
        out.append(m)
        first = jnp.min(jnp.where(s == m, idx, n), axis=0, keepdims=True)
        hit = idx == first
        s = jnp.where(hit, -jnp.inf, s)
        if want_rank:
            rank = jnp.where(hit, float(it), rank)
    return out, rank


def _all_equal(x, value):
    return jnp.min(jnp.where(x == value, 1.0, 0.0)) > 0.5


def _top_values_distinct(s, k, n_pad):
    out = []
    for it in range(k):
        m = jnp.max(s, axis=0, keepdims=True)
        out.append(m)
        s = jnp.where(s == m, -jnp.inf, s)
    removed = jnp.sum(jnp.where(s == -jnp.inf, 1.0, 0.0), axis=0, keepdims=True)
    return out, _all_equal(removed, float(k + n_pad))


def _batcher_network(n):
    def merge(lo, hi, r):
        step = r * 2
        if step < hi - lo:
            yield from merge(lo, hi, step)
            yield from merge(lo + r, hi, step)
            yield from [(i, i + r) for i in range(lo + r, hi - r, step)]
        else:
            yield (lo, lo + r)

    def sort(lo, hi):
        if hi - lo >= 1:
            mid = lo + (hi - lo) // 2
            yield from sort(lo, mid)
            yield from sort(mid + 1, hi)
            yield from merge(lo, hi, 1)

    return list(sort(0, n - 1))


_SUBLANES = 8
_KEY_TILES = PEER_N_KEYS // _SUBLANES
_SORT_KEY_TILES = _batcher_network(_KEY_TILES)


def _top_values_sorted(s, k):
    tiles = [s[j * _SUBLANES:(j + 1) * _SUBLANES, :] for j in range(_KEY_TILES)]
    for i, j in _SORT_KEY_TILES:
        tiles[i], tiles[j] = jnp.maximum(tiles[i], tiles[j]), jnp.minimum(tiles[i], tiles[j])
    out = []
    pops = jnp.zeros_like(tiles[0])
    for it in range(k):
        m = jnp.max(tiles[0], axis=0, keepdims=True)
        out.append(m)
        hit = tiles[0] == m
        pops = pops + jnp.where(hit, 1.0, 0.0)
        depth = k - 1 - it
        for j in range(min(depth, _KEY_TILES - 1)):
            tiles[j] = jnp.where(hit, tiles[j + 1], tiles[j])
        if depth >= _KEY_TILES - 1:
            tiles[-1] = jnp.where(hit, -jnp.inf, tiles[-1])
    return out, _all_equal(jnp.sum(pops, axis=0, keepdims=True), float(k))


def _ranks_from_values(s, values):
    rank = jnp.full(s.shape, _NO_RANK, F32)
    for r in reversed(range(len(values))):
        rank = jnp.where(s >= values[r], float(r), rank)
    return rank


def _dup_bf16_words(x):
    bits = pltpu.bitcast(x.astype(BF16).astype(F32), jnp.int32)
    return bits | lax.shift_right_logical(bits, jnp.int32(16))


def _route_head(h, s1, s2, fast, bq_ref, r2_ref, aw_ref, cw_ref):
    tm = s1.shape[1]
    if fast:
        v1, ok1 = _top_values_sorted(s1, _N_RANKS)
        v2, ok2 = _top_values_sorted(s2, _N_RANKS)
        rank2 = _ranks_from_values(s2, v2)
    else:
        v1, _ = _top_values_exact(s1, _N_RANKS, False)
        v2, rank2 = _top_values_exact(s2, _N_RANKS, True)
    n_pad = (-len(_CAND_PAIRS)) % 8
    n_cand = len(_CAND_PAIRS) + n_pad
    cand_row = lax.broadcasted_iota(jnp.int32, (n_cand, tm), 0)
    cand = jnp.full((n_cand, tm), -jnp.inf, F32)
    for r, (a, b) in enumerate(_CAND_PAIRS):
        cand = jnp.where(cand_row == r, v1[a] + v2[b], cand)
    if fast:
        top, ok3 = _top_values_distinct(cand, _N_RANKS, n_pad)
    else:
        top, _ = _top_values_exact(cand, _N_RANKS, False)
    tau = 0.5 * (top[PEER_TOPK - 1] + top[PEER_TOPK])
    z = jnp.exp(top[0] - top[0])
    for kk in range(1, PEER_TOPK):
        z = z + jnp.exp(top[kk] - top[0])
    inv_z = 1.0 / z
    count = jnp.zeros_like(s1)
    for b in range(PEER_TOPK):
        count = jnp.where(s1 >= tau - v2[b], float(b + 1), count)
    rows = slice(h * PEER_N_KEYS, (h + 1) * PEER_N_KEYS)
    bq_ref[rows, :] = (jnp.exp(s2 - v2[0]) * inv_z).astype(BF16)
    r2_ref[rows, :] = rank2.astype(BF16)
    a_half = 0.5 * jnp.exp(s1 - v1[0])
    aw_ref[:, h, :, :] = _dup_bf16_words(a_half).reshape(PEER_N_KEYS // 8, 8, tm)
    cw_ref[:, h, :, :] = _dup_bf16_words(count).reshape(PEER_N_KEYS // 8, 8, tm)
    return jnp.logical_and(jnp.logical_and(ok1, ok2), ok3) if fast else None


def _route_kernel(q_ref, sk_ref, bq_ref, r2_ref, aw_ref, cw_ref):
    outs = (bq_ref, r2_ref, aw_ref, cw_ref)
    for h in range(PEER_HEADS):
        s = []
        for c in range(2):
            hc = 2 * h + c
            s.append(lax.dot_general(sk_ref[hc].astype(BF16), q_ref[:, hc * PEER_HALF:(hc + 1) * PEER_HALF],
                                     NT_DIMS, preferred_element_type=F32))
        no_ties = _route_head(h, s[0], s[1], True, *outs)

        @pl.when(jnp.logical_not(no_ties))
        def _():
            _route_head(h, s[0], s[1], False, *outs)


def peer_route(q, sub_keys, layer):
    tm = 256
    n_tok = q.shape[0]
    sk = sub_keys.reshape(-1, PEER_HEADS * 2, PEER_N_KEYS, PEER_HALF)
    tiles = PEER_N_KEYS // 8
    col_out = jax.ShapeDtypeStruct((PEER_HEADS * PEER_N_KEYS, n_tok), BF16)
    col_spec = pl.BlockSpec((PEER_HEADS * PEER_N_KEYS, tm), lambda i: (0, i))
    row_out = jax.ShapeDtypeStruct((tiles, PEER_HEADS, 8, n_tok), jnp.int32)
    row_spec = pl.BlockSpec((tiles, PEER_HEADS, 8, tm), lambda i: (0, 0, 0, i))
    return pl.pallas_call(
        _route_kernel,
        out_shape=(col_out, col_out, row_out, row_out),
        grid=(n_tok // tm,),
        in_specs=[pl.BlockSpec((tm, PEER_HEADS * 2 * PEER_HALF), lambda i: (i, 0)),
                  pl.BlockSpec((None, PEER_HEADS * 2, PEER_N_KEYS, PEER_HALF), lambda i: (layer, 0, 0, 0))],
        out_specs=(col_spec, col_spec, row_spec, row_spec),
        compiler_params=_params(("parallel",), 32),
        name="peer_route",
    )(q, sk)


PEER_TE = 512
PEER_TM = 1024
PEER_CHUNK = 512
PEER_N_TILES = PEER_N_EXPERTS // PEER_TE


def _bf16_rows(word_row, n_rows):
    return pltpu.bitcast(jnp.broadcast_to(word_row, (n_rows // 2, word_row.shape[1])), BF16)


PEER_LANES = 256
PEER_ROWS = 128


def _peer_gated_slab(k, k0, bq_ref, r2_ref, aw_ref, cw_ref, ht_ref, at_ref):
    for j in range(ht_ref.shape[1] // PEER_LANES):
        lanes = slice(j * PEER_LANES, (j + 1) * PEER_LANES)
        for r in range(PEER_N_KEYS // PEER_ROWS):
            g = [None, None]
            for h in range(PEER_HEADS):
                count = _bf16_rows(cw_ref[0, h, k0 + k:k0 + k + 1, lanes], PEER_ROWS)
                a_half = _bf16_rows(aw_ref[0, h, k0 + k:k0 + k + 1, lanes], PEER_ROWS)
                rows = slice(h * PEER_N_KEYS + r * PEER_ROWS, h * PEER_N_KEYS + (r + 1) * PEER_ROWS)
                bq = bq_ref[rows, lanes]
                contrib = jnp.where(r2_ref[rows, lanes] < count, bq, jnp.zeros_like(bq)) * a_half
                g[h % 2] = contrib if g[h % 2] is None else g[h % 2] + contrib
            krows = slice(k * PEER_N_KEYS + r * PEER_ROWS, k * PEER_N_KEYS + (r + 1) * PEER_ROWS)
            hk = ht_ref[krows, lanes]
            gelu2 = hk * (1.0 + lax.erf(hk * math.sqrt(0.5)))
            at_ref[krows, lanes] = gelu2.astype(BF16) * (g[0] + g[1])


def _peer_step(x_ref, bq_ref, r2_ref, aw_ref, cw_ref, u_ref, v_ref, o_ref, at_ref, ht_write, ht_read, k0):
    keys_per_tile = PEER_TE // PEER_N_KEYS
    keys_per_chunk = PEER_CHUNK // PEER_N_KEYS
    tm = x_ref.shape[0]
    tok = tm // keys_per_tile
    u = u_ref[...].astype(BF16) if ht_write is not None else None
    for k in range(keys_per_tile):
        if ht_read is not None:
            _peer_gated_slab(k, k0, bq_ref, r2_ref, aw_ref, cw_ref, ht_read, at_ref)
        if ht_write is not None:
            ht_write[:, k * tok:(k + 1) * tok] = lax.dot_general(
                u, x_ref[k * tok:(k + 1) * tok, :], NT_DIMS, preferred_element_type=F32)
        if ht_read is not None and (k + 1) % keys_per_chunk == 0:
            chunk = slice((k + 1) * PEER_N_KEYS - PEER_CHUNK, (k + 1) * PEER_N_KEYS)
            o_ref[...] += lax.dot_general(at_ref[chunk, :], v_ref[chunk, :].astype(BF16), TN_DIMS,
                                          preferred_element_type=F32)


def _peer_kernel(x_ref, bq_ref, r2_ref, aw_ref, cw_ref, u_ref, v_ref, o_ref, ht0_ref, ht1_ref, at_ref):
    e = pl.program_id(1)
    last = PEER_N_TILES
    keys_per_tile = PEER_TE // PEER_N_KEYS
    assert 8 // keys_per_tile == 2
    step = functools.partial(_peer_step, x_ref, bq_ref, r2_ref, aw_ref, cw_ref, u_ref, v_ref, o_ref, at_ref)
    even = e % 2 == 0

    @pl.when(e == 0)
    def _():
        o_ref[...] = jnp.zeros_like(o_ref)
        step(ht0_ref, None, None)

    @pl.when(jnp.logical_not(even))
    def _():
        step(ht1_ref, ht0_ref, 0)

    @pl.when(even & (e > 0) & (e < last))
    def _():
        step(ht0_ref, ht1_ref, keys_per_tile)

    @pl.when(e == last)
    def _():
        step(None, ht1_ref, keys_per_tile)


def peer_experts(u, bq, r2, aw, cw, exp_u, exp_v, layer):
    tm, te = PEER_TM, PEER_TE
    n_tok = u.shape[0]
    assert PEER_N_TILES % 2 == 0 and n_tok % tm == 0
    tiles_per_row_block = 8 * PEER_N_KEYS // te
    once = pl.Buffered(1)
    col_spec = pl.BlockSpec((PEER_HEADS * PEER_N_KEYS, tm), lambda i, e: (0, i), pipeline_mode=once)
    row_spec = pl.BlockSpec((1, PEER_HEADS, 8, tm),
                            lambda i, e: (jnp.maximum(e - 1, 0) // tiles_per_row_block, 0, 0, i))
    return pl.pallas_call(
        _peer_kernel,
        out_shape=jax.ShapeDtypeStruct((n_tok, D_MODEL), F32),
        grid=(n_tok // tm, PEER_N_TILES + 1),
        in_specs=[pl.BlockSpec((tm, D_MODEL), lambda i, e: (i, 0), pipeline_mode=once),
                  col_spec, col_spec,
                  row_spec, row_spec,
                  pl.BlockSpec((None, te, D_MODEL), lambda i, e: (layer, jnp.minimum(e, PEER_N_TILES - 1), 0)),
                  pl.BlockSpec((None, te, D_MODEL), lambda i, e: (layer, jnp.maximum(e - 1, 0), 0))],
        out_specs=pl.BlockSpec((tm, D_MODEL), lambda i, e: (i, 0), pipeline_mode=once),
        scratch_shapes=[pltpu.VMEM((te, tm), F32), pltpu.VMEM((te, tm), F32), pltpu.VMEM((te, tm), BF16)],
        compiler_params=_params(("parallel", "arbitrary"), 56),
        name="peer_experts",
    )(u, bq, r2, aw, cw, exp_u, exp_v)


def peer_ffn(u, w_q, sub_keys, exp_u, exp_v, layer):
    q = matmul_resident(u, w_q, BF16, "peer_query", layer=layer)
    bq, r2, aw, cw = peer_route(q, sub_keys, layer)
    return peer_experts(u, bq, r2, aw, cw, exp_u, exp_v, layer)


def mla_mixer(u, cache_ckv, cache_krope, w_dq, q_norm, w_uq, w_dkv, kv_norm, w_uk, w_uv, w_o):
    pad = (-(MLA_Q_LORA + MLA_KV_LORA + MLA_ROPE)) % 128
    w_cat = jnp.concatenate([w_dq, w_dkv, jnp.zeros((D_MODEL, pad), F32)], axis=1)
    cq, ckv, kr = mla_down(u, w_cat, q_norm, kv_norm)
    w_uq3 = w_uq.reshape(MLA_Q_LORA, MLA_HEADS, MLA_NOPE + MLA_ROPE)
    w_q_cat = jnp.concatenate([w_uq3[:, :, :MLA_NOPE].reshape(MLA_Q_LORA, -1),
                               w_uq3[:, :, MLA_NOPE:].reshape(MLA_Q_LORA, -1)], axis=1)
    q = matmul(cq, w_q_cat, BF16, name="mla_q_up")
    w_nope = MLA_HEADS * MLA_NOPE
    ckv_src = jnp.concatenate([ckv, cache_ckv.reshape(-1, MLA_KV_LORA)], axis=0).astype(BF16)
    kv_up = matmul(ckv_src, jnp.concatenate([w_uk, w_uv], axis=1), BF16, tm=512, tn=2048, name="mla_kv_up")
    cos, sin = (jnp.asarray(t, F32) for t in _rope_tables())
    w_rope = MLA_HEADS * MLA_ROPE
    qr_rot = rope_rows(q, N_PROMPT // 256, w_nope // w_rope, w_rope, N_SAMPLE, cos, sin, BF16)
    cos2, sin2 = (t.reshape(DEC_SEQ // 2, 128) for t in (cos[:, :MLA_ROPE], sin[:, :MLA_ROPE]))
    kr_pairs = kr[N_PROMPT:].reshape(N_SAMPLE // 2, 2 * MLA_ROPE)
    kr_rot = rope_rows(kr_pairs, 0, 0, 2 * MLA_ROPE, N_SAMPLE // 2, cos2, sin2, BF16).reshape(N_SAMPLE, MLA_ROPE)
    o_p = mla_attention_prompt(q, kv_up, kr)
    o_s = mla_attention_sample(q, qr_rot, kv_up, kr_rot, cache_krope.reshape(-1, MLA_ROPE))
    op = matmul_resident(o_p, w_o, F32, "mla_out", x_sample=o_s)
    return op, ckv, kr


def gqa_mixer(u, cache_k, cache_v, w_qkv, sink, w_o):
    nq = GQA_HEADS * GQA_HEAD_DIM
    nk = GQA_KV_HEADS * GQA_HEAD_DIM
    q = matmul_resident(u, w_qkv, BF16, "gqa_q", n=nq)
    kv = matmul(u, w_qkv, F32, name="gqa_kv", col0=nq)
    cos, sin = (jnp.asarray(t, F32) for t in _rope_tables())
    q_rot = rope_rows(q, N_PROMPT // 256, 0, nq, N_SAMPLE, cos, sin, BF16)
    k_rot = rope_rows(kv, N_PROMPT // 256, 0, nk, N_SAMPLE, cos, sin, BF16)
    o_p = gqa_attention_prompt(q, kv, sink)
    o_s = gqa_attention_sample(q_rot, k_rot, kv, cache_k.reshape(-1, nk), cache_v.reshape(-1, nk), sink)
    op = matmul_resident(o_p, w_o, F32, "gqa_out", x_sample=o_s)
    return op, kv


def fnet_mixer(u, w_out):
    f_p = fourier_real_2d(u, 0, BATCH, SEQ)
    f_s = fourier_real_2d(u, N_PROMPT // DEC_SEQ, DEC_BATCH, DEC_SEQ)
    return matmul_resident(f_p, w_out, F32, "fnet_out", x_sample=f_s)


def conv_mixer(u, w_in, conv_w, conv_b, w_out):
    h3 = matmul(u, w_in, BF16, name="conv_in")
    y_p = gated_conv(h3, conv_w, conv_b, 0, BATCH, SEQ)
    y_s = gated_conv(h3, conv_w, conv_b, N_PROMPT // DEC_SEQ, DEC_BATCH, DEC_SEQ)
    return matmul_resident(y_p, w_out, F32, "conv_out", x_sample=y_s)


def kernel(x_prompt, x_sample, cache_l0_ckv, cache_l0_krope, cache_l1_k, cache_l1_v, c, c_ctx, ada_w, ada_b, ln1_g, ln1_b, ln2_g, ln2_b, mla_w_dq, mla_q_norm, mla_w_uq, mla_w_dkv, mla_kv_norm, mla_w_uk, mla_w_uv, mla_w_o, gqa_w_qkv, gqa_sink, gqa_w_o, fnet_w_out, conv_w_in, conv_w, conv_b, conv_w_out, peer_w_q, peer_sub_keys, peer_u, peer_v):
    x = jnp.concatenate([x_prompt.reshape(N_PROMPT, D_MODEL), x_sample.reshape(N_SAMPLE, D_MODEL)], axis=0)
    cond = jnp.concatenate([c_ctx[None, :], c, jnp.zeros((N_COND - 1 - DEC_BATCH, D_MODEL), F32)], axis=0)
    mods_all = ada_modulation_all(cond, ada_w, ada_b).reshape(DEPTH, N_COND, 6, 1, D_MODEL)

    u = modulate_tokens(x, mods_all[0])
    new_ckv = new_krope = new_kv = None
    for i in range(DEPTH):
        mods = mods_all[i]
        if i == 0:
            delta, new_ckv, new_krope = mla_mixer(u, cache_l0_ckv, cache_l0_krope, mla_w_dq, mla_q_norm, mla_w_uq,
                                                  mla_w_dkv, mla_kv_norm, mla_w_uk, mla_w_uv, mla_w_o)
        elif i == 1:
            delta, new_kv = gqa_mixer(u, cache_l1_k, cache_l1_v, gqa_w_qkv, gqa_sink, gqa_w_o)
        elif i == 2:
            delta = fnet_mixer(u, fnet_w_out)
        else:
            delta = conv_mixer(u, conv_w_in, conv_w, conv_b, conv_w_out)
        x, u = post_norm(x, delta, mods, 2, ln1_g[i], ln1_b[i], mods, 3)
        delta = peer_ffn(u, peer_w_q, peer_sub_keys, peer_u, peer_v, i)
        if i + 1 < DEPTH:
            x, u = post_norm(x, delta, mods, 5, ln2_g[i], ln2_b[i], mods_all[i + 1], 0)
        else:
            y_prompt = post_norm(x, delta, mods, 5, ln2_g[i], ln2_b[i], rows=(0, N_PROMPT))
            y_sample = post_norm(x, delta, mods, 5, ln2_g[i], ln2_b[i], rows=(N_PROMPT, N_SAMPLE))

    nk = GQA_KV_HEADS * GQA_HEAD_DIM
    return (y_prompt.reshape(BATCH, SEQ, D_MODEL),
            y_sample.reshape(DEC_BATCH, DEC_SEQ, D_MODEL),
            new_ckv[:N_PROMPT].reshape(BATCH, SEQ, MLA_KV_LORA),
            new_krope[:N_PROMPT].reshape(BATCH, SEQ, MLA_ROPE),
            new_kv[:N_PROMPT, :nk].reshape(BATCH, SEQ, GQA_KV_HEADS, GQA_HEAD_DIM),
            new_kv[:N_PROMPT, nk:].reshape(BATCH, SEQ, GQA_KV_HEADS, GQA_HEAD_DIM))
```

```python
import functools
import math

import numpy as np
import jax
import jax.numpy as jnp
from jax import lax
from jax.experimental import pallas as pl
from jax.experimental.pallas import tpu as pltpu

F32 = jnp.float32
BF16 = jnp.bfloat16

D_MODEL = 2048
BATCH = 16
SEQ = 256
DEPTH = 4
DEC_BATCH = 2
DEC_SEQ = 1024
PAST_LEN = 256
GRID_W = 64
ROPE_BASE = 10000.0
LN_EPS = 1e-5
RMS_EPS = 1e-6
NEG_INF = -1e30
DEEPNORM_ALPHA = (2 * DEPTH) ** 0.25

MLA_HEADS = 16
MLA_Q_LORA = 512
MLA_KV_LORA = 512
MLA_NOPE = 128
MLA_ROPE = 64
MLA_V = 128
MLA_SCALE = (MLA_NOPE + MLA_ROPE) ** -0.5

GQA_HEADS = 32
GQA_KV_HEADS = 8
GQA_HEAD_DIM = 64
WINDOW = 128
GQA_SCALE = GQA_HEAD_DIM ** -0.5

FNET_GROUPS = 4
FNET_GROUP_DIM = D_MODEL // FNET_GROUPS

PEER_HEADS = 8
PEER_N_KEYS = 128
PEER_N_EXPERTS = PEER_N_KEYS * PEER_N_KEYS
PEER_HALF = 128
PEER_TOPK = 16

N_PROMPT = BATCH * SEQ
N_SAMPLE = DEC_BATCH * DEC_SEQ
N_TOK = N_PROMPT + N_SAMPLE
N_COND = 8

MIB = 1024 * 1024

NT_DIMS = (((1,), (1,)), ((), ()))
TN_DIMS = (((0,), (0,)), ((), ()))


def _params(sem, vmem_mib, flags=None):
    return pltpu.CompilerParams(dimension_semantics=sem, vmem_limit_bytes=vmem_mib * MIB, flags=flags)


def _group_of_tile(i, tm):
    return jnp.maximum((i * tm) // DEC_SEQ - (N_PROMPT // DEC_SEQ - 1), 0)


def _mod_spec(tm, which):
    return pl.BlockSpec((None, None, 1, D_MODEL), lambda i: (_group_of_tile(i, tm), which, 0, 0))


def _ada_kernel(c_ref, w_ref, b_ref, o_ref):
    c = c_ref[...]
    s = (c * jax.nn.sigmoid(c)).astype(BF16)
    o_ref[...] = jnp.dot(s, w_ref[...].astype(BF16), preferred_element_type=F32) + b_ref[...]


def ada_modulation_all(cond, ada_w, ada_b):
    tn = 1024
    n_out = 6 * D_MODEL
    return pl.pallas_call(
        _ada_kernel,
        out_shape=jax.ShapeDtypeStruct((DEPTH, N_COND, n_out), F32),
        grid=(DEPTH, n_out // tn),
        in_specs=[
            pl.BlockSpec((N_COND, D_MODEL), lambda l, j: (0, 0)),
            pl.BlockSpec((None, D_MODEL, tn), lambda l, j: (l, 0, j)),
            pl.BlockSpec((None, 1, tn), lambda l, j: (l, 0, j)),
        ],
        out_specs=pl.BlockSpec((None, N_COND, tn), lambda l, j: (l, 0, j)),
        compiler_params=_params(("parallel", "parallel"), 40),
        name="ada_modulation",
    )(cond, ada_w, ada_b.reshape(DEPTH, 1, n_out))


def _modulate_kernel(x_ref, sh_ref, sc_ref, u_ref):
    u_ref[...] = (x_ref[...] * (1.0 + sc_ref[...]) + sh_ref[...]).astype(u_ref.dtype)


def modulate_tokens(x, mods):
    tm = 256
    return pl.pallas_call(
        _modulate_kernel,
        out_shape=jax.ShapeDtypeStruct((N_TOK, D_MODEL), BF16),
        grid=(N_TOK // tm,),
        in_specs=[pl.BlockSpec((tm, D_MODEL), lambda i: (i, 0)), _mod_spec(tm, 0), _mod_spec(tm, 1)],
        out_specs=pl.BlockSpec((tm, D_MODEL), lambda i: (i, 0)),
        compiler_params=_params(("parallel",), 24),
        name="modulate",
    )(x, mods, mods)


def _layer_norm_rows(y, g, b):
    mu = jnp.mean(y, axis=-1, keepdims=True)
    yc = y - mu
    var = jnp.mean(yc * yc, axis=-1, keepdims=True)
    return yc * lax.rsqrt(var + LN_EPS) * g + b


def _post_norm_kernel(x_ref, d_ref, gate_ref, g_ref, b_ref, xo_ref):
    y = DEEPNORM_ALPHA * x_ref[...] + gate_ref[...] * d_ref[...].astype(F32)
    xo_ref[...] = _layer_norm_rows(y, g_ref[...], b_ref[...])


def _post_norm_mod_kernel(x_ref, d_ref, gate_ref, g_ref, b_ref, sh_ref, sc_ref, xo_ref, uo_ref):
    y = DEEPNORM_ALPHA * x_ref[...] + gate_ref[...] * d_ref[...].astype(F32)
    xn = _layer_norm_rows(y, g_ref[...], b_ref[...])
    xo_ref[...] = xn
    uo_ref[...] = (xn * (1.0 + sc_ref[...]) + sh_ref[...]).astype(uo_ref.dtype)


def post_norm(x, delta, mods, gate_idx, g, b, next_mods=None, next_idx=None, rows=(0, N_TOK)):
    tm = 256
    row = pl.BlockSpec((tm, D_MODEL), lambda i: (i, 0))
    vec = pl.BlockSpec((1, D_MODEL), lambda i: (0, 0))
    ins = [x, delta, mods, g.reshape(1, D_MODEL), b.reshape(1, D_MODEL)]
    specs = [row, row, _mod_spec(tm, gate_idx), vec, vec]
    if next_mods is None:
        i0 = rows[0] // tm
        row_in = pl.BlockSpec((tm, D_MODEL), lambda i: (i0 + i, 0))
        gate = pl.BlockSpec((None, None, 1, D_MODEL), lambda i: (_group_of_tile(i0 + i, tm), gate_idx, 0, 0))
        return pl.pallas_call(
            _post_norm_kernel,
            out_shape=jax.ShapeDtypeStruct((rows[1], D_MODEL), F32),
            grid=(rows[1] // tm,), in_specs=[row_in, row_in, gate, vec, vec], out_specs=row,
            compiler_params=_params(("parallel",), 32), name="post_norm",
        )(*ins)
    ins += [next_mods, next_mods]
    specs += [_mod_spec(tm, next_idx), _mod_spec(tm, next_idx + 1)]
    return pl.pallas_call(
        _post_norm_mod_kernel,
        out_shape=(jax.ShapeDtypeStruct((N_TOK, D_MODEL), F32), jax.ShapeDtypeStruct((N_TOK, D_MODEL), BF16)),
        grid=(N_TOK // tm,), in_specs=specs, out_specs=(row, row),
        compiler_params=_params(("parallel",), 32), name="post_norm_mod",
    )(*ins)


def _mm_kernel(x_ref, w_ref, o_ref):
    o_ref[...] = jnp.dot(x_ref[...], w_ref[...].astype(BF16), preferred_element_type=F32).astype(o_ref.dtype)


def _mm_resident_kernel(*refs, prompt_tiles):
    if prompt_tiles is None:
        x_ref, w_ref, o_ref, wb_ref = refs
    else:
        x_ref, xs_ref, w_ref, o_ref, wb_ref = refs
    i = pl.program_id(0)

    @pl.when(i == 0)
    def _():
        wb_ref[...] = w_ref[...].astype(BF16)

    if prompt_tiles is None:
        o_ref[...] = jnp.dot(x_ref[...], wb_ref[...], preferred_element_type=F32).astype(o_ref.dtype)
    else:
        @pl.when(i < prompt_tiles)
        def _():
            o_ref[...] = jnp.dot(x_ref[...], wb_ref[...], preferred_element_type=F32).astype(o_ref.dtype)

        @pl.when(i >= prompt_tiles)
        def _():
            o_ref[...] = jnp.dot(xs_ref[...], wb_ref[...], preferred_element_type=F32).astype(o_ref.dtype)


def matmul_resident(x, w, out_dtype, name, x_sample=None, layer=None, col0=0, n=None):
    tm = 512
    k = x.shape[1]
    n = w.shape[-1] - col0 if n is None else n
    assert col0 % n == 0
    j0 = col0 // n
    rows = x.shape[0] + (0 if x_sample is None else x_sample.shape[0])
    once = pl.Buffered(1)
    if layer is None:
        w_spec = pl.BlockSpec((k, n), lambda i: (0, j0), pipeline_mode=once)
    else:
        w_spec = pl.BlockSpec((None, k, n), lambda i: (layer, 0, j0), pipeline_mode=once)
    if x_sample is None:
        prompt_tiles = None
        xs, x_specs = (x,), [pl.BlockSpec((tm, k), lambda i: (i, 0))]
    else:
        prompt_tiles = x.shape[0] // tm
        xs = (x, x_sample)
        x_specs = [pl.BlockSpec((tm, k), lambda i: (jnp.minimum(i, prompt_tiles - 1), 0)),
                   pl.BlockSpec((tm, k), lambda i: (jnp.maximum(i - prompt_tiles, 0), 0))]
    return pl.pallas_call(
        functools.partial(_mm_resident_kernel, prompt_tiles=prompt_tiles),
        out_shape=jax.ShapeDtypeStruct((rows, n), out_dtype),
        grid=(rows // tm,),
        in_specs=x_specs + [w_spec],
        out_specs=pl.BlockSpec((tm, n), lambda i: (i, 0)),
        scratch_shapes=[pltpu.VMEM((k, n), BF16)],
        compiler_params=_params(("arbitrary",), 48),
        name=name,
    )(*xs, w)


def matmul(x, w, out_dtype, tm=2048, tn=512, name="matmul", layer=None, col0=0, n=None):
    m, k = x.shape
    n = w.shape[-1] - col0 if n is None else n
    tm = min(tm, m)
    tn = min(tn, n)
    assert m % tm == 0 and n % tn == 0 and col0 % tn == 0, (m, n, tm, tn, col0)
    j0 = col0 // tn
    if layer is None:
        w_spec = pl.BlockSpec((k, tn), lambda i, j: (0, j0 + j))
    else:
        w_spec = pl.BlockSpec((None, k, tn), lambda i, j: (layer, 0, j0 + j))
    return pl.pallas_call(
        _mm_kernel,
        out_shape=jax.ShapeDtypeStruct((m, n), out_dtype),
        grid=(m // tm, n // tn),
        in_specs=[pl.BlockSpec((tm, k), lambda i, j: (i, 0)), w_spec],
        out_specs=pl.BlockSpec((tm, tn), lambda i, j: (i, j)),
        compiler_params=_params(("parallel", "parallel"), 48),
        name=name,
    )(x, w)


def _rms_rows(y, g):
    return y * lax.rsqrt(jnp.mean(y * y, axis=-1, keepdims=True) + RMS_EPS) * g


def _mla_down_kernel(x_ref, w_ref, qn_ref, kvn_ref, cq_ref, ckv_ref, kr_ref):
    y = jnp.dot(x_ref[...], w_ref[...].astype(BF16), preferred_element_type=F32)
    cq_ref[...] = _rms_rows(y[:, :MLA_Q_LORA], qn_ref[...]).astype(cq_ref.dtype)
    ckv_ref[...] = _rms_rows(y[:, MLA_Q_LORA:MLA_Q_LORA + MLA_KV_LORA], kvn_ref[...])
    kr_ref[...] = y[:, MLA_Q_LORA + MLA_KV_LORA:MLA_Q_LORA + MLA_KV_LORA + MLA_ROPE]


def mla_down(u, w_cat, q_norm, kv_norm):
    tm = 512
    n = w_cat.shape[1]
    return pl.pallas_call(
        _mla_down_kernel,
        out_shape=(jax.ShapeDtypeStruct((N_TOK, MLA_Q_LORA), BF16),
                   jax.ShapeDtypeStruct((N_TOK, MLA_KV_LORA), F32),
                   jax.ShapeDtypeStruct((N_TOK, MLA_ROPE), F32)),
        grid=(N_TOK // tm,),
        in_specs=[pl.BlockSpec((tm, D_MODEL), lambda i: (i, 0)),
                  pl.BlockSpec((D_MODEL, n), lambda i: (0, 0)),
                  pl.BlockSpec((1, MLA_Q_LORA), lambda i: (0, 0)),
                  pl.BlockSpec((1, MLA_KV_LORA), lambda i: (0, 0))],
        out_specs=(pl.BlockSpec((tm, MLA_Q_LORA), lambda i: (i, 0)),
                   pl.BlockSpec((tm, MLA_KV_LORA), lambda i: (i, 0)),
                   pl.BlockSpec((tm, MLA_ROPE), lambda i: (i, 0))),
        compiler_params=_params(("parallel",), 48),
        name="mla_down",
    )(u, w_cat, q_norm.reshape(1, -1), kv_norm.reshape(1, -1))


def _rope_tables():
    t = np.arange(DEC_SEQ)
    quarter = MLA_ROPE // 4
    inv_freq = ROPE_BASE ** (-np.arange(quarter, dtype=np.float64) / quarter)
    ang_row = (t // GRID_W)[:, None] * inv_freq[None, :]
    ang_col = (t % GRID_W)[:, None] * inv_freq[None, :]
    cos = np.concatenate([np.cos(ang_row)] * 2 + [np.cos(ang_col)] * 2, -1)
    sin = np.concatenate([-np.sin(ang_row), np.sin(ang_row), -np.sin(ang_col), np.sin(ang_col)], -1)
    return np.tile(cos, (1, 2)), np.tile(sin, (1, 2))


def _rope_kernel(x_ref, c_ref, s_ref, o_ref):
    x = x_ref[...].astype(F32)
    w = x.shape[-1]
    reps = w // c_ref.shape[-1]
    cos = jnp.tile(c_ref[...], (1, reps))
    sin = jnp.tile(s_ref[...], (1, reps))
    lane = lax.broadcasted_iota(jnp.int32, x.shape, 1)
    partner = jnp.where((lane % 32) < 16, pltpu.roll(x, w - 16, 1), pltpu.roll(x, 16, 1))
    o_ref[...] = (x * cos + partner * sin).astype(o_ref.dtype)


def rope_rows(x, row_block0, col_block, width, n_rows, cos, sin, out_dtype, tr=256):
    t_blocks = cos.shape[0] // tr
    return pl.pallas_call(
        _rope_kernel,
        out_shape=jax.ShapeDtypeStruct((n_rows, width), out_dtype),
        grid=(n_rows // tr,),
        in_specs=[pl.BlockSpec((tr, width), lambda i: (row_block0 + i, col_block)),
                  pl.BlockSpec((tr, 128), lambda i: (i % t_blocks, 0)),
                  pl.BlockSpec((tr, 128), lambda i: (i % t_blocks, 0))],
        out_specs=pl.BlockSpec((tr, width), lambda i: (i, 0)),
        compiler_params=_params(("parallel",), 32),
        name="axial_rope",
    )(x, cos, sin)


def _softmax_parts(parts, sink):
    m = parts[0].max(axis=-1, keepdims=True)
    for s in parts[1:]:
        m = jnp.maximum(m, s.max(axis=-1, keepdims=True))
    if sink is not None:
        m = jnp.maximum(m, sink)
    ps = [jnp.exp(s - m) for s in parts]
    l = ps[0].sum(axis=-1, keepdims=True)
    for p in ps[1:]:
        l = l + p.sum(axis=-1, keepdims=True)
    if sink is not None:
        l = l + jnp.exp(sink - m)
    return ps, 1.0 / l


def _mla_attn_kernel(*refs, with_ctx):
    if with_ctx:
        qn_ref, qr_ref, kn_ref, v_ref, kr_ref, kn2_ref, v2_ref, kr2_ref, o_ref = refs
    else:
        qn_ref, qr_ref, kn_ref, v_ref, kr_ref, o_ref = refs
    kr = kr_ref[...].astype(BF16)
    kr2 = kr2_ref[...].astype(BF16) if with_ctx else None
    for h in range(MLA_HEADS):
        n0, n1 = h * MLA_NOPE, (h + 1) * MLA_NOPE
        qn = qn_ref[:, n0:n1]
        qr = qr_ref[:, h * MLA_ROPE:(h + 1) * MLA_ROPE]
        s = (lax.dot_general(qn, kn_ref[:, n0:n1], NT_DIMS, preferred_element_type=F32)
             + lax.dot_general(qr, kr, NT_DIMS, preferred_element_type=F32)) * MLA_SCALE
        parts = [s]
        if with_ctx:
            s2 = (lax.dot_general(qn, kn2_ref[:, n0:n1], NT_DIMS, preferred_element_type=F32)
                  + lax.dot_general(qr, kr2, NT_DIMS, preferred_element_type=F32)) * MLA_SCALE
            parts.append(s2)
        ps, inv_l = _softmax_parts(parts, None)
        o = jnp.dot(ps[0].astype(BF16), v_ref[:, n0:n1], preferred_element_type=F32)
        if with_ctx:
            o = o + jnp.dot(ps[1].astype(BF16), v2_ref[:, n0:n1], preferred_element_type=F32)
        o_ref[:, n0:n1] = (o * inv_l).astype(o_ref.dtype)


def mla_attention_prompt(q, kv_up, kr):
    w = MLA_HEADS * MLA_NOPE
    return pl.pallas_call(
        functools.partial(_mla_attn_kernel, with_ctx=False),
        out_shape=jax.ShapeDtypeStruct((N_PROMPT, w), BF16),
        grid=(BATCH,),
        in_specs=[pl.BlockSpec((SEQ, w), lambda b: (b, 0)),
                  pl.BlockSpec((SEQ, MLA_HEADS * MLA_ROPE), lambda b: (b, w // (MLA_HEADS * MLA_ROPE))),
                  pl.BlockSpec((SEQ, w), lambda b: (b, 0)),
                  pl.BlockSpec((SEQ, w), lambda b: (b, 1)),
                  pl.BlockSpec((SEQ, MLA_ROPE), lambda b: (b, 0))],
        out_specs=pl.BlockSpec((SEQ, w), lambda b: (b, 0)),
        compiler_params=_params(("parallel",), 32),
        name="mla_attention_prompt",
    )(q, q, kv_up, kv_up, kr)


def mla_attention_sample(qn, qr_rot, kv_up, kr_rot, kr_cache):
    w = MLA_HEADS * MLA_NOPE
    tq = 256
    nq = DEC_SEQ // tq
    q0 = N_PROMPT // tq
    lat0 = N_PROMPT // DEC_SEQ
    ctx0 = N_TOK // PAST_LEN
    return pl.pallas_call(
        functools.partial(_mla_attn_kernel, with_ctx=True),
        out_shape=jax.ShapeDtypeStruct((N_SAMPLE, w), BF16),
        grid=(DEC_BATCH, nq),
        in_specs=[pl.BlockSpec((tq, w), lambda b, i: (q0 + b * nq + i, 0)),
                  pl.BlockSpec((tq, MLA_HEADS * MLA_ROPE), lambda b, i: (b * nq + i, 0)),
                  pl.BlockSpec((DEC_SEQ, w), lambda b, i: (lat0 + b, 0)),
                  pl.BlockSpec((DEC_SEQ, w), lambda b, i: (lat0 + b, 1)),
                  pl.BlockSpec((DEC_SEQ, MLA_ROPE), lambda b, i: (b, 0)),
                  pl.BlockSpec((PAST_LEN, w), lambda b, i: (ctx0 + b, 0)),
                  pl.BlockSpec((PAST_LEN, w), lambda b, i: (ctx0 + b, 1)),
                  pl.BlockSpec((PAST_LEN, MLA_ROPE), lambda b, i: (b, 0))],
        out_specs=pl.BlockSpec((tq, w), lambda b, i: (b * nq + i, 0)),
        compiler_params=_params(("parallel", "parallel"), 56),
        name="mla_attention_sample",
    )(qn, qr_rot, kv_up, kv_up, kr_rot, kv_up, kv_up, kr_cache)


def _gqa_attn_kernel(*refs, with_ctx, tq):
    if with_ctx:
        sink_ref, q_ref, k_ref, v_ref, k2_ref, v2_ref, o_ref = refs
    else:
        sink_ref, q_ref, k_ref, v_ref, o_ref = refs
    group = GQA_HEADS // GQA_KV_HEADS
    d = GQA_HEAD_DIM
    if with_ctx:
        t = pl.program_id(1) * tq + lax.broadcasted_iota(jnp.int32, (tq, DEC_SEQ), 0)
        s_pos = lax.broadcasted_iota(jnp.int32, (tq, DEC_SEQ), 1)
        in_window = jnp.abs(t - s_pos) <= WINDOW
    for hk in range(GQA_KV_HEADS):
        k = k_ref[:, hk * d:(hk + 1) * d].astype(BF16)
        v = v_ref[:, hk * d:(hk + 1) * d].astype(BF16)
        if with_ctx:
            k2 = k2_ref[:, hk * d:(hk + 1) * d].astype(BF16)
            v2 = v2_ref[:, hk * d:(hk + 1) * d].astype(BF16)
        for g in range(group):
            h = hk * group + g
            q = q_ref[:, h * d:(h + 1) * d]
            s = lax.dot_general(q, k, NT_DIMS, preferred_element_type=F32) * GQA_SCALE
            parts = [s]
            if with_ctx:
                parts = [jnp.where(in_window, s, NEG_INF),
                         lax.dot_general(q, k2, NT_DIMS, preferred_element_type=F32) * GQA_SCALE]
            ps, inv_l = _softmax_parts(parts, sink_ref[h])
            o = jnp.dot(ps[0].astype(BF16), v, preferred_element_type=F32)
            if with_ctx:
                o = o + jnp.dot(ps[1].astype(BF16), v2, preferred_element_type=F32)
            o_ref[:, h * d:(h + 1) * d] = (o * inv_l).astype(o_ref.dtype)


def gqa_attention_prompt(q, kv, sink):
    wq = GQA_HEADS * GQA_HEAD_DIM
    wk = GQA_KV_HEADS * GQA_HEAD_DIM
    return pl.pallas_call(
        functools.partial(_gqa_attn_kernel, with_ctx=False, tq=SEQ),
        out_shape=jax.ShapeDtypeStruct((N_PROMPT, wq), BF16),
        grid=(BATCH,),
        in_specs=[pl.BlockSpec(memory_space=pltpu.SMEM),
                  pl.BlockSpec((SEQ, wq), lambda b: (b, 0)),
                  pl.BlockSpec((SEQ, wk), lambda b: (b, 0)),
                  pl.BlockSpec((SEQ, wk), lambda b: (b, 1))],
        out_specs=pl.BlockSpec((SEQ, wq), lambda b: (b, 0)),
        compiler_params=_params(("parallel",), 32),
        name="gqa_attention_prompt",
    )(sink, q, kv, kv)


def gqa_attention_sample(q_rot, k_rot, kv, k_cache, v_cache, sink):
    wq = GQA_HEADS * GQA_HEAD_DIM
    wk = GQA_KV_HEADS * GQA_HEAD_DIM
    tq = 256
    nq = DEC_SEQ // tq
    lat0 = N_PROMPT // DEC_SEQ
    return pl.pallas_call(
        functools.partial(_gqa_attn_kernel, with_ctx=True, tq=tq),
        out_shape=jax.ShapeDtypeStruct((N_SAMPLE, wq), BF16),
        grid=(DEC_BATCH, nq),
        in_specs=[pl.BlockSpec(memory_space=pltpu.SMEM),
                  pl.BlockSpec((tq, wq), lambda b, i: (b * nq + i, 0)),
                  pl.BlockSpec((DEC_SEQ, wk), lambda b, i: (b, 0)),
                  pl.BlockSpec((DEC_SEQ, wk), lambda b, i: (lat0 + b, 1)),
                  pl.BlockSpec((PAST_LEN, wk), lambda b, i: (b, 0)),
                  pl.BlockSpec((PAST_LEN, wk), lambda b, i: (b, 0))],
        out_specs=pl.BlockSpec((tq, wq), lambda b, i: (b * nq + i, 0)),
        compiler_params=_params(("parallel", "parallel"), 56),
        name="gqa_attention_sample",
    )(sink, q_rot, k_rot, kv, k_cache, v_cache)


def _dft_tables(n):
    jk = (np.arange(n)[:, None] * np.arange(n)[None, :]) % n
    ang = 2.0 * np.pi * jk / n
    return np.cos(ang) / math.sqrt(n), np.sin(ang) / math.sqrt(n)


def _fnet_kernel(x_ref, ct_ref, st_ref, cc_ref, sc_ref, o_ref):
    x = x_ref[...]
    y1 = jnp.dot(x, cc_ref[...], preferred_element_type=F32).astype(BF16)
    y2 = jnp.dot(x, sc_ref[...], preferred_element_type=F32).astype(BF16)
    o = (jnp.dot(ct_ref[...], y1, preferred_element_type=F32)
         - jnp.dot(st_ref[...], y2, preferred_element_type=F32))
    o_ref[...] = o.astype(o_ref.dtype)


def fourier_real_2d(u, seq_block0, n_seq, t):
    ct, st = (jnp.asarray(a, BF16) for a in _dft_tables(t))
    cc, sc = (jnp.asarray(a, BF16) for a in _dft_tables(FNET_GROUP_DIM))
    gd = FNET_GROUP_DIM
    return pl.pallas_call(
        _fnet_kernel,
        out_shape=jax.ShapeDtypeStruct((n_seq * t, D_MODEL), BF16),
        grid=(n_seq, FNET_GROUPS),
        in_specs=[pl.BlockSpec((t, gd), lambda b, g: (seq_block0 + b, g)),
                  pl.BlockSpec((t, t), lambda b, g: (0, 0)),
                  pl.BlockSpec((t, t), lambda b, g: (0, 0)),
                  pl.BlockSpec((gd, gd), lambda b, g: (0, 0)),
                  pl.BlockSpec((gd, gd), lambda b, g: (0, 0))],
        out_specs=pl.BlockSpec((t, gd), lambda b, g: (b, g)),
        compiler_params=_params(("parallel", "parallel"), 32),
        name="fourier_mix",
    )(u, ct, st, cc, sc)


def _conv_kernel(b_ref, c_ref, h_ref, w_ref, cb_ref, o_ref):
    z = c_ref[...].astype(F32) * h_ref[...].astype(F32)
    t = z.shape[0]
    row = lax.broadcasted_iota(jnp.int32, z.shape, 0)
    z_prev = jnp.where(row == 0, 0.0, pltpu.roll(z, 1, 0))
    z_next = jnp.where(row == t - 1, 0.0, pltpu.roll(z, t - 1, 0))
    conv = z_prev * w_ref[0:1, :] + z * w_ref[1:2, :] + z_next * w_ref[2:3, :] + cb_ref[...]
    o_ref[...] = (b_ref[...].astype(F32) * conv).astype(o_ref.dtype)


def gated_conv(h3, conv_w, conv_b, seq_block0, n_seq, t):
    tn = 512
    nj = D_MODEL // tn
    return pl.pallas_call(
        _conv_kernel,
        out_shape=jax.ShapeDtypeStruct((n_seq * t, D_MODEL), BF16),
        grid=(n_seq, nj),
        in_specs=[pl.BlockSpec((t, tn), lambda s, j: (seq_block0 + s, j)),
                  pl.BlockSpec((t, tn), lambda s, j: (seq_block0 + s, nj + j)),
                  pl.BlockSpec((t, tn), lambda s, j: (seq_block0 + s, 2 * nj + j)),
                  pl.BlockSpec((3, tn), lambda s, j: (0, j)),
                  pl.BlockSpec((1, tn), lambda s, j: (0, j))],
        out_specs=pl.BlockSpec((t, tn), lambda s, j: (s, j)),
        compiler_params=_params(("parallel", "parallel"), 32),
        name="gated_conv",
    )(h3, h3, h3, conv_w, conv_b.reshape(1, D_MODEL))


_N_RANKS = PEER_TOPK + 1
_CAND_PAIRS = [(a, b) for a in range(_N_RANKS) for b in range(_N_RANKS) if (a + 1) * (b + 1) <= _N_RANKS]


_NO_RANK = 127.0


def _top_values_exact(s, k, want_rank):
    n = s.shape[0]
    idx = lax.broadcasted_iota(jnp.int32, s.shape, 0)
    rank = jnp.full(s.shape, _NO_RANK, F32) if want_rank else None
    out = []
    for it in range(k):
        m = jnp.max(s, axis=0, keepdims=True)
        out.append(m)
        first = jnp.min(jnp.where(s == m, idx, n), axis=0, keepdims=True)
        hit = idx == first
        s = jnp.where(hit, -jnp.inf, s)
        if want_rank:
            rank = jnp.where(hit, float(it), rank)
    return out, rank


def _all_equal(x, value):
    return jnp.min(jnp.where(x == value, 1.0, 0.0)) > 0.5


def _top_values_distinct(s, k, n_pad):
    out = []
    for it in range(k):
        m = jnp.max(s, axis=0, keepdims=True)
        out.append(m)
        s = jnp.where(s == m, -jnp.inf, s)
    removed = jnp.sum(jnp.where(s == -jnp.inf, 1.0, 0.0), axis=0, keepdims=True)
    return out, _all_equal(removed, float(k + n_pad))


def _batcher_network(n):
    def merge(lo, hi, r):
        step = r * 2
        if step < hi - lo:
            yield from merge(lo, hi, step)
            yield from merge(lo + r, hi, step)
            yield from [(i, i + r) for i in range(lo + r, hi - r, step)]
        else:
            yield (lo, lo + r)

    def sort(lo, hi):
        if hi - lo >= 1:
            mid = lo + (hi - lo) // 2
            yield from sort(lo, mid)
            yield from sort(mid + 1, hi)
            yield from merge(lo, hi, 1)

    return list(sort(0, n - 1))


_SUBLANES = 8
_KEY_TILES = PEER_N_KEYS // _SUBLANES
_SORT_KEY_TILES = _batcher_network(_KEY_TILES)


def _top_values_sorted(s, k):
    tiles = [s[j * _SUBLANES:(j + 1) * _SUBLANES, :] for j in range(_KEY_TILES)]
    for i, j in _SORT_KEY_TILES:
        tiles[i], tiles[j] = jnp.maximum(tiles[i], tiles[j]), jnp.minimum(tiles[i], tiles[j])
    out = []
    pops = jnp.zeros_like(tiles[0])
    for it in range(k):
        m = jnp.max(tiles[0], axis=0, keepdims=True)
        out.append(m)
        hit = tiles[0] == m
        pops = pops + jnp.where(hit, 1.0, 0.0)
        depth = k - 1 - it
        for j in range(min(depth, _KEY_TILES - 1)):
            tiles[j] = jnp.where(hit, tiles[j + 1], tiles[j])
        if depth >= _KEY_TILES - 1:
            tiles[-1] = jnp.where(hit, -jnp.inf, tiles[-1])
    return out, _all_equal(jnp.sum(pops, axis=0, keepdims=True), float(k))


def _ranks_from_values(s, values):
    rank = jnp.full(s.shape, _NO_RANK, F32)
    for r in reversed(range(len(values))):
        rank = jnp.where(s >= values[r], float(r), rank)
    return rank


def _dup_bf16_words(x):
    bits = pltpu.bitcast(x.astype(BF16).astype(F32), jnp.int32)
    return bits | lax.shift_right_logical(bits, jnp.int32(16))


def _route_head(h, s1, s2, fast, bq_ref, r2_ref, aw_ref, cw_ref):
    tm = s1.shape[1]
    if fast:
        v1, ok1 = _top_values_sorted(s1, _N_RANKS)
        v2, ok2 = _top_values_sorted(s2, _N_RANKS)
        rank2 = _ranks_from_values(s2, v2)
    else:
        v1, _ = _top_values_exact(s1, _N_RANKS, False)
        v2, rank2 = _top_values_exact(s2, _N_RANKS, True)
    n_pad = (-len(_CAND_PAIRS)) % 8
    n_cand = len(_CAND_PAIRS) + n_pad
    cand_row = lax.broadcasted_iota(jnp.int32, (n_cand, tm), 0)
    cand = jnp.full((n_cand, tm), -jnp.inf, F32)
    for r, (a, b) in enumerate(_CAND_PAIRS):
        cand = jnp.where(cand_row == r, v1[a] + v2[b], cand)
    if fast:
        top, ok3 = _top_values_distinct(cand, _N_RANKS, n_pad)
    else:
        top, _ = _top_values_exact(cand, _N_RANKS, False)
    tau = 0.5 * (top[PEER_TOPK - 1] + top[PEER_TOPK])
    z = jnp.exp(top[0] - top[0])
    for kk in range(1, PEER_TOPK):
        z = z + jnp.exp(top[kk] - top[0])
    inv_z = 1.0 / z
    count = jnp.zeros_like(s1)
    for b in range(PEER_TOPK):
        count = jnp.where(s1 >= tau - v2[b], float(b + 1), count)
    rows = slice(h * PEER_N_KEYS, (h + 1) * PEER_N_KEYS)
    bq_ref[rows, :] = (jnp.exp(s2 - v2[0]) * inv_z).astype(BF16)
    r2_ref[rows, :] = rank2.astype(BF16)
    a_half = 0.5 * jnp.exp(s1 - v1[0])
    aw_ref[:, h, :, :] = _dup_bf16_words(a_half).reshape(PEER_N_KEYS // 8, 8, tm)
    cw_ref[:, h, :, :] = _dup_bf16_words(count).reshape(PEER_N_KEYS // 8, 8, tm)
    return jnp.logical_and(jnp.logical_and(ok1, ok2), ok3) if fast else None


def _route_kernel(q_ref, sk_ref, bq_ref, r2_ref, aw_ref, cw_ref):
    outs = (bq_ref, r2_ref, aw_ref, cw_ref)
    for h in range(PEER_HEADS):
        s = []
        for c in range(2):
            hc = 2 * h + c
            s.append(lax.dot_general(sk_ref[hc].astype(BF16), q_ref[:, hc * PEER_HALF:(hc + 1) * PEER_HALF],
                                     NT_DIMS, preferred_element_type=F32))
        no_ties = _route_head(h, s[0], s[1], True, *outs)

        @pl.when(jnp.logical_not(no_ties))
        def _():
            _route_head(h, s[0], s[1], False, *outs)


def peer_route(q, sub_keys, layer):
    tm = 256
    n_tok = q.shape[0]
    sk = sub_keys.reshape(-1, PEER_HEADS * 2, PEER_N_KEYS, PEER_HALF)
    tiles = PEER_N_KEYS // 8
    col_out = jax.ShapeDtypeStruct((PEER_HEADS * PEER_N_KEYS, n_tok), BF16)
    col_spec = pl.BlockSpec((PEER_HEADS * PEER_N_KEYS, tm), lambda i: (0, i))
    row_out = jax.ShapeDtypeStruct((tiles, PEER_HEADS, 8, n_tok), jnp.int32)
    row_spec = pl.BlockSpec((tiles, PEER_HEADS, 8, tm), lambda i: (0, 0, 0, i))
    return pl.pallas_call(
        _route_kernel,
        out_shape=(col_out, col_out, row_out, row_out),
        grid=(n_tok // tm,),
        in_specs=[pl.BlockSpec((tm, PEER_HEADS * 2 * PEER_HALF), lambda i: (i, 0)),
                  pl.BlockSpec((None, PEER_HEADS * 2, PEER_N_KEYS, PEER_HALF), lambda i: (layer, 0, 0, 0))],
        out_specs=(col_spec, col_spec, row_spec, row_spec),
        compiler_params=_params(("parallel",), 32),
        name="peer_route",
    )(q, sk)


PEER_TE = 512
PEER_TM = 1024
PEER_CHUNK = 512
PEER_N_TILES = PEER_N_EXPERTS // PEER_TE


def _bf16_rows(word_row, n_rows):
    return pltpu.bitcast(jnp.broadcast_to(word_row, (n_rows // 2, word_row.shape[1])), BF16)


PEER_LANES = 256
PEER_ROWS = 128


def _peer_gated_slab(k, k0, bq_ref, r2_ref, aw_ref, cw_ref, ht_ref, at_ref):
    for j in range(ht_ref.shape[1] // PEER_LANES):
        lanes = slice(j * PEER_LANES, (j + 1) * PEER_LANES)
        for r in range(PEER_N_KEYS // PEER_ROWS):
            g = [None, None]
            for h in range(PEER_HEADS):
                count = _bf16_rows(cw_ref[0, h, k0 + k:k0 + k + 1, lanes], PEER_ROWS)
                a_half = _bf16_rows(aw_ref[0, h, k0 + k:k0 + k + 1, lanes], PEER_ROWS)
                rows = slice(h * PEER_N_KEYS + r * PEER_ROWS, h * PEER_N_KEYS + (r + 1) * PEER_ROWS)
                bq = bq_ref[rows, lanes]
                contrib = jnp.where(r2_ref[rows, lanes] < count, bq, jnp.zeros_like(bq)) * a_half
                g[h % 2] = contrib if g[h % 2] is None else g[h % 2] + contrib
            krows = slice(k * PEER_N_KEYS + r * PEER_ROWS, k * PEER_N_KEYS + (r + 1) * PEER_ROWS)
            hk = ht_ref[krows, lanes]
            gelu2 = hk * (1.0 + lax.erf(hk * math.sqrt(0.5)))
            at_ref[krows, lanes] = gelu2.astype(BF16) * (g[0] + g[1])


def _peer_step(x_ref, bq_ref, r2_ref, aw_ref, cw_ref, u_ref, v_ref, o_ref, at_ref, ht_write, ht_read, k0):
    keys_per_tile = PEER_TE // PEER_N_KEYS
    keys_per_chunk = PEER_CHUNK // PEER_N_KEYS
    tm = x_ref.shape[0]
    tok = tm // keys_per_tile
    u = u_ref[...].astype(BF16) if ht_write is not None else None
    for k in range(keys_per_tile):
        if ht_read is not None:
            _peer_gated_slab(k, k0, bq_ref, r2_ref, aw_ref, cw_ref, ht_read, at_ref)
        if ht_write is not None:
            ht_write[:, k * tok:(k + 1) * tok] = lax.dot_general(
                u, x_ref[k * tok:(k + 1) * tok, :], NT_DIMS, preferred_element_type=F32)
        if ht_read is not None and (k + 1) % keys_per_chunk == 0:
            chunk = slice((k + 1) * PEER_N_KEYS - PEER_CHUNK, (k + 1) * PEER_N_KEYS)
            o_ref[...] += lax.dot_general(at_ref[chunk, :], v_ref[chunk, :].astype(BF16), TN_DIMS,
                                          preferred_element_type=F32)


def _peer_kernel(x_ref, bq_ref, r2_ref, aw_ref, cw_ref, u_ref, v_ref, o_ref, ht0_ref, ht1_ref, at_ref):
    e = pl.program_id(1)
    last = PEER_N_TILES
    keys_per_tile = PEER_TE // PEER_N_KEYS
    assert 8 // keys_per_tile == 2
    step = functools.partial(_peer_step, x_ref, bq_ref, r2_ref, aw_ref, cw_ref, u_ref, v_ref, o_ref, at_ref)
    even = e % 2 == 0

    @pl.when(e == 0)
    def _():
        o_ref[...] = jnp.zeros_like(o_ref)
        step(ht0_ref, None, None)

    @pl.when(jnp.logical_not(even))
    def _():
        step(ht1_ref, ht0_ref, 0)

    @pl.when(even & (e > 0) & (e < last))
    def _():
        step(ht0_ref, ht1_ref, keys_per_tile)

    @pl.when(e == last)
    def _():
        step(None, ht1_ref, keys_per_tile)


def peer_experts(u, bq, r2, aw, cw, exp_u, exp_v, layer):
    tm, te = PEER_TM, PEER_TE
    n_tok = u.shape[0]
    assert PEER_N_TILES % 2 == 0 and n_tok % tm == 0
    tiles_per_row_block = 8 * PEER_N_KEYS // te
    once = pl.Buffered(1)
    col_spec = pl.BlockSpec((PEER_HEADS * PEER_N_KEYS, tm), lambda i, e: (0, i))
    row_spec = pl.BlockSpec((1, PEER_HEADS, 8, tm),
                            lambda i, e: (jnp.maximum(e - 1, 0) // tiles_per_row_block, 0, 0, i))
    return pl.pallas_call(
        _peer_kernel,
        out_shape=jax.ShapeDtypeStruct((n_tok, D_MODEL), F32),
        grid=(n_tok // tm, PEER_N_TILES + 1),
        in_specs=[pl.BlockSpec((tm, D_MODEL), lambda i, e: (i, 0)),
                  col_spec, col_spec,
                  row_spec, row_spec,
                  pl.BlockSpec((None, te, D_MODEL), lambda i, e: (layer, jnp.minimum(e, PEER_N_TILES - 1), 0)),
                  pl.BlockSpec((None, te, D_MODEL), lambda i, e: (layer, jnp.maximum(e - 1, 0), 0))],
        out_specs=pl.BlockSpec((tm, D_MODEL), lambda i, e: (i, 0), pipeline_mode=once),
        scratch_shapes=[pltpu.VMEM((te, tm), F32), pltpu.VMEM((te, tm), F32), pltpu.VMEM((te, tm), BF16)],
        compiler_params=_params(("parallel", "arbitrary"), 56),
        name="peer_experts",
    )(u, bq, r2, aw, cw, exp_u, exp_v)


def peer_ffn(u, w_q, sub_keys, exp_u, exp_v, layer):
    q = matmul_resident(u, w_q, BF16, "peer_query", layer=layer)
    bq, r2, aw, cw = peer_route(q, sub_keys, layer)
    return peer_experts(u, bq, r2, aw, cw, exp_u, exp_v, layer)


def mla_mixer(u, cache_ckv, cache_krope, w_dq, q_norm, w_uq, w_dkv, kv_norm, w_uk, w_uv, w_o):
    pad = (-(MLA_Q_LORA + MLA_KV_LORA + MLA_ROPE)) % 128
    w_cat = jnp.concatenate([w_dq, w_dkv, jnp.zeros((D_MODEL, pad), F32)], axis=1)
    cq, ckv, kr = mla_down(u, w_cat, q_norm, kv_norm)
    w_uq3 = w_uq.reshape(MLA_Q_LORA, MLA_HEADS, MLA_NOPE + MLA_ROPE)
    w_q_cat = jnp.concatenate([w_uq3[:, :, :MLA_NOPE].reshape(MLA_Q_LORA, -1),
                               w_uq3[:, :, MLA_NOPE:].reshape(MLA_Q_LORA, -1)], axis=1)
    q = matmul_resident(cq, w_q_cat, BF16, "mla_q_up")
    w_nope = MLA_HEADS * MLA_NOPE
    ckv_src = jnp.concatenate([ckv, cache_ckv.reshape(-1, MLA_KV_LORA)], axis=0).astype(BF16)
    kv_up = matmul_resident(ckv_src, jnp.concatenate([w_uk, w_uv], axis=1), BF16, "mla_kv_up")
    cos, sin = (jnp.asarray(t, F32) for t in _rope_tables())
    w_rope = MLA_HEADS * MLA_ROPE
    qr_rot = rope_rows(q, N_PROMPT // 256, w_nope // w_rope, w_rope, N_SAMPLE, cos, sin, BF16)
    cos2, sin2 = (t.reshape(DEC_SEQ // 2, 128) for t in (cos[:, :MLA_ROPE], sin[:, :MLA_ROPE]))
    kr_pairs = kr[N_PROMPT:].reshape(N_SAMPLE // 2, 2 * MLA_ROPE)
    kr_rot = rope_rows(kr_pairs, 0, 0, 2 * MLA_ROPE, N_SAMPLE // 2, cos2, sin2, BF16).reshape(N_SAMPLE, MLA_ROPE)
    o_p = mla_attention_prompt(q, kv_up, kr)
    o_s = mla_attention_sample(q, qr_rot, kv_up, kr_rot, cache_krope.reshape(-1, MLA_ROPE))
    op = matmul_resident(o_p, w_o, BF16, "mla_out", x_sample=o_s)
    return op, ckv, kr


def gqa_mixer(u, cache_k, cache_v, w_qkv, sink, w_o):
    nq = GQA_HEADS * GQA_HEAD_DIM
    nk = GQA_KV_HEADS * GQA_HEAD_DIM
    q = matmul_resident(u, w_qkv, BF16, "gqa_q", n=nq)
    kv = matmul_resident(u, w_qkv, F32, "gqa_kv", col0=nq)
    cos, sin = (jnp.asarray(t, F32) for t in _rope_tables())
    q_rot = rope_rows(q, N_PROMPT // 256, 0, nq, N_SAMPLE, cos, sin, BF16)
    k_rot = rope_rows(kv, N_PROMPT // 256, 0, nk, N_SAMPLE, cos, sin, BF16)
    o_p = gqa_attention_prompt(q, kv, sink)
    o_s = gqa_attention_sample(q_rot, k_rot, kv, cache_k.reshape(-1, nk), cache_v.reshape(-1, nk), sink)
    op = matmul_resident(o_p, w_o, BF16, "gqa_out", x_sample=o_s)
    return op, kv


def fnet_mixer(u, w_out):
    f_p = fourier_real_2d(u, 0, BATCH, SEQ)
    f_s = fourier_real_2d(u, N_PROMPT // DEC_SEQ, DEC_BATCH, DEC_SEQ)
    return matmul_resident(f_p, w_out, BF16, "fnet_out", x_sample=f_s)


def conv_mixer(u, w_in, conv_w, conv_b, w_out):
    h3 = matmul(u, w_in, BF16, name="conv_in")
    y_p = gated_conv(h3, conv_w, conv_b, 0, BATCH, SEQ)
    y_s = gated_conv(h3, conv_w, conv_b, N_PROMPT // DEC_SEQ, DEC_BATCH, DEC_SEQ)
    return matmul_resident(y_p, w_out, BF16, "conv_out", x_sample=y_s)


def kernel(x_prompt, x_sample, cache_l0_ckv, cache_l0_krope, cache_l1_k, cache_l1_v, c, c_ctx, ada_w, ada_b, ln1_g, ln1_b, ln2_g, ln2_b, mla_w_dq, mla_q_norm, mla_w_uq, mla_w_dkv, mla_kv_norm, mla_w_uk, mla_w_uv, mla_w_o, gqa_w_qkv, gqa_sink, gqa_w_o, fnet_w_out, conv_w_in, conv_w, conv_b, conv_w_out, peer_w_q, peer_sub_keys, peer_u, peer_v):
    x = jnp.concatenate([x_prompt.reshape(N_PROMPT, D_MODEL), x_sample.reshape(N_SAMPLE, D_MODEL)], axis=0)
    cond = jnp.concatenate([c_ctx[None, :], c, jnp.zeros((N_COND - 1 - DEC_BATCH, D_MODEL), F32)], axis=0)
    mods_all = ada_modulation_all(cond, ada_w, ada_b).reshape(DEPTH, N_COND, 6, 1, D_MODEL)

    u = modulate_tokens(x, mods_all[0])
    new_ckv = new_krope = new_kv = None
    for i in range(DEPTH):
        mods = mods_all[i]
        if i == 0:
            delta, new_ckv, new_krope = mla_mixer(u, cache_l0_ckv, cache_l0_krope, mla_w_dq, mla_q_norm, mla_w_uq,
                                                  mla_w_dkv, mla_kv_norm, mla_w_uk, mla_w_uv, mla_w_o)
        elif i == 1:
            delta, new_kv = gqa_mixer(u, cache_l1_k, cache_l1_v, gqa_w_qkv, gqa_sink, gqa_w_o)
        elif i == 2:
            delta = fnet_mixer(u, fnet_w_out)
        else:
            delta = conv_mixer(u, conv_w_in, conv_w, conv_b, conv_w_out)
        x, u = post_norm(x, delta, mods, 2, ln1_g[i], ln1_b[i], mods, 3)
        delta = peer_ffn(u, peer_w_q, peer_sub_keys, peer_u, peer_v, i)
        if i + 1 < DEPTH:
            x, u = post_norm(x, delta, mods, 5, ln2_g[i], ln2_b[i], mods_all[i + 1], 0)
        else:
            y_prompt = post_norm(x, delta, mods, 5, ln2_g[i], ln2_b[i], rows=(0, N_PROMPT))
            y_sample = post_norm(x, delta, mods, 5, ln2_g[i], ln2_b[i], rows=(N_PROMPT, N_SAMPLE))

    nk = GQA_KV_HEADS * GQA_HEAD_DIM
    return (y_prompt.reshape(BATCH, SEQ, D_MODEL),
            y_sample.reshape(DEC_BATCH, DEC_SEQ, D_MODEL),
            new_ckv[:N_PROMPT].reshape(BATCH, SEQ, MLA_KV_LORA),
            new_krope[:N_PROMPT].reshape(BATCH, SEQ, MLA_ROPE),
            new_kv[:N_PROMPT, :nk].reshape(BATCH, SEQ, GQA_KV_HEADS, GQA_HEAD_DIM),
            new_kv[:N_PROMPT, nk:].reshape(BATCH, SEQ, GQA_KV_HEADS, GQA_HEAD_DIM))
```

```python
import functools
import math

import numpy as np
import jax
import jax.numpy as jnp
from jax import lax
from jax.experimental import pallas as pl
from jax.experimental.pallas import tpu as pltpu

F32 = jnp.float32
BF16 = jnp.bfloat16

D_MODEL = 2048
BATCH = 16
SEQ = 256
DEPTH = 4
DEC_BATCH = 2
DEC_SEQ = 1024
PAST_LEN = 256
GRID_W = 64
ROPE_BASE = 10000.0
LN_EPS = 1e-5
RMS_EPS = 1e-6
NEG_INF = -1e30
DEEPNORM_ALPHA = (2 * DEPTH) ** 0.25

MLA_HEADS = 16
MLA_Q_LORA = 512
MLA_KV_LORA = 512
MLA_NOPE = 128
MLA_ROPE = 64
MLA_V = 128
MLA_SCALE = (MLA_NOPE + MLA_ROPE) ** -0.5

GQA_HEADS = 32
GQA_KV_HEADS = 8
GQA_HEAD_DIM = 64
WINDOW = 128
GQA_SCALE = GQA_HEAD_DIM ** -0.5

FNET_GROUPS = 4
FNET_GROUP_DIM = D_MODEL // FNET_GROUPS

PEER_HEADS = 8
PEER_N_KEYS = 128
PEER_N_EXPERTS = PEER_N_KEYS * PEER_N_KEYS
PEER_HALF = 128
PEER_TOPK = 16

N_PROMPT = BATCH * SEQ
N_SAMPLE = DEC_BATCH * DEC_SEQ
N_TOK = N_PROMPT + N_SAMPLE
N_COND = 8

MIB = 1024 * 1024

NT_DIMS = (((1,), (1,)), ((), ()))
TN_DIMS = (((0,), (0,)), ((), ()))


def _params(sem, vmem_mib, flags=None):
    return pltpu.CompilerParams(dimension_semantics=sem, vmem_limit_bytes=vmem_mib * MIB, flags=flags)


def _group_of_tile(i, tm):
    return jnp.maximum((i * tm) // DEC_SEQ - (N_PROMPT // DEC_SEQ - 1), 0)


def _mod_spec(tm, which):
    return pl.BlockSpec((None, None, 1, D_MODEL), lambda i: (_group_of_tile(i, tm), which, 0, 0))


def _ada_kernel(c_ref, w_ref, b_ref, o_ref):
    c = c_ref[...]
    s = (c * jax.nn.sigmoid(c)).astype(BF16)
    o_ref[...] = jnp.dot(s, w_ref[...].astype(BF16), preferred_element_type=F32) + b_ref[...]


def ada_modulation_all(cond, ada_w, ada_b):
    tn = 1024
    n_out = 6 * D_MODEL
    return pl.pallas_call(
        _ada_kernel,
        out_shape=jax.ShapeDtypeStruct((DEPTH, N_COND, n_out), F32),
        grid=(DEPTH, n_out // tn),
        in_specs=[
            pl.BlockSpec((N_COND, D_MODEL), lambda l, j: (0, 0)),
            pl.BlockSpec((None, D_MODEL, tn), lambda l, j: (l, 0, j)),
            pl.BlockSpec((None, 1, tn), lambda l, j: (l, 0, j)),
        ],
        out_specs=pl.BlockSpec((None, N_COND, tn), lambda l, j: (l, 0, j)),
        compiler_params=_params(("parallel", "parallel"), 40),
        name="ada_modulation",
    )(cond, ada_w, ada_b.reshape(DEPTH, 1, n_out))


def _modulate_kernel(x_ref, sh_ref, sc_ref, u_ref):
    u_ref[...] = (x_ref[...] * (1.0 + sc_ref[...]) + sh_ref[...]).astype(u_ref.dtype)


def modulate_tokens(x, mods):
    tm = 256
    return pl.pallas_call(
        _modulate_kernel,
        out_shape=jax.ShapeDtypeStruct((N_TOK, D_MODEL), BF16),
        grid=(N_TOK // tm,),
        in_specs=[pl.BlockSpec((tm, D_MODEL), lambda i: (i, 0)), _mod_spec(tm, 0), _mod_spec(tm, 1)],
        out_specs=pl.BlockSpec((tm, D_MODEL), lambda i: (i, 0)),
        compiler_params=_params(("parallel",), 24),
        name="modulate",
    )(x, mods, mods)


def _layer_norm_rows(y, g, b):
    mu = jnp.mean(y, axis=-1, keepdims=True)
    yc = y - mu
    var = jnp.mean(yc * yc, axis=-1, keepdims=True)
    return yc * lax.rsqrt(var + LN_EPS) * g + b


def _post_norm_kernel(x_ref, d_ref, gate_ref, g_ref, b_ref, xo_ref):
    y = DEEPNORM_ALPHA * x_ref[...] + gate_ref[...] * d_ref[...].astype(F32)
    xo_ref[...] = _layer_norm_rows(y, g_ref[...], b_ref[...])


def _post_norm_mod_kernel(x_ref, d_ref, gate_ref, g_ref, b_ref, sh_ref, sc_ref, xo_ref, uo_ref):
    y = DEEPNORM_ALPHA * x_ref[...] + gate_ref[...] * d_ref[...].astype(F32)
    xn = _layer_norm_rows(y, g_ref[...], b_ref[...])
    xo_ref[...] = xn
    uo_ref[...] = (xn * (1.0 + sc_ref[...]) + sh_ref[...]).astype(uo_ref.dtype)


def post_norm(x, delta, mods, gate_idx, g, b, next_mods=None, next_idx=None, rows=(0, N_TOK)):
    tm = 256
    row = pl.BlockSpec((tm, D_MODEL), lambda i: (i, 0))
    vec = pl.BlockSpec((1, D_MODEL), lambda i: (0, 0))
    ins = [x, delta, mods, g.reshape(1, D_MODEL), b.reshape(1, D_MODEL)]
    specs = [row, row, _mod_spec(tm, gate_idx), vec, vec]
    if next_mods is None:
        i0 = rows[0] // tm
        row_in = pl.BlockSpec((tm, D_MODEL), lambda i: (i0 + i, 0))
        gate = pl.BlockSpec((None, None, 1, D_MODEL), lambda i: (_group_of_tile(i0 + i, tm), gate_idx, 0, 0))
        return pl.pallas_call(
            _post_norm_kernel,
            out_shape=jax.ShapeDtypeStruct((rows[1], D_MODEL), F32),
            grid=(rows[1] // tm,), in_specs=[row_in, row_in, gate, vec, vec], out_specs=row,
            compiler_params=_params(("parallel",), 32), name="post_norm",
        )(*ins)
    ins += [next_mods, next_mods]
    specs += [_mod_spec(tm, next_idx), _mod_spec(tm, next_idx + 1)]
    return pl.pallas_call(
        _post_norm_mod_kernel,
        out_shape=(jax.ShapeDtypeStruct((N_TOK, D_MODEL), F32), jax.ShapeDtypeStruct((N_TOK, D_MODEL), BF16)),
        grid=(N_TOK // tm,), in_specs=specs, out_specs=(row, row),
        compiler_params=_params(("parallel",), 32), name="post_norm_mod",
    )(*ins)


def _mm_kernel(x_ref, w_ref, o_ref):
    o_ref[...] = jnp.dot(x_ref[...], w_ref[...].astype(BF16), preferred_element_type=F32).astype(o_ref.dtype)


def _mm_resident_kernel(*refs, prompt_tiles):
    if prompt_tiles is None:
        x_ref, w_ref, o_ref, wb_ref = refs
    else:
        x_ref, xs_ref, w_ref, o_ref, wb_ref = refs
    i = pl.program_id(0)

    @pl.when(i == 0)
    def _():
        wb_ref[...] = w_ref[...].astype(BF16)

    if prompt_tiles is None:
        o_ref[...] = jnp.dot(x_ref[...], wb_ref[...], preferred_element_type=F32).astype(o_ref.dtype)
    else:
        @pl.when(i < prompt_tiles)
        def _():
            o_ref[...] = jnp.dot(x_ref[...], wb_ref[...], preferred_element_type=F32).astype(o_ref.dtype)

        @pl.when(i >= prompt_tiles)
        def _():
            o_ref[...] = jnp.dot(xs_ref[...], wb_ref[...], preferred_element_type=F32).astype(o_ref.dtype)


def matmul_resident(x, w, out_dtype, name, x_sample=None, layer=None, col0=0, n=None):
    tm = 512
    k = x.shape[1]
    n = w.shape[-1] - col0 if n is None else n
    assert col0 % n == 0
    j0 = col0 // n
    rows = x.shape[0] + (0 if x_sample is None else x_sample.shape[0])
    once = pl.Buffered(1)
    if layer is None:
        w_spec = pl.BlockSpec((k, n), lambda i: (0, j0), pipeline_mode=once)
    else:
        w_spec = pl.BlockSpec((None, k, n), lambda i: (layer, 0, j0), pipeline_mode=once)
    if x_sample is None:
        prompt_tiles = None
        xs, x_specs = (x,), [pl.BlockSpec((tm, k), lambda i: (i, 0))]
    else:
        prompt_tiles = x.shape[0] // tm
        xs = (x, x_sample)
        x_specs = [pl.BlockSpec((tm, k), lambda i: (jnp.minimum(i, prompt_tiles - 1), 0)),
                   pl.BlockSpec((tm, k), lambda i: (jnp.maximum(i - prompt_tiles, 0), 0))]
    return pl.pallas_call(
        functools.partial(_mm_resident_kernel, prompt_tiles=prompt_tiles),
        out_shape=jax.ShapeDtypeStruct((rows, n), out_dtype),
        grid=(rows // tm,),
        in_specs=x_specs + [w_spec],
        out_specs=pl.BlockSpec((tm, n), lambda i: (i, 0)),
        scratch_shapes=[pltpu.VMEM((k, n), BF16)],
        compiler_params=_params(("arbitrary",), 48),
        name=name,
    )(*xs, w)


def matmul(x, w, out_dtype, tm=2048, tn=512, name="matmul", layer=None, col0=0, n=None):
    m, k = x.shape
    n = w.shape[-1] - col0 if n is None else n
    tm = min(tm, m)
    tn = min(tn, n)
    assert m % tm == 0 and n % tn == 0 and col0 % tn == 0, (m, n, tm, tn, col0)
    j0 = col0 // tn
    if layer is None:
        w_spec = pl.BlockSpec((k, tn), lambda i, j: (0, j0 + j))
    else:
        w_spec = pl.BlockSpec((None, k, tn), lambda i, j: (layer, 0, j0 + j))
    return pl.pallas_call(
        _mm_kernel,
        out_shape=jax.ShapeDtypeStruct((m, n), out_dtype),
        grid=(m // tm, n // tn),
        in_specs=[pl.BlockSpec((tm, k), lambda i, j: (i, 0)), w_spec],
        out_specs=pl.BlockSpec((tm, tn), lambda i, j: (i, j)),
        compiler_params=_params(("parallel", "parallel"), 48),
        name=name,
    )(x, w)


def _rms_rows(y, g):
    return y * lax.rsqrt(jnp.mean(y * y, axis=-1, keepdims=True) + RMS_EPS) * g


def _mla_down_kernel(x_ref, w_ref, qn_ref, kvn_ref, cq_ref, ckv_ref, kr_ref):
    y = jnp.dot(x_ref[...], w_ref[...].astype(BF16), preferred_element_type=F32)
    cq_ref[...] = _rms_rows(y[:, :MLA_Q_LORA], qn_ref[...]).astype(cq_ref.dtype)
    ckv_ref[...] = _rms_rows(y[:, MLA_Q_LORA:MLA_Q_LORA + MLA_KV_LORA], kvn_ref[...])
    kr_ref[...] = y[:, MLA_Q_LORA + MLA_KV_LORA:MLA_Q_LORA + MLA_KV_LORA + MLA_ROPE]


def mla_down(u, w_cat, q_norm, kv_norm):
    tm = 512
    n = w_cat.shape[1]
    return pl.pallas_call(
        _mla_down_kernel,
        out_shape=(jax.ShapeDtypeStruct((N_TOK, MLA_Q_LORA), BF16),
                   jax.ShapeDtypeStruct((N_TOK, MLA_KV_LORA), F32),
                   jax.ShapeDtypeStruct((N_TOK, MLA_ROPE), F32)),
        grid=(N_TOK // tm,),
        in_specs=[pl.BlockSpec((tm, D_MODEL), lambda i: (i, 0)),
                  pl.BlockSpec((D_MODEL, n), lambda i: (0, 0)),
                  pl.BlockSpec((1, MLA_Q_LORA), lambda i: (0, 0)),
                  pl.BlockSpec((1, MLA_KV_LORA), lambda i: (0, 0))],
        out_specs=(pl.BlockSpec((tm, MLA_Q_LORA), lambda i: (i, 0)),
                   pl.BlockSpec((tm, MLA_KV_LORA), lambda i: (i, 0)),
                   pl.BlockSpec((tm, MLA_ROPE), lambda i: (i, 0))),
        compiler_params=_params(("parallel",), 48),
        name="mla_down",
    )(u, w_cat, q_norm.reshape(1, -1), kv_norm.reshape(1, -1))


def _rope_tables():
    t = np.arange(DEC_SEQ)
    quarter = MLA_ROPE // 4
    inv_freq = ROPE_BASE ** (-np.arange(quarter, dtype=np.float64) / quarter)
    ang_row = (t // GRID_W)[:, None] * inv_freq[None, :]
    ang_col = (t % GRID_W)[:, None] * inv_freq[None, :]
    cos = np.concatenate([np.cos(ang_row)] * 2 + [np.cos(ang_col)] * 2, -1)
    sin = np.concatenate([-np.sin(ang_row), np.sin(ang_row), -np.sin(ang_col), np.sin(ang_col)], -1)
    return np.tile(cos, (1, 2)), np.tile(sin, (1, 2))


def _rope_kernel(x_ref, c_ref, s_ref, o_ref):
    x = x_ref[...].astype(F32)
    w = x.shape[-1]
    reps = w // c_ref.shape[-1]
    cos = jnp.tile(c_ref[...], (1, reps))
    sin = jnp.tile(s_ref[...], (1, reps))
    lane = lax.broadcasted_iota(jnp.int32, x.shape, 1)
    partner = jnp.where((lane % 32) < 16, pltpu.roll(x, w - 16, 1), pltpu.roll(x, 16, 1))
    o_ref[...] = (x * cos + partner * sin).astype(o_ref.dtype)


def rope_rows(x, row_block0, col_block, width, n_rows, cos, sin, out_dtype, tr=256):
    t_blocks = cos.shape[0] // tr
    return pl.pallas_call(
        _rope_kernel,
        out_shape=jax.ShapeDtypeStruct((n_rows, width), out_dtype),
        grid=(n_rows // tr,),
        in_specs=[pl.BlockSpec((tr, width), lambda i: (row_block0 + i, col_block)),
                  pl.BlockSpec((tr, 128), lambda i: (i % t_blocks, 0)),
                  pl.BlockSpec((tr, 128), lambda i: (i % t_blocks, 0))],
        out_specs=pl.BlockSpec((tr, width), lambda i: (i, 0)),
        compiler_params=_params(("parallel",), 32),
        name="axial_rope",
    )(x, cos, sin)


def _softmax_parts(parts, sink):
    m = parts[0].max(axis=-1, keepdims=True)
    for s in parts[1:]:
        m = jnp.maximum(m, s.max(axis=-1, keepdims=True))
    if sink is not None:
        m = jnp.maximum(m, sink)
    ps = [jnp.exp(s - m) for s in parts]
    l = ps[0].sum(axis=-1, keepdims=True)
    for p in ps[1:]:
        l = l + p.sum(axis=-1, keepdims=True)
    if sink is not None:
        l = l + jnp.exp(sink - m)
    return ps, 1.0 / l


def _mla_attn_kernel(*refs, with_ctx):
    if with_ctx:
        qn_ref, qr_ref, kn_ref, v_ref, kr_ref, kn2_ref, v2_ref, kr2_ref, o_ref = refs
    else:
        qn_ref, qr_ref, kn_ref, v_ref, kr_ref, o_ref = refs
    kr = kr_ref[...].astype(BF16)
    kr2 = kr2_ref[...].astype(BF16) if with_ctx else None
    for h in range(MLA_HEADS):
        n0, n1 = h * MLA_NOPE, (h + 1) * MLA_NOPE
        qn = qn_ref[:, n0:n1]
        qr = qr_ref[:, h * MLA_ROPE:(h + 1) * MLA_ROPE]
        s = (lax.dot_general(qn, kn_ref[:, n0:n1], NT_DIMS, preferred_element_type=F32)
             + lax.dot_general(qr, kr, NT_DIMS, preferred_element_type=F32)) * MLA_SCALE
        parts = [s]
        if with_ctx:
            s2 = (lax.dot_general(qn, kn2_ref[:, n0:n1], NT_DIMS, preferred_element_type=F32)
                  + lax.dot_general(qr, kr2, NT_DIMS, preferred_element_type=F32)) * MLA_SCALE
            parts.append(s2)
        ps, inv_l = _softmax_parts(parts, None)
        o = jnp.dot(ps[0].astype(BF16), v_ref[:, n0:n1], preferred_element_type=F32)
        if with_ctx:
            o = o + jnp.dot(ps[1].astype(BF16), v2_ref[:, n0:n1], preferred_element_type=F32)
        o_ref[:, n0:n1] = (o * inv_l).astype(o_ref.dtype)


def mla_attention_prompt(q, kv_up, kr):
    w = MLA_HEADS * MLA_NOPE
    return pl.pallas_call(
        functools.partial(_mla_attn_kernel, with_ctx=False),
        out_shape=jax.ShapeDtypeStruct((N_PROMPT, w), BF16),
        grid=(BATCH,),
        in_specs=[pl.BlockSpec((SEQ, w), lambda b: (b, 0)),
                  pl.BlockSpec((SEQ, MLA_HEADS * MLA_ROPE), lambda b: (b, w // (MLA_HEADS * MLA_ROPE))),
                  pl.BlockSpec((SEQ, w), lambda b: (b, 0)),
                  pl.BlockSpec((SEQ, w), lambda b: (b, 1)),
                  pl.BlockSpec((SEQ, MLA_ROPE), lambda b: (b, 0))],
        out_specs=pl.BlockSpec((SEQ, w), lambda b: (b, 0)),
        compiler_params=_params(("parallel",), 32),
        name="mla_attention_prompt",
    )(q, q, kv_up, kv_up, kr)


def mla_attention_sample(qn, qr_rot, kv_up, kr_rot, kr_cache):
    w = MLA_HEADS * MLA_NOPE
    tq = 256
    nq = DEC_SEQ // tq
    q0 = N_PROMPT // tq
    lat0 = N_PROMPT // DEC_SEQ
    ctx0 = N_TOK // PAST_LEN
    return pl.pallas_call(
        functools.partial(_mla_attn_kernel, with_ctx=True),
        out_shape=jax.ShapeDtypeStruct((N_SAMPLE, w), BF16),
        grid=(DEC_BATCH, nq),
        in_specs=[pl.BlockSpec((tq, w), lambda b, i: (q0 + b * nq + i, 0)),
                  pl.BlockSpec((tq, MLA_HEADS * MLA_ROPE), lambda b, i: (b * nq + i, 0)),
                  pl.BlockSpec((DEC_SEQ, w), lambda b, i: (lat0 + b, 0)),
                  pl.BlockSpec((DEC_SEQ, w), lambda b, i: (lat0 + b, 1)),
                  pl.BlockSpec((DEC_SEQ, MLA_ROPE), lambda b, i: (b, 0)),
                  pl.BlockSpec((PAST_LEN, w), lambda b, i: (ctx0 + b, 0)),
                  pl.BlockSpec((PAST_LEN, w), lambda b, i: (ctx0 + b, 1)),
                  pl.BlockSpec((PAST_LEN, MLA_ROPE), lambda b, i: (b, 0))],
        out_specs=pl.BlockSpec((tq, w), lambda b, i: (b * nq + i, 0)),
        compiler_params=_params(("parallel", "parallel"), 56),
        name="mla_attention_sample",
    )(qn, qr_rot, kv_up, kv_up, kr_rot, kv_up, kv_up, kr_cache)


def _gqa_attn_kernel(*refs, with_ctx, tq):
    if with_ctx:
        sink_ref, q_ref, k_ref, v_ref, k2_ref, v2_ref, o_ref = refs
    else:
        sink_ref, q_ref, k_ref, v_ref, o_ref = refs
    group = GQA_HEADS // GQA_KV_HEADS
    d = GQA_HEAD_DIM
    if with_ctx:
        t = pl.program_id(1) * tq + lax.broadcasted_iota(jnp.int32, (tq, DEC_SEQ), 0)
        s_pos = lax.broadcasted_iota(jnp.int32, (tq, DEC_SEQ), 1)
        in_window = jnp.abs(t - s_pos) <= WINDOW
    for hk in range(GQA_KV_HEADS):
        k = k_ref[:, hk * d:(hk + 1) * d].astype(BF16)
        v = v_ref[:, hk * d:(hk + 1) * d].astype(BF16)
        if with_ctx:
            k2 = k2_ref[:, hk * d:(hk + 1) * d].astype(BF16)
            v2 = v2_ref[:, hk * d:(hk + 1) * d].astype(BF16)
        for g in range(group):
            h = hk * group + g
            q = q_ref[:, h * d:(h + 1) * d]
            s = lax.dot_general(q, k, NT_DIMS, preferred_element_type=F32) * GQA_SCALE
            parts = [s]
            if with_ctx:
                parts = [jnp.where(in_window, s, NEG_INF),
                         lax.dot_general(q, k2, NT_DIMS, preferred_element_type=F32) * GQA_SCALE]
            ps, inv_l = _softmax_parts(parts, sink_ref[h])
            o = jnp.dot(ps[0].astype(BF16), v, preferred_element_type=F32)
            if with_ctx:
                o = o + jnp.dot(ps[1].astype(BF16), v2, preferred_element_type=F32)
            o_ref[:, h * d:(h + 1) * d] = (o * inv_l).astype(o_ref.dtype)


def gqa_attention_prompt(q, kv, sink):
    wq = GQA_HEADS * GQA_HEAD_DIM
    wk = GQA_KV_HEADS * GQA_HEAD_DIM
    return pl.pallas_call(
        functools.partial(_gqa_attn_kernel, with_ctx=False, tq=SEQ),
        out_shape=jax.ShapeDtypeStruct((N_PROMPT, wq), BF16),
        grid=(BATCH,),
        in_specs=[pl.BlockSpec(memory_space=pltpu.SMEM),
                  pl.BlockSpec((SEQ, wq), lambda b: (b, 0)),
                  pl.BlockSpec((SEQ, wk), lambda b: (b, 0)),
                  pl.BlockSpec((SEQ, wk), lambda b: (b, 1))],
        out_specs=pl.BlockSpec((SEQ, wq), lambda b: (b, 0)),
        compiler_params=_params(("parallel",), 32),
        name="gqa_attention_prompt",
    )(sink, q, kv, kv)


def gqa_attention_sample(q_rot, k_rot, kv, k_cache, v_cache, sink):
    wq = GQA_HEADS * GQA_HEAD_DIM
    wk = GQA_KV_HEADS * GQA_HEAD_DIM
    tq = 256
    nq = DEC_SEQ // tq
    lat0 = N_PROMPT // DEC_SEQ
    return pl.pallas_call(
        functools.partial(_gqa_attn_kernel, with_ctx=True, tq=tq),
        out_shape=jax.ShapeDtypeStruct((N_SAMPLE, wq), BF16),
        grid=(DEC_BATCH, nq),
        in_specs=[pl.BlockSpec(memory_space=pltpu.SMEM),
                  pl.BlockSpec((tq, wq), lambda b, i: (b * nq + i, 0)),
                  pl.BlockSpec((DEC_SEQ, wk), lambda b, i: (b, 0)),
                  pl.BlockSpec((DEC_SEQ, wk), lambda b, i: (lat0 + b, 1)),
                  pl.BlockSpec((PAST_LEN, wk), lambda b, i: (b, 0)),
                  pl.BlockSpec((PAST_LEN, wk), lambda b, i: (b, 0))],
        out_specs=pl.BlockSpec((tq, wq), lambda b, i: (b * nq + i, 0)),
        compiler_params=_params(("parallel", "parallel"), 56),
        name="gqa_attention_sample",
    )(sink, q_rot, k_rot, kv, k_cache, v_cache)


def _dft_tables(n):
    jk = (np.arange(n)[:, None] * np.arange(n)[None, :]) % n
    ang = 2.0 * np.pi * jk / n
    return np.cos(ang) / math.sqrt(n), np.sin(ang) / math.sqrt(n)


def _fnet_kernel(x_ref, ct_ref, st_ref, cc_ref, sc_ref, o_ref):
    x = x_ref[...]
    y1 = jnp.dot(x, cc_ref[...], preferred_element_type=F32).astype(BF16)
    y2 = jnp.dot(x, sc_ref[...], preferred_element_type=F32).astype(BF16)
    o = (jnp.dot(ct_ref[...], y1, preferred_element_type=F32)
         - jnp.dot(st_ref[...], y2, preferred_element_type=F32))
    o_ref[...] = o.astype(o_ref.dtype)


def fourier_real_2d(u, seq_block0, n_seq, t):
    ct, st = (jnp.asarray(a, BF16) for a in _dft_tables(t))
    cc, sc = (jnp.asarray(a, BF16) for a in _dft_tables(FNET_GROUP_DIM))
    gd = FNET_GROUP_DIM
    return pl.pallas_call(
        _fnet_kernel,
        out_shape=jax.ShapeDtypeStruct((n_seq * t, D_MODEL), BF16),
        grid=(n_seq, FNET_GROUPS),
        in_specs=[pl.BlockSpec((t, gd), lambda b, g: (seq_block0 + b, g)),
                  pl.BlockSpec((t, t), lambda b, g: (0, 0)),
                  pl.BlockSpec((t, t), lambda b, g: (0, 0)),
                  pl.BlockSpec((gd, gd), lambda b, g: (0, 0)),
                  pl.BlockSpec((gd, gd), lambda b, g: (0, 0))],
        out_specs=pl.BlockSpec((t, gd), lambda b, g: (b, g)),
        compiler_params=_params(("parallel", "parallel"), 32),
        name="fourier_mix",
    )(u, ct, st, cc, sc)


def _conv_kernel(b_ref, c_ref, h_ref, w_ref, cb_ref, o_ref):
    z = c_ref[...].astype(F32) * h_ref[...].astype(F32)
    t = z.shape[0]
    row = lax.broadcasted_iota(jnp.int32, z.shape, 0)
    z_prev = jnp.where(row == 0, 0.0, pltpu.roll(z, 1, 0))
    z_next = jnp.where(row == t - 1, 0.0, pltpu.roll(z, t - 1, 0))
    conv = z_prev * w_ref[0:1, :] + z * w_ref[1:2, :] + z_next * w_ref[2:3, :] + cb_ref[...]
    o_ref[...] = (b_ref[...].astype(F32) * conv).astype(o_ref.dtype)


def gated_conv(h3, conv_w, conv_b, seq_block0, n_seq, t):
    tn = 512
    nj = D_MODEL // tn
    return pl.pallas_call(
        _conv_kernel,
        out_shape=jax.ShapeDtypeStruct((n_seq * t, D_MODEL), BF16),
        grid=(n_seq, nj),
        in_specs=[pl.BlockSpec((t, tn), lambda s, j: (seq_block0 + s, j)),
                  pl.BlockSpec((t, tn), lambda s, j: (seq_block0 + s, nj + j)),
                  pl.BlockSpec((t, tn), lambda s, j: (seq_block0 + s, 2 * nj + j)),
                  pl.BlockSpec((3, tn), lambda s, j: (0, j)),
                  pl.BlockSpec((1, tn), lambda s, j: (0, j))],
        out_specs=pl.BlockSpec((t, tn), lambda s, j: (s, j)),
        compiler_params=_params(("parallel", "parallel"), 32),
        name="gated_conv",
    )(h3, h3, h3, conv_w, conv_b.reshape(1, D_MODEL))


_N_RANKS = PEER_TOPK + 1
_CAND_PAIRS = [(a, b) for a in range(_N_RANKS) for b in range(_N_RANKS) if (a + 1) * (b + 1) <= _N_RANKS]


_NO_RANK = 127.0


def _top_values_exact(s, k, want_rank):
    n = s.shape[0]
    idx = lax.broadcasted_iota(jnp.int32, s.shape, 0)
    rank = jnp.full(s.shape, _NO_RANK, F32) if want_rank else None
    out = []
    for it in range(k):
        m = jnp.max(s, axis=0, keepdims=True)
        out.append(m)
        first = jnp.min(jnp.where(s == m, idx, n), axis=0, keepdims=True)
        hit = idx == first
        s = jnp.where(hit, -jnp.inf, s)
        if want_rank:
            rank = jnp.where(hit, float(it), rank)
    return out, rank


def _all_equal(x, value):
    return jnp.min(jnp.where(x == value, 1.0, 0.0)) > 0.5


def _top_values_distinct(s, k, n_pad):
    out = []
    for it in range(k):
        m = jnp.max(s, axis=0, keepdims=True)
        out.append(m)
        s = jnp.where(s == m, -jnp.inf, s)
    removed = jnp.sum(jnp.where(s == -jnp.inf, 1.0, 0.0), axis=0, keepdims=True)
    return out, _all_equal(removed, float(k + n_pad))


def _batcher_network(n):
    def merge(lo, hi, r):
        step = r * 2
        if step < hi - lo:
            yield from merge(lo, hi, step)
            yield from merge(lo + r, hi, step)
            yield from [(i, i + r) for i in range(lo + r, hi - r, step)]
        else:
            yield (lo, lo + r)

    def sort(lo, hi):
        if hi - lo >= 1:
            mid = lo + (hi - lo) // 2
            yield from sort(lo, mid)
            yield from sort(mid + 1, hi)
            yield from merge(lo, hi, 1)

    return list(sort(0, n - 1))


_SUBLANES = 8
_KEY_TILES = PEER_N_KEYS // _SUBLANES
_SORT_KEY_TILES = _batcher_network(_KEY_TILES)


def _top_values_sorted(s, k):
    tiles = [s[j * _SUBLANES:(j + 1) * _SUBLANES, :] for j in range(_KEY_TILES)]
    for i, j in _SORT_KEY_TILES:
        tiles[i], tiles[j] = jnp.maximum(tiles[i], tiles[j]), jnp.minimum(tiles[i], tiles[j])
    out = []
    pops = jnp.zeros_like(tiles[0])
    for it in range(k):
        m = jnp.max(tiles[0], axis=0, keepdims=True)
        out.append(m)
        hit = tiles[0] == m
        pops = pops + jnp.where(hit, 1.0, 0.0)
        depth = k - 1 - it
        for j in range(min(depth, _KEY_TILES - 1)):
            tiles[j] = jnp.where(hit, tiles[j + 1], tiles[j])
        if depth >= _KEY_TILES - 1:
            tiles[-1] = jnp.where(hit, -jnp.inf, tiles[-1])
    return out, _all_equal(jnp.sum(pops, axis=0, keepdims=True), float(k))


def _ranks_from_values(s, values):
    rank = jnp.full(s.shape, _NO_RANK, F32)
    for r in reversed(range(len(values))):
        rank = jnp.where(s >= values[r], float(r), rank)
    return rank


def _dup_bf16_words(x):
    bits = pltpu.bitcast(x.astype(BF16).astype(F32), jnp.int32)
    return bits | lax.shift_right_logical(bits, jnp.int32(16))


def _route_head(h, s1, s2, fast, bq_ref, r2_ref, aw_ref, cw_ref):
    tm = s1.shape[1]
    if fast:
        v1, ok1 = _top_values_sorted(s1, _N_RANKS)
        v2, ok2 = _top_values_sorted(s2, _N_RANKS)
        rank2 = _ranks_from_values(s2, v2)
    else:
        v1, _ = _top_values_exact(s1, _N_RANKS, False)
        v2, rank2 = _top_values_exact(s2, _N_RANKS, True)
    n_pad = (-len(_CAND_PAIRS)) % 8
    n_cand = len(_CAND_PAIRS) + n_pad
    cand_row = lax.broadcasted_iota(jnp.int32, (n_cand, tm), 0)
    cand = jnp.full((n_cand, tm), -jnp.inf, F32)
    for r, (a, b) in enumerate(_CAND_PAIRS):
        cand = jnp.where(cand_row == r, v1[a] + v2[b], cand)
    if fast:
        top, ok3 = _top_values_distinct(cand, _N_RANKS, n_pad)
    else:
        top, _ = _top_values_exact(cand, _N_RANKS, False)
    tau = 0.5 * (top[PEER_TOPK - 1] + top[PEER_TOPK])
    z = jnp.exp(top[0] - top[0])
    for kk in range(1, PEER_TOPK):
        z = z + jnp.exp(top[kk] - top[0])
    inv_z = 1.0 / z
    count = jnp.zeros_like(s1)
    for b in range(PEER_TOPK):
        count = jnp.where(s1 >= tau - v2[b], float(b + 1), count)
    rows = slice(h * PEER_N_KEYS, (h + 1) * PEER_N_KEYS)
    bq_ref[rows, :] = (jnp.exp(s2 - v2[0]) * inv_z).astype(BF16)
    r2_ref[rows, :] = rank2.astype(BF16)
    a_half = 0.5 * jnp.exp(s1 - v1[0])
    aw_ref[:, h, :, :] = _dup_bf16_words(a_half).reshape(PEER_N_KEYS // 8, 8, tm)
    cw_ref[:, h, :, :] = _dup_bf16_words(count).reshape(PEER_N_KEYS // 8, 8, tm)
    return jnp.logical_and(jnp.logical_and(ok1, ok2), ok3) if fast else None


def _route_kernel(q_ref, sk_ref, bq_ref, r2_ref, aw_ref, cw_ref):
    outs = (bq_ref, r2_ref, aw_ref, cw_ref)

    def scores(hc):
        return lax.dot_general(sk_ref[hc].astype(BF16), q_ref[:, hc * PEER_HALF:(hc + 1) * PEER_HALF],
                               NT_DIMS, preferred_element_type=F32)

    no_ties = None
    for h in range(PEER_HEADS):
        ok = _route_head(h, scores(2 * h), scores(2 * h + 1), True, *outs)
        no_ties = ok if no_ties is None else jnp.logical_and(no_ties, ok)

    @pl.when(jnp.logical_not(no_ties))
    def _():
        for h in range(PEER_HEADS):
            _route_head(h, scores(2 * h), scores(2 * h + 1), False, *outs)


def peer_route(q, sub_keys, layer):
    tm = 256
    n_tok = q.shape[0]
    sk = sub_keys.reshape(-1, PEER_HEADS * 2, PEER_N_KEYS, PEER_HALF)
    tiles = PEER_N_KEYS // 8
    col_out = jax.ShapeDtypeStruct((PEER_HEADS * PEER_N_KEYS, n_tok), BF16)
    col_spec = pl.BlockSpec((PEER_HEADS * PEER_N_KEYS, tm), lambda i: (0, i))
    row_out = jax.ShapeDtypeStruct((tiles, PEER_HEADS, 8, n_tok), jnp.int32)
    row_spec = pl.BlockSpec((tiles, PEER_HEADS, 8, tm), lambda i: (0, 0, 0, i))
    return pl.pallas_call(
        _route_kernel,
        out_shape=(col_out, col_out, row_out, row_out),
        grid=(n_tok // tm,),
        in_specs=[pl.BlockSpec((tm, PEER_HEADS * 2 * PEER_HALF), lambda i: (i, 0)),
                  pl.BlockSpec((None, PEER_HEADS * 2, PEER_N_KEYS, PEER_HALF), lambda i: (layer, 0, 0, 0))],
        out_specs=(col_spec, col_spec, row_spec, row_spec),
        compiler_params=_params(("parallel",), 32),
        name="peer_route",
    )(q, sk)


PEER_TE = 512
PEER_TM = 1024
PEER_CHUNK = 512
PEER_N_TILES = PEER_N_EXPERTS // PEER_TE


def _bf16_rows(word_row, n_rows):
    return pltpu.bitcast(jnp.broadcast_to(word_row, (n_rows // 2, word_row.shape[1])), BF16)


PEER_LANES = 256
PEER_ROWS = 128


def _peer_gated_slab(k, k0, bq_ref, r2_ref, aw_ref, cw_ref, ht_ref, at_ref):
    for j in range(ht_ref.shape[1] // PEER_LANES):
        lanes = slice(j * PEER_LANES, (j + 1) * PEER_LANES)
        for r in range(PEER_N_KEYS // PEER_ROWS):
            g = [None, None]
            for h in range(PEER_HEADS):
                count = _bf16_rows(cw_ref[0, h, k0 + k:k0 + k + 1, lanes], PEER_ROWS)
                a_half = _bf16_rows(aw_ref[0, h, k0 + k:k0 + k + 1, lanes], PEER_ROWS)
                rows = slice(h * PEER_N_KEYS + r * PEER_ROWS, h * PEER_N_KEYS + (r + 1) * PEER_ROWS)
                bq = bq_ref[rows, lanes]
                contrib = jnp.where(r2_ref[rows, lanes] < count, bq, jnp.zeros_like(bq)) * a_half
                g[h % 2] = contrib if g[h % 2] is None else g[h % 2] + contrib
            krows = slice(k * PEER_N_KEYS + r * PEER_ROWS, k * PEER_N_KEYS + (r + 1) * PEER_ROWS)
            hk = ht_ref[krows, lanes]
            gelu2 = hk * (1.0 + lax.erf(hk * math.sqrt(0.5)))
            at_ref[krows, lanes] = gelu2.astype(BF16) * (g[0] + g[1])


def _peer_step(x_ref, bq_ref, r2_ref, aw_ref, cw_ref, u_ref, v_ref, o_ref, at_ref, ht_write, ht_read, k0):
    keys_per_tile = PEER_TE // PEER_N_KEYS
    keys_per_chunk = PEER_CHUNK // PEER_N_KEYS
    tm = x_ref.shape[0]
    tok = tm // keys_per_tile
    u = u_ref[...].astype(BF16) if ht_write is not None else None
    for k in range(keys_per_tile):
        if ht_read is not None:
            _peer_gated_slab(k, k0, bq_ref, r2_ref, aw_ref, cw_ref, ht_read, at_ref)
        if ht_write is not None:
            ht_write[:, k * tok:(k + 1) * tok] = lax.dot_general(
                u, x_ref[k * tok:(k + 1) * tok, :], NT_DIMS, preferred_element_type=F32)
        if ht_read is not None and (k + 1) % keys_per_chunk == 0:
            chunk = slice((k + 1) * PEER_N_KEYS - PEER_CHUNK, (k + 1) * PEER_N_KEYS)
            o_ref[...] += lax.dot_general(at_ref[chunk, :], v_ref[chunk, :].astype(BF16), TN_DIMS,
                                          preferred_element_type=F32)


def _peer_kernel(x_ref, bq_ref, r2_ref, aw_ref, cw_ref, u_ref, v_ref, o_ref, ht0_ref, ht1_ref, at_ref):
    e = pl.program_id(1)
    last = PEER_N_TILES
    keys_per_tile = PEER_TE // PEER_N_KEYS
    assert 8 // keys_per_tile == 2
    step = functools.partial(_peer_step, x_ref, bq_ref, r2_ref, aw_ref, cw_ref, u_ref, v_ref, o_ref, at_ref)
    even = e % 2 == 0

    @pl.when(e == 0)
    def _():
        o_ref[...] = jnp.zeros_like(o_ref)
        step(ht0_ref, None, None)

    @pl.when(jnp.logical_not(even))
    def _():
        step(ht1_ref, ht0_ref, 0)

    @pl.when(even & (e > 0) & (e < last))
    def _():
        step(ht0_ref, ht1_ref, keys_per_tile)

    @pl.when(e == last)
    def _():
        step(None, ht1_ref, keys_per_tile)


def peer_experts(u, bq, r2, aw, cw, exp_u, exp_v, layer):
    tm, te = PEER_TM, PEER_TE
    n_tok = u.shape[0]
    assert PEER_N_TILES % 2 == 0 and n_tok % tm == 0
    tiles_per_row_block = 8 * PEER_N_KEYS // te
    col_spec = pl.BlockSpec((PEER_HEADS * PEER_N_KEYS, tm), lambda i, e: (0, i))
    row_spec = pl.BlockSpec((1, PEER_HEADS, 8, tm),
                            lambda i, e: (jnp.maximum(e - 1, 0) // tiles_per_row_block, 0, 0, i))
    return pl.pallas_call(
        _peer_kernel,
        out_shape=jax.ShapeDtypeStruct((n_tok, D_MODEL), F32),
        grid=(n_tok // tm, PEER_N_TILES + 1),
        in_specs=[pl.BlockSpec((tm, D_MODEL), lambda i, e: (i, 0)),
                  col_spec, col_spec,
                  row_spec, row_spec,
                  pl.BlockSpec((None, te, D_MODEL), lambda i, e: (layer, jnp.minimum(e, PEER_N_TILES - 1), 0)),
                  pl.BlockSpec((None, te, D_MODEL), lambda i, e: (layer, jnp.maximum(e - 1, 0), 0))],
        out_specs=pl.BlockSpec((tm, D_MODEL), lambda i, e: (i, 0)),
        scratch_shapes=[pltpu.VMEM((te, tm), F32), pltpu.VMEM((te, tm), F32), pltpu.VMEM((te, tm), BF16)],
        compiler_params=_params(("parallel", "arbitrary"), 60),
        name="peer_experts",
    )(u, bq, r2, aw, cw, exp_u, exp_v)


def peer_ffn(u, w_q, sub_keys, exp_u, exp_v, layer):
    q = matmul_resident(u, w_q, BF16, "peer_query", layer=layer)
    bq, r2, aw, cw = peer_route(q, sub_keys, layer)
    return peer_experts(u, bq, r2, aw, cw, exp_u, exp_v, layer)


def mla_mixer(u, cache_ckv, cache_krope, w_dq, q_norm, w_uq, w_dkv, kv_norm, w_uk, w_uv, w_o):
    pad = (-(MLA_Q_LORA + MLA_KV_LORA + MLA_ROPE)) % 128
    w_cat = jnp.concatenate([w_dq, w_dkv, jnp.zeros((D_MODEL, pad), F32)], axis=1)
    cq, ckv, kr = mla_down(u, w_cat, q_norm, kv_norm)
    w_uq3 = w_uq.reshape(MLA_Q_LORA, MLA_HEADS, MLA_NOPE + MLA_ROPE)
    w_q_cat = jnp.concatenate([w_uq3[:, :, :MLA_NOPE].reshape(MLA_Q_LORA, -1),
                               w_uq3[:, :, MLA_NOPE:].reshape(MLA_Q_LORA, -1)], axis=1)
    q = matmul_resident(cq, w_q_cat, BF16, "mla_q_up")
    w_nope = MLA_HEADS * MLA_NOPE
    ckv_src = jnp.concatenate([ckv, cache_ckv.reshape(-1, MLA_KV_LORA)], axis=0).astype(BF16)
    kv_up = matmul_resident(ckv_src, jnp.concatenate([w_uk, w_uv], axis=1), BF16, "mla_kv_up")
    cos, sin = (jnp.asarray(t, F32) for t in _rope_tables())
    w_rope = MLA_HEADS * MLA_ROPE
    qr_rot = rope_rows(q, N_PROMPT // 256, w_nope // w_rope, w_rope, N_SAMPLE, cos, sin, BF16)
    cos2, sin2 = (t.reshape(DEC_SEQ // 2, 128) for t in (cos[:, :MLA_ROPE], sin[:, :MLA_ROPE]))
    kr_pairs = kr[N_PROMPT:].reshape(N_SAMPLE // 2, 2 * MLA_ROPE)
    kr_rot = rope_rows(kr_pairs, 0, 0, 2 * MLA_ROPE, N_SAMPLE // 2, cos2, sin2, BF16).reshape(N_SAMPLE, MLA_ROPE)
    o_p = mla_attention_prompt(q, kv_up, kr)
    o_s = mla_attention_sample(q, qr_rot, kv_up, kr_rot, cache_krope.reshape(-1, MLA_ROPE))
    op = matmul_resident(o_p, w_o, BF16, "mla_out", x_sample=o_s)
    return op, ckv, kr


def gqa_mixer(u, cache_k, cache_v, w_qkv, sink, w_o):
    nq = GQA_HEADS * GQA_HEAD_DIM
    nk = GQA_KV_HEADS * GQA_HEAD_DIM
    q = matmul_resident(u, w_qkv, BF16, "gqa_q", n=nq)
    kv = matmul_resident(u, w_qkv, F32, "gqa_kv", col0=nq)
    cos, sin = (jnp.asarray(t, F32) for t in _rope_tables())
    q_rot = rope_rows(q, N_PROMPT // 256, 0, nq, N_SAMPLE, cos, sin, BF16)
    k_rot = rope_rows(kv, N_PROMPT // 256, 0, nk, N_SAMPLE, cos, sin, BF16)
    o_p = gqa_attention_prompt(q, kv, sink)
    o_s = gqa_attention_sample(q_rot, k_rot, kv, cache_k.reshape(-1, nk), cache_v.reshape(-1, nk), sink)
    op = matmul_resident(o_p, w_o, BF16, "gqa_out", x_sample=o_s)
    return op, kv


def fnet_mixer(u, w_out):
    f_p = fourier_real_2d(u, 0, BATCH, SEQ)
    f_s = fourier_real_2d(u, N_PROMPT // DEC_SEQ, DEC_BATCH, DEC_SEQ)
    return matmul_resident(f_p, w_out, BF16, "fnet_out", x_sample=f_s)


def conv_mixer(u, w_in, conv_w, conv_b, w_out):
    h3 = matmul(u, w_in, BF16, name="conv_in")
    y_p = gated_conv(h3, conv_w, conv_b, 0, BATCH, SEQ)
    y_s = gated_conv(h3, conv_w, conv_b, N_PROMPT // DEC_SEQ, DEC_BATCH, DEC_SEQ)
    return matmul_resident(y_p, w_out, BF16, "conv_out", x_sample=y_s)


def kernel(x_prompt, x_sample, cache_l0_ckv, cache_l0_krope, cache_l1_k, cache_l1_v, c, c_ctx, ada_w, ada_b, ln1_g, ln1_b, ln2_g, ln2_b, mla_w_dq, mla_q_norm, mla_w_uq, mla_w_dkv, mla_kv_norm, mla_w_uk, mla_w_uv, mla_w_o, gqa_w_qkv, gqa_sink, gqa_w_o, fnet_w_out, conv_w_in, conv_w, conv_b, conv_w_out, peer_w_q, peer_sub_keys, peer_u, peer_v):
    x = jnp.concatenate([x_prompt.reshape(N_PROMPT, D_MODEL), x_sample.reshape(N_SAMPLE, D_MODEL)], axis=0)
    cond = jnp.concatenate([c_ctx[None, :], c, jnp.zeros((N_COND - 1 - DEC_BATCH, D_MODEL), F32)], axis=0)
    mods_all = ada_modulation_all(cond, ada_w, ada_b).reshape(DEPTH, N_COND, 6, 1, D_MODEL)

    u = modulate_tokens(x, mods_all[0])
    new_ckv = new_krope = new_kv = None
    for i in range(DEPTH):
        mods = mods_all[i]
        if i == 0:
            delta, new_ckv, new_krope = mla_mixer(u, cache_l0_ckv, cache_l0_krope, mla_w_dq, mla_q_norm, mla_w_uq,
                                                  mla_w_dkv, mla_kv_norm, mla_w_uk, mla_w_uv, mla_w_o)
        elif i == 1:
            delta, new_kv = gqa_mixer(u, cache_l1_k, cache_l1_v, gqa_w_qkv, gqa_sink, gqa_w_o)
        elif i == 2:
            delta = fnet_mixer(u, fnet_w_out)
        else:
            delta = conv_mixer(u, conv_w_in, conv_w, conv_b, conv_w_out)
        x, u = post_norm(x, delta, mods, 2, ln1_g[i], ln1_b[i], mods, 3)
        delta = peer_ffn(u, peer_w_q, peer_sub_keys, peer_u, peer_v, i)
        if i + 1 < DEPTH:
            x, u = post_norm(x, delta, mods, 5, ln2_g[i], ln2_b[i], mods_all[i + 1], 0)
        else:
            y_prompt = post_norm(x, delta, mods, 5, ln2_g[i], ln2_b[i], rows=(0, N_PROMPT))
            y_sample = post_norm(x, delta, mods, 5, ln2_g[i], ln2_b[i], rows=(N_PROMPT, N_SAMPLE))

    nk = GQA_KV_HEADS * GQA_HEAD_DIM
    return (y_prompt.reshape(BATCH, SEQ, D_MODEL),
            y_sample.reshape(DEC_BATCH, DEC_SEQ, D_MODEL),
            new_ckv[:N_PROMPT].reshape(BATCH, SEQ, MLA_KV_LORA),
            new_krope[:N_PROMPT].reshape(BATCH, SEQ, MLA_ROPE),
            new_kv[:N_PROMPT, :nk].reshape(BATCH, SEQ, GQA_KV_HEADS, GQA_HEAD_DIM),
            new_kv[:N_PROMPT, nk:].reshape(BATCH, SEQ, GQA_KV_HEADS, GQA_HEAD_DIM))
```

```python
import functools
import math

import numpy as np
import jax
import jax.numpy as jnp
from jax import lax
from jax.experimental import pallas as pl
from jax.experimental.pallas import tpu as pltpu

F32 = jnp.float32
BF16 = jnp.bfloat16

D_MODEL = 2048
BATCH = 16
SEQ = 256
DEPTH = 4
DEC_BATCH = 2
DEC_SEQ = 1024
PAST_LEN = 256
GRID_W = 64
ROPE_BASE = 10000.0
LN_EPS = 1e-5
RMS_EPS = 1e-6
NEG_INF = -1e30
DEEPNORM_ALPHA = (2 * DEPTH) ** 0.25

MLA_HEADS = 16
MLA_Q_LORA = 512
MLA_KV_LORA = 512
MLA_NOPE = 128
MLA_ROPE = 64
MLA_V = 128
MLA_SCALE = (MLA_NOPE + MLA_ROPE) ** -0.5

GQA_HEADS = 32
GQA_KV_HEADS = 8
GQA_HEAD_DIM = 64
WINDOW = 128
GQA_SCALE = GQA_HEAD_DIM ** -0.5

FNET_GROUPS = 4
FNET_GROUP_DIM = D_MODEL // FNET_GROUPS

PEER_HEADS = 8
PEER_N_KEYS = 128
PEER_N_EXPERTS = PEER_N_KEYS * PEER_N_KEYS
PEER_HALF = 128
PEER_TOPK = 16

N_PROMPT = BATCH * SEQ
N_SAMPLE = DEC_BATCH * DEC_SEQ
N_TOK = N_PROMPT + N_SAMPLE
N_COND = 8

MIB = 1024 * 1024

NT_DIMS = (((1,), (1,)), ((), ()))
TN_DIMS = (((0,), (0,)), ((), ()))


def _params(sem, vmem_mib, flags=None):
    return pltpu.CompilerParams(dimension_semantics=sem, vmem_limit_bytes=vmem_mib * MIB, flags=flags)


def _group_of_tile(i, tm):
    return jnp.maximum((i * tm) // DEC_SEQ - (N_PROMPT // DEC_SEQ - 1), 0)


def _mod_spec(tm, which):
    return pl.BlockSpec((None, None, 1, D_MODEL), lambda i: (_group_of_tile(i, tm), which, 0, 0))


def _ada_kernel(c_ref, w_ref, b_ref, o_ref):
    c = c_ref[...]
    s = (c * jax.nn.sigmoid(c)).astype(BF16)
    o_ref[...] = jnp.dot(s, w_ref[...].astype(BF16), preferred_element_type=F32) + b_ref[...]


def ada_modulation_all(cond, ada_w, ada_b):
    tn = 1024
    n_out = 6 * D_MODEL
    return pl.pallas_call(
        _ada_kernel,
        out_shape=jax.ShapeDtypeStruct((DEPTH, N_COND, n_out), F32),
        grid=(DEPTH, n_out // tn),
        in_specs=[
            pl.BlockSpec((N_COND, D_MODEL), lambda l, j: (0, 0)),
            pl.BlockSpec((None, D_MODEL, tn), lambda l, j: (l, 0, j)),
            pl.BlockSpec((None, 1, tn), lambda l, j: (l, 0, j)),
        ],
        out_specs=pl.BlockSpec((None, N_COND, tn), lambda l, j: (l, 0, j)),
        compiler_params=_params(("parallel", "parallel"), 40),
        name="ada_modulation",
    )(cond, ada_w, ada_b.reshape(DEPTH, 1, n_out))


def _modulate_kernel(x_ref, sh_ref, sc_ref, u_ref):
    u_ref[...] = (x_ref[...] * (1.0 + sc_ref[...]) + sh_ref[...]).astype(u_ref.dtype)


def modulate_tokens(x, mods):
    tm = 256
    return pl.pallas_call(
        _modulate_kernel,
        out_shape=jax.ShapeDtypeStruct((N_TOK, D_MODEL), BF16),
        grid=(N_TOK // tm,),
        in_specs=[pl.BlockSpec((tm, D_MODEL), lambda i: (i, 0)), _mod_spec(tm, 0), _mod_spec(tm, 1)],
        out_specs=pl.BlockSpec((tm, D_MODEL), lambda i: (i, 0)),
        compiler_params=_params(("parallel",), 24),
        name="modulate",
    )(x, mods, mods)


def _layer_norm_rows(y, g, b):
    mu = jnp.mean(y, axis=-1, keepdims=True)
    yc = y - mu
    var = jnp.mean(yc * yc, axis=-1, keepdims=True)
    return yc * lax.rsqrt(var + LN_EPS) * g + b


def _post_norm_kernel(x_ref, d_ref, gate_ref, g_ref, b_ref, xo_ref):
    y = DEEPNORM_ALPHA * x_ref[...] + gate_ref[...] * d_ref[...].astype(F32)
    xo_ref[...] = _layer_norm_rows(y, g_ref[...], b_ref[...])


def _post_norm_mod_kernel(x_ref, d_ref, gate_ref, g_ref, b_ref, sh_ref, sc_ref, xo_ref, uo_ref):
    y = DEEPNORM_ALPHA * x_ref[...] + gate_ref[...] * d_ref[...].astype(F32)
    xn = _layer_norm_rows(y, g_ref[...], b_ref[...])
    xo_ref[...] = xn
    uo_ref[...] = (xn * (1.0 + sc_ref[...]) + sh_ref[...]).astype(uo_ref.dtype)


def post_norm(x, delta, mods, gate_idx, g, b, next_mods=None, next_idx=None, rows=(0, N_TOK)):
    tm = 256
    row = pl.BlockSpec((tm, D_MODEL), lambda i: (i, 0))
    vec = pl.BlockSpec((1, D_MODEL), lambda i: (0, 0))
    ins = [x, delta, mods, g.reshape(1, D_MODEL), b.reshape(1, D_MODEL)]
    specs = [row, row, _mod_spec(tm, gate_idx), vec, vec]
    if next_mods is None:
        i0 = rows[0] // tm
        row_in = pl.BlockSpec((tm, D_MODEL), lambda i: (i0 + i, 0))
        gate = pl.BlockSpec((None, None, 1, D_MODEL), lambda i: (_group_of_tile(i0 + i, tm), gate_idx, 0, 0))
        return pl.pallas_call(
            _post_norm_kernel,
            out_shape=jax.ShapeDtypeStruct((rows[1], D_MODEL), F32),
            grid=(rows[1] // tm,), in_specs=[row_in, row_in, gate, vec, vec], out_specs=row,
            compiler_params=_params(("parallel",), 32), name="post_norm",
        )(*ins)
    ins += [next_mods, next_mods]
    specs += [_mod_spec(tm, next_idx), _mod_spec(tm, next_idx + 1)]
    return pl.pallas_call(
        _post_norm_mod_kernel,
        out_shape=(jax.ShapeDtypeStruct((N_TOK, D_MODEL), F32), jax.ShapeDtypeStruct((N_TOK, D_MODEL), BF16)),
        grid=(N_TOK // tm,), in_specs=specs, out_specs=(row, row),
        compiler_params=_params(("parallel",), 32), name="post_norm_mod",
    )(*ins)


def _mm_kernel(x_ref, w_ref, o_ref):
    o_ref[...] = jnp.dot(x_ref[...], w_ref[...].astype(BF16), preferred_element_type=F32).astype(o_ref.dtype)


def _mm_resident_kernel(*refs, prompt_tiles):
    if prompt_tiles is None:
        x_ref, w_ref, o_ref, wb_ref = refs
    else:
        x_ref, xs_ref, w_ref, o_ref, wb_ref = refs
    i = pl.program_id(0)

    @pl.when(i == 0)
    def _():
        wb_ref[...] = w_ref[...].astype(BF16)

    if prompt_tiles is None:
        o_ref[...] = jnp.dot(x_ref[...], wb_ref[...], preferred_element_type=F32).astype(o_ref.dtype)
    else:
        @pl.when(i < prompt_tiles)
        def _():
            o_ref[...] = jnp.dot(x_ref[...], wb_ref[...], preferred_element_type=F32).astype(o_ref.dtype)

        @pl.when(i >= prompt_tiles)
        def _():
            o_ref[...] = jnp.dot(xs_ref[...], wb_ref[...], preferred_element_type=F32).astype(o_ref.dtype)


def matmul_resident(x, w, out_dtype, name, x_sample=None, layer=None, col0=0, n=None):
    tm = 512
    k = x.shape[1]
    n = w.shape[-1] - col0 if n is None else n
    assert col0 % n == 0
    j0 = col0 // n
    rows = x.shape[0] + (0 if x_sample is None else x_sample.shape[0])
    once = pl.Buffered(1)
    if layer is None:
        w_spec = pl.BlockSpec((k, n), lambda i: (0, j0), pipeline_mode=once)
    else:
        w_spec = pl.BlockSpec((None, k, n), lambda i: (layer, 0, j0), pipeline_mode=once)
    if x_sample is None:
        prompt_tiles = None
        xs, x_specs = (x,), [pl.BlockSpec((tm, k), lambda i: (i, 0))]
    else:
        prompt_tiles = x.shape[0] // tm
        xs = (x, x_sample)
        x_specs = [pl.BlockSpec((tm, k), lambda i: (jnp.minimum(i, prompt_tiles - 1), 0)),
                   pl.BlockSpec((tm, k), lambda i: (jnp.maximum(i - prompt_tiles, 0), 0))]
    return pl.pallas_call(
        functools.partial(_mm_resident_kernel, prompt_tiles=prompt_tiles),
        out_shape=jax.ShapeDtypeStruct((rows, n), out_dtype),
        grid=(rows // tm,),
        in_specs=x_specs + [w_spec],
        out_specs=pl.BlockSpec((tm, n), lambda i: (i, 0)),
        scratch_shapes=[pltpu.VMEM((k, n), BF16)],
        compiler_params=_params(("arbitrary",), 48),
        name=name,
    )(*xs, w)


def matmul(x, w, out_dtype, tm=2048, tn=512, name="matmul", layer=None, col0=0, n=None):
    m, k = x.shape
    n = w.shape[-1] - col0 if n is None else n
    tm = min(tm, m)
    tn = min(tn, n)
    assert m % tm == 0 and n % tn == 0 and col0 % tn == 0, (m, n, tm, tn, col0)
    j0 = col0 // tn
    if layer is None:
        w_spec = pl.BlockSpec((k, tn), lambda i, j: (0, j0 + j))
    else:
        w_spec = pl.BlockSpec((None, k, tn), lambda i, j: (layer, 0, j0 + j))
    return pl.pallas_call(
        _mm_kernel,
        out_shape=jax.ShapeDtypeStruct((m, n), out_dtype),
        grid=(m // tm, n // tn),
        in_specs=[pl.BlockSpec((tm, k), lambda i, j: (i, 0)), w_spec],
        out_specs=pl.BlockSpec((tm, tn), lambda i, j: (i, j)),
        compiler_params=_params(("parallel", "parallel"), 48),
        name=name,
    )(x, w)


def _rms_rows(y, g):
    return y * lax.rsqrt(jnp.mean(y * y, axis=-1, keepdims=True) + RMS_EPS) * g


def _mla_down_kernel(x_ref, w_ref, qn_ref, kvn_ref, cq_ref, ckv_ref, kr_ref):
    y = jnp.dot(x_ref[...], w_ref[...].astype(BF16), preferred_element_type=F32)
    cq_ref[...] = _rms_rows(y[:, :MLA_Q_LORA], qn_ref[...]).astype(cq_ref.dtype)
    ckv_ref[...] = _rms_rows(y[:, MLA_Q_LORA:MLA_Q_LORA + MLA_KV_LORA], kvn_ref[...])
    kr_ref[...] = y[:, MLA_Q_LORA + MLA_KV_LORA:MLA_Q_LORA + MLA_KV_LORA + MLA_ROPE]


def mla_down(u, w_cat, q_norm, kv_norm):
    tm = 512
    n = w_cat.shape[1]
    return pl.pallas_call(
        _mla_down_kernel,
        out_shape=(jax.ShapeDtypeStruct((N_TOK, MLA_Q_LORA), BF16),
                   jax.ShapeDtypeStruct((N_TOK, MLA_KV_LORA), F32),
                   jax.ShapeDtypeStruct((N_TOK, MLA_ROPE), F32)),
        grid=(N_TOK // tm,),
        in_specs=[pl.BlockSpec((tm, D_MODEL), lambda i: (i, 0)),
                  pl.BlockSpec((D_MODEL, n), lambda i: (0, 0)),
                  pl.BlockSpec((1, MLA_Q_LORA), lambda i: (0, 0)),
                  pl.BlockSpec((1, MLA_KV_LORA), lambda i: (0, 0))],
        out_specs=(pl.BlockSpec((tm, MLA_Q_LORA), lambda i: (i, 0)),
                   pl.BlockSpec((tm, MLA_KV_LORA), lambda i: (i, 0)),
                   pl.BlockSpec((tm, MLA_ROPE), lambda i: (i, 0))),
        compiler_params=_params(("parallel",), 48),
        name="mla_down",
    )(u, w_cat, q_norm.reshape(1, -1), kv_norm.reshape(1, -1))


def _rope_tables():
    t = np.arange(DEC_SEQ)
    quarter = MLA_ROPE // 4
    inv_freq = ROPE_BASE ** (-np.arange(quarter, dtype=np.float64) / quarter)
    ang_row = (t // GRID_W)[:, None] * inv_freq[None, :]
    ang_col = (t % GRID_W)[:, None] * inv_freq[None, :]
    cos = np.concatenate([np.cos(ang_row)] * 2 + [np.cos(ang_col)] * 2, -1)
    sin = np.concatenate([-np.sin(ang_row), np.sin(ang_row), -np.sin(ang_col), np.sin(ang_col)], -1)
    return np.tile(cos, (1, 2)), np.tile(sin, (1, 2))


def _rope_kernel(x_ref, c_ref, s_ref, o_ref):
    x = x_ref[...].astype(F32)
    w = x.shape[-1]
    reps = w // c_ref.shape[-1]
    cos = jnp.tile(c_ref[...], (1, reps))
    sin = jnp.tile(s_ref[...], (1, reps))
    lane = lax.broadcasted_iota(jnp.int32, x.shape, 1)
    partner = jnp.where((lane % 32) < 16, pltpu.roll(x, w - 16, 1), pltpu.roll(x, 16, 1))
    o_ref[...] = (x * cos + partner * sin).astype(o_ref.dtype)


def rope_rows(x, row_block0, col_block, width, n_rows, cos, sin, out_dtype, tr=256):
    t_blocks = cos.shape[0] // tr
    return pl.pallas_call(
        _rope_kernel,
        out_shape=jax.ShapeDtypeStruct((n_rows, width), out_dtype),
        grid=(n_rows // tr,),
        in_specs=[pl.BlockSpec((tr, width), lambda i: (row_block0 + i, col_block)),
                  pl.BlockSpec((tr, 128), lambda i: (i % t_blocks, 0)),
                  pl.BlockSpec((tr, 128), lambda i: (i % t_blocks, 0))],
        out_specs=pl.BlockSpec((tr, width), lambda i: (i, 0)),
        compiler_params=_params(("parallel",), 32),
        name="axial_rope",
    )(x, cos, sin)


def _softmax_parts(parts, sink):
    m = parts[0].max(axis=-1, keepdims=True)
    for s in parts[1:]:
        m = jnp.maximum(m, s.max(axis=-1, keepdims=True))
    if sink is not None:
        m = jnp.maximum(m, sink)
    ps = [jnp.exp(s - m) for s in parts]
    l = ps[0].sum(axis=-1, keepdims=True)
    for p in ps[1:]:
        l = l + p.sum(axis=-1, keepdims=True)
    if sink is not None:
        l = l + jnp.exp(sink - m)
    return ps, 1.0 / l


def _mla_attn_kernel(*refs, with_ctx):
    if with_ctx:
        qn_ref, qr_ref, kn_ref, v_ref, kr_ref, kn2_ref, v2_ref, kr2_ref, o_ref = refs
    else:
        qn_ref, qr_ref, kn_ref, v_ref, kr_ref, o_ref = refs
    kr = kr_ref[...].astype(BF16)
    kr2 = kr2_ref[...].astype(BF16) if with_ctx else None
    for h in range(MLA_HEADS):
        n0, n1 = h * MLA_NOPE, (h + 1) * MLA_NOPE
        qn = qn_ref[:, n0:n1]
        qr = qr_ref[:, h * MLA_ROPE:(h + 1) * MLA_ROPE]
        s = (lax.dot_general(qn, kn_ref[:, n0:n1], NT_DIMS, preferred_element_type=F32)
             + lax.dot_general(qr, kr, NT_DIMS, preferred_element_type=F32)) * MLA_SCALE
        parts = [s]
        if with_ctx:
            s2 = (lax.dot_general(qn, kn2_ref[:, n0:n1], NT_DIMS, preferred_element_type=F32)
                  + lax.dot_general(qr, kr2, NT_DIMS, preferred_element_type=F32)) * MLA_SCALE
            parts.append(s2)
        ps, inv_l = _softmax_parts(parts, None)
        o = jnp.dot(ps[0].astype(BF16), v_ref[:, n0:n1], preferred_element_type=F32)
        if with_ctx:
            o = o + jnp.dot(ps[1].astype(BF16), v2_ref[:, n0:n1], preferred_element_type=F32)
        o_ref[:, n0:n1] = (o * inv_l).astype(o_ref.dtype)


def mla_attention_prompt(q, kv_up, kr):
    w = MLA_HEADS * MLA_NOPE
    return pl.pallas_call(
        functools.partial(_mla_attn_kernel, with_ctx=False),
        out_shape=jax.ShapeDtypeStruct((N_PROMPT, w), BF16),
        grid=(BATCH,),
        in_specs=[pl.BlockSpec((SEQ, w), lambda b: (b, 0)),
                  pl.BlockSpec((SEQ, MLA_HEADS * MLA_ROPE), lambda b: (b, w // (MLA_HEADS * MLA_ROPE))),
                  pl.BlockSpec((SEQ, w), lambda b: (b, 0)),
                  pl.BlockSpec((SEQ, w), lambda b: (b, 1)),
                  pl.BlockSpec((SEQ, MLA_ROPE), lambda b: (b, 0))],
        out_specs=pl.BlockSpec((SEQ, w), lambda b: (b, 0)),
        compiler_params=_params(("parallel",), 32),
        name="mla_attention_prompt",
    )(q, q, kv_up, kv_up, kr)


def mla_attention_sample(qn, qr_rot, kv_up, kr_rot, kr_cache):
    w = MLA_HEADS * MLA_NOPE
    tq = 256
    nq = DEC_SEQ // tq
    q0 = N_PROMPT // tq
    lat0 = N_PROMPT // DEC_SEQ
    ctx0 = N_TOK // PAST_LEN
    return pl.pallas_call(
        functools.partial(_mla_attn_kernel, with_ctx=True),
        out_shape=jax.ShapeDtypeStruct((N_SAMPLE, w), BF16),
        grid=(DEC_BATCH, nq),
        in_specs=[pl.BlockSpec((tq, w), lambda b, i: (q0 + b * nq + i, 0)),
                  pl.BlockSpec((tq, MLA_HEADS * MLA_ROPE), lambda b, i: (b * nq + i, 0)),
                  pl.BlockSpec((DEC_SEQ, w), lambda b, i: (lat0 + b, 0)),
                  pl.BlockSpec((DEC_SEQ, w), lambda b, i: (lat0 + b, 1)),
                  pl.BlockSpec((DEC_SEQ, MLA_ROPE), lambda b, i: (b, 0)),
                  pl.BlockSpec((PAST_LEN, w), lambda b, i: (ctx0 + b, 0)),
                  pl.BlockSpec((PAST_LEN, w), lambda b, i: (ctx0 + b, 1)),
                  pl.BlockSpec((PAST_LEN, MLA_ROPE), lambda b, i: (b, 0))],
        out_specs=pl.BlockSpec((tq, w), lambda b, i: (b * nq + i, 0)),
        compiler_params=_params(("parallel", "parallel"), 56),
        name="mla_attention_sample",
    )(qn, qr_rot, kv_up, kv_up, kr_rot, kv_up, kv_up, kr_cache)


def _gqa_attn_kernel(*refs, with_ctx, tq):
    if with_ctx:
        sink_ref, q_ref, k_ref, v_ref, k2_ref, v2_ref, o_ref = refs
    else:
        sink_ref, q_ref, k_ref, v_ref, o_ref = refs
    group = GQA_HEADS // GQA_KV_HEADS
    d = GQA_HEAD_DIM
    if with_ctx:
        t = pl.program_id(1) * tq + lax.broadcasted_iota(jnp.int32, (tq, DEC_SEQ), 0)
        s_pos = lax.broadcasted_iota(jnp.int32, (tq, DEC_SEQ), 1)
        in_window = jnp.abs(t - s_pos) <= WINDOW
    for hk in range(GQA_KV_HEADS):
        k = k_ref[:, hk * d:(hk + 1) * d].astype(BF16)
        v = v_ref[:, hk * d:(hk + 1) * d].astype(BF16)
        if with_ctx:
            k2 = k2_ref[:, hk * d:(hk + 1) * d].astype(BF16)
            v2 = v2_ref[:, hk * d:(hk + 1) * d].astype(BF16)
        for g in range(group):
            h = hk * group + g
            q = q_ref[:, h * d:(h + 1) * d]
            s = lax.dot_general(q, k, NT_DIMS, preferred_element_type=F32) * GQA_SCALE
            parts = [s]
            if with_ctx:
                parts = [jnp.where(in_window, s, NEG_INF),
                         lax.dot_general(q, k2, NT_DIMS, preferred_element_type=F32) * GQA_SCALE]
            ps, inv_l = _softmax_parts(parts, sink_ref[h])
            o = jnp.dot(ps[0].astype(BF16), v, preferred_element_type=F32)
            if with_ctx:
                o = o + jnp.dot(ps[1].astype(BF16), v2, preferred_element_type=F32)
            o_ref[:, h * d:(h + 1) * d] = (o * inv_l).astype(o_ref.dtype)


def gqa_attention_prompt(q, kv, sink):
    wq = GQA_HEADS * GQA_HEAD_DIM
    wk = GQA_KV_HEADS * GQA_HEAD_DIM
    return pl.pallas_call(
        functools.partial(_gqa_attn_kernel, with_ctx=False, tq=SEQ),
        out_shape=jax.ShapeDtypeStruct((N_PROMPT, wq), BF16),
        grid=(BATCH,),
        in_specs=[pl.BlockSpec(memory_space=pltpu.SMEM),
                  pl.BlockSpec((SEQ, wq), lambda b: (b, 0)),
                  pl.BlockSpec((SEQ, wk), lambda b: (b, 0)),
                  pl.BlockSpec((SEQ, wk), lambda b: (b, 1))],
        out_specs=pl.BlockSpec((SEQ, wq), lambda b: (b, 0)),
        compiler_params=_params(("parallel",), 32),
        name="gqa_attention_prompt",
    )(sink, q, kv, kv)


def gqa_attention_sample(q_rot, k_rot, kv, k_cache, v_cache, sink):
    wq = GQA_HEADS * GQA_HEAD_DIM
    wk = GQA_KV_HEADS * GQA_HEAD_DIM
    tq = 256
    nq = DEC_SEQ // tq
    lat0 = N_PROMPT // DEC_SEQ
    return pl.pallas_call(
        functools.partial(_gqa_attn_kernel, with_ctx=True, tq=tq),
        out_shape=jax.ShapeDtypeStruct((N_SAMPLE, wq), BF16),
        grid=(DEC_BATCH, nq),
        in_specs=[pl.BlockSpec(memory_space=pltpu.SMEM),
                  pl.BlockSpec((tq, wq), lambda b, i: (b * nq + i, 0)),
                  pl.BlockSpec((DEC_SEQ, wk), lambda b, i: (b, 0)),
                  pl.BlockSpec((DEC_SEQ, wk), lambda b, i: (lat0 + b, 1)),
                  pl.BlockSpec((PAST_LEN, wk), lambda b, i: (b, 0)),
                  pl.BlockSpec((PAST_LEN, wk), lambda b, i: (b, 0))],
        out_specs=pl.BlockSpec((tq, wq), lambda b, i: (b * nq + i, 0)),
        compiler_params=_params(("parallel", "parallel"), 56),
        name="gqa_attention_sample",
    )(sink, q_rot, k_rot, kv, k_cache, v_cache)


def _dft_tables(n):
    jk = (np.arange(n)[:, None] * np.arange(n)[None, :]) % n
    ang = 2.0 * np.pi * jk / n
    return np.cos(ang) / math.sqrt(n), np.sin(ang) / math.sqrt(n)


def _fnet_kernel(x_ref, ct_ref, st_ref, cc_ref, sc_ref, o_ref):
    x = x_ref[...]
    y1 = jnp.dot(x, cc_ref[...], preferred_element_type=F32).astype(BF16)
    y2 = jnp.dot(x, sc_ref[...], preferred_element_type=F32).astype(BF16)
    o = (jnp.dot(ct_ref[...], y1, preferred_element_type=F32)
         - jnp.dot(st_ref[...], y2, preferred_element_type=F32))
    o_ref[...] = o.astype(o_ref.dtype)


def fourier_real_2d(u, seq_block0, n_seq, t):
    ct, st = (jnp.asarray(a, BF16) for a in _dft_tables(t))
    cc, sc = (jnp.asarray(a, BF16) for a in _dft_tables(FNET_GROUP_DIM))
    gd = FNET_GROUP_DIM
    return pl.pallas_call(
        _fnet_kernel,
        out_shape=jax.ShapeDtypeStruct((n_seq * t, D_MODEL), BF16),
        grid=(n_seq, FNET_GROUPS),
        in_specs=[pl.BlockSpec((t, gd), lambda b, g: (seq_block0 + b, g)),
                  pl.BlockSpec((t, t), lambda b, g: (0, 0)),
                  pl.BlockSpec((t, t), lambda b, g: (0, 0)),
                  pl.BlockSpec((gd, gd), lambda b, g: (0, 0)),
                  pl.BlockSpec((gd, gd), lambda b, g: (0, 0))],
        out_specs=pl.BlockSpec((t, gd), lambda b, g: (b, g)),
        compiler_params=_params(("parallel", "parallel"), 32),
        name="fourier_mix",
    )(u, ct, st, cc, sc)


def _conv_kernel(b_ref, c_ref, h_ref, w_ref, cb_ref, o_ref):
    z = c_ref[...].astype(F32) * h_ref[...].astype(F32)
    t = z.shape[0]
    row = lax.broadcasted_iota(jnp.int32, z.shape, 0)
    z_prev = jnp.where(row == 0, 0.0, pltpu.roll(z, 1, 0))
    z_next = jnp.where(row == t - 1, 0.0, pltpu.roll(z, t - 1, 0))
    conv = z_prev * w_ref[0:1, :] + z * w_ref[1:2, :] + z_next * w_ref[2:3, :] + cb_ref[...]
    o_ref[...] = (b_ref[...].astype(F32) * conv).astype(o_ref.dtype)


def gated_conv(h3, conv_w, conv_b, seq_block0, n_seq, t):
    tn = 512
    nj = D_MODEL // tn
    return pl.pallas_call(
        _conv_kernel,
        out_shape=jax.ShapeDtypeStruct((n_seq * t, D_MODEL), BF16),
        grid=(n_seq, nj),
        in_specs=[pl.BlockSpec((t, tn), lambda s, j: (seq_block0 + s, j)),
                  pl.BlockSpec((t, tn), lambda s, j: (seq_block0 + s, nj + j)),
                  pl.BlockSpec((t, tn), lambda s, j: (seq_block0 + s, 2 * nj + j)),
                  pl.BlockSpec((3, tn), lambda s, j: (0, j)),
                  pl.BlockSpec((1, tn), lambda s, j: (0, j))],
        out_specs=pl.BlockSpec((t, tn), lambda s, j: (s, j)),
        compiler_params=_params(("parallel", "parallel"), 32),
        name="gated_conv",
    )(h3, h3, h3, conv_w, conv_b.reshape(1, D_MODEL))


_N_RANKS = PEER_TOPK + 1
_CAND_PAIRS = [(a, b) for a in range(_N_RANKS) for b in range(_N_RANKS) if (a + 1) * (b + 1) <= _N_RANKS]


_NO_RANK = 127.0


def _top_values_exact(s, k, want_rank):
    n = s.shape[0]
    idx = lax.broadcasted_iota(jnp.int32, s.shape, 0)
    rank = jnp.full(s.shape, _NO_RANK, F32) if want_rank else None
    out = []
    for it in range(k):
        m = jnp.max(s, axis=0, keepdims=True)
        out.append(m)
        first = jnp.min(jnp.where(s == m, idx, n), axis=0, keepdims=True)
        hit = idx == first
        s = jnp.where(hit, -jnp.inf, s)
        if want_rank:
            rank = jnp.where(hit, float(it), rank)
    return out, rank


def _all_equal(x, value):
    return jnp.min(jnp.where(x == value, 1.0, 0.0)) > 0.5


def _top_values_distinct(s, k, n_pad):
    out = []
    for it in range(k):
        m = jnp.max(s, axis=0, keepdims=True)
        out.append(m)
        s = jnp.where(s == m, -jnp.inf, s)
    removed = jnp.sum(jnp.where(s == -jnp.inf, 1.0, 0.0), axis=0, keepdims=True)
    return out, _all_equal(removed, float(k + n_pad))


def _batcher_network(n):
    def merge(lo, hi, r):
        step = r * 2
        if step < hi - lo:
            yield from merge(lo, hi, step)
            yield from merge(lo + r, hi, step)
            yield from [(i, i + r) for i in range(lo + r, hi - r, step)]
        else:
            yield (lo, lo + r)

    def sort(lo, hi):
        if hi - lo >= 1:
            mid = lo + (hi - lo) // 2
            yield from sort(lo, mid)
            yield from sort(mid + 1, hi)
            yield from merge(lo, hi, 1)

    return list(sort(0, n - 1))


_SUBLANES = 8
_KEY_TILES = PEER_N_KEYS // _SUBLANES
_SORT_KEY_TILES = _batcher_network(_KEY_TILES)


def _top_values_sorted(s, k):
    tiles = [s[j * _SUBLANES:(j + 1) * _SUBLANES, :] for j in range(_KEY_TILES)]
    for i, j in _SORT_KEY_TILES:
        tiles[i], tiles[j] = jnp.maximum(tiles[i], tiles[j]), jnp.minimum(tiles[i], tiles[j])
    out = []
    pops = jnp.zeros_like(tiles[0])
    for it in range(k):
        m = jnp.max(tiles[0], axis=0, keepdims=True)
        out.append(m)
        hit = tiles[0] == m
        pops = pops + jnp.where(hit, 1.0, 0.0)
        depth = k - 1 - it
        for j in range(min(depth, _KEY_TILES - 1)):
            tiles[j] = jnp.where(hit, tiles[j + 1], tiles[j])
        if depth >= _KEY_TILES - 1:
            tiles[-1] = jnp.where(hit, -jnp.inf, tiles[-1])
    return out, _all_equal(jnp.sum(pops, axis=0, keepdims=True), float(k))


def _ranks_from_values(s, values):
    rank = jnp.full(s.shape, _NO_RANK, F32)
    for r in reversed(range(len(values))):
        rank = jnp.where(s >= values[r], float(r), rank)
    return rank


def _dup_bf16_words(x):
    bits = pltpu.bitcast(x.astype(BF16).astype(F32), jnp.int32)
    return bits | lax.shift_right_logical(bits, jnp.int32(16))


def _route_head(h, s1, s2, fast, bq_ref, r2_ref, aw_ref, cw_ref):
    tm = s1.shape[1]
    if fast:
        v1, ok1 = _top_values_sorted(s1, _N_RANKS)
        v2, ok2 = _top_values_sorted(s2, _N_RANKS)
        rank2 = _ranks_from_values(s2, v2)
    else:
        v1, _ = _top_values_exact(s1, _N_RANKS, False)
        v2, rank2 = _top_values_exact(s2, _N_RANKS, True)
    n_pad = (-len(_CAND_PAIRS)) % 8
    n_cand = len(_CAND_PAIRS) + n_pad
    cand_row = lax.broadcasted_iota(jnp.int32, (n_cand, tm), 0)
    cand = jnp.full((n_cand, tm), -jnp.inf, F32)
    for r, (a, b) in enumerate(_CAND_PAIRS):
        cand = jnp.where(cand_row == r, v1[a] + v2[b], cand)
    if fast:
        top, ok3 = _top_values_distinct(cand, _N_RANKS, n_pad)
    else:
        top, _ = _top_values_exact(cand, _N_RANKS, False)
    tau = 0.5 * (top[PEER_TOPK - 1] + top[PEER_TOPK])
    z = jnp.exp(top[0] - top[0])
    for kk in range(1, PEER_TOPK):
        z = z + jnp.exp(top[kk] - top[0])
    inv_z = 1.0 / z
    count = jnp.zeros_like(s1)
    for b in range(PEER_TOPK):
        count = jnp.where(s1 >= tau - v2[b], float(b + 1), count)
    rows = slice(h * PEER_N_KEYS, (h + 1) * PEER_N_KEYS)
    bq_ref[rows, :] = (jnp.exp(s2 - v2[0]) * inv_z).astype(BF16)
    r2_ref[rows, :] = rank2.astype(BF16)
    a_half = 0.5 * jnp.exp(s1 - v1[0])
    aw_ref[:, h, :, :] = _dup_bf16_words(a_half).reshape(PEER_N_KEYS // 8, 8, tm)
    cw_ref[:, h, :, :] = _dup_bf16_words(count).reshape(PEER_N_KEYS // 8, 8, tm)
    return jnp.logical_and(jnp.logical_and(ok1, ok2), ok3) if fast else None


def _route_kernel(q_ref, sk_ref, bq_ref, r2_ref, aw_ref, cw_ref):
    outs = (bq_ref, r2_ref, aw_ref, cw_ref)

    def scores(hc):
        return lax.dot_general(sk_ref[hc].astype(BF16), q_ref[:, hc * PEER_HALF:(hc + 1) * PEER_HALF],
                               NT_DIMS, preferred_element_type=F32)

    no_ties = [_route_head(h, scores(2 * h), scores(2 * h + 1), True, *outs) for h in range(PEER_HEADS)]
    for h in range(PEER_HEADS):
        @pl.when(jnp.logical_not(no_ties[h]))
        def _():
            _route_head(h, scores(2 * h), scores(2 * h + 1), False, *outs)


def peer_route(q, sub_keys, layer):
    tm = 256
    n_tok = q.shape[0]
    sk = sub_keys.reshape(-1, PEER_HEADS * 2, PEER_N_KEYS, PEER_HALF)
    tiles = PEER_N_KEYS // 8
    col_out = jax.ShapeDtypeStruct((PEER_HEADS * PEER_N_KEYS, n_tok), BF16)
    col_spec = pl.BlockSpec((PEER_HEADS * PEER_N_KEYS, tm), lambda i: (0, i))
    row_out = jax.ShapeDtypeStruct((tiles, PEER_HEADS, 8, n_tok), jnp.int32)
    row_spec = pl.BlockSpec((tiles, PEER_HEADS, 8, tm), lambda i: (0, 0, 0, i))
    return pl.pallas_call(
        _route_kernel,
        out_shape=(col_out, col_out, row_out, row_out),
        grid=(n_tok // tm,),
        in_specs=[pl.BlockSpec((tm, PEER_HEADS * 2 * PEER_HALF), lambda i: (i, 0)),
                  pl.BlockSpec((None, PEER_HEADS * 2, PEER_N_KEYS, PEER_HALF), lambda i: (layer, 0, 0, 0))],
        out_specs=(col_spec, col_spec, row_spec, row_spec),
        compiler_params=_params(("parallel",), 32),
        name="peer_route",
    )(q, sk)


PEER_TE = 512
PEER_TM = 1024
PEER_CHUNK = 512
PEER_N_TILES = PEER_N_EXPERTS // PEER_TE


def _bf16_rows(word_row, n_rows):
    return pltpu.bitcast(jnp.broadcast_to(word_row, (n_rows // 2, word_row.shape[1])), BF16)


PEER_LANES = 256
PEER_ROWS = 128


def _peer_gated_slab(k, k0, bq_ref, r2_ref, aw_ref, cw_ref, ht_ref, at_ref):
    for j in range(ht_ref.shape[1] // PEER_LANES):
        lanes = slice(j * PEER_LANES, (j + 1) * PEER_LANES)
        for r in range(PEER_N_KEYS // PEER_ROWS):
            g = [None, None]
            for h in range(PEER_HEADS):
                count = _bf16_rows(cw_ref[0, h, k0 + k:k0 + k + 1, lanes], PEER_ROWS)
                a_half = _bf16_rows(aw_ref[0, h, k0 + k:k0 + k + 1, lanes], PEER_ROWS)
                rows = slice(h * PEER_N_KEYS + r * PEER_ROWS, h * PEER_N_KEYS + (r + 1) * PEER_ROWS)
                bq = bq_ref[rows, lanes]
                contrib = jnp.where(r2_ref[rows, lanes] < count, bq, jnp.zeros_like(bq)) * a_half
                g[h % 2] = contrib if g[h % 2] is None else g[h % 2] + contrib
            krows = slice(k * PEER_N_KEYS + r * PEER_ROWS, k * PEER_N_KEYS + (r + 1) * PEER_ROWS)
            hk = ht_ref[krows, lanes]
            gelu2 = hk * (1.0 + lax.erf(hk * math.sqrt(0.5)))
            at_ref[krows, lanes] = gelu2.astype(BF16) * (g[0] + g[1])


def _peer_step(x_ref, bq_ref, r2_ref, aw_ref, cw_ref, u_ref, v_ref, o_ref, at_ref, ht_write, ht_read, k0):
    keys_per_tile = PEER_TE // PEER_N_KEYS
    keys_per_chunk = PEER_CHUNK // PEER_N_KEYS
    tm = x_ref.shape[0]
    tok = tm // keys_per_tile
    u = u_ref[...].astype(BF16) if ht_write is not None else None
    for k in range(keys_per_tile):
        if ht_read is not None:
            _peer_gated_slab(k, k0, bq_ref, r2_ref, aw_ref, cw_ref, ht_read, at_ref)
        if ht_write is not None:
            ht_write[:, k * tok:(k + 1) * tok] = lax.dot_general(
                u, x_ref[k * tok:(k + 1) * tok, :], NT_DIMS, preferred_element_type=F32)
        if ht_read is not None and (k + 1) % keys_per_chunk == 0:
            chunk = slice((k + 1) * PEER_N_KEYS - PEER_CHUNK, (k + 1) * PEER_N_KEYS)
            o_ref[...] += lax.dot_general(at_ref[chunk, :], v_ref[chunk, :].astype(BF16), TN_DIMS,
                                          preferred_element_type=F32)


def _peer_kernel(x_ref, bq_ref, r2_ref, aw_ref, cw_ref, u_ref, v_ref, o_ref, ht0_ref, ht1_ref, at_ref):
    e = pl.program_id(1)
    last = PEER_N_TILES
    keys_per_tile = PEER_TE // PEER_N_KEYS
    assert 8 // keys_per_tile == 2
    step = functools.partial(_peer_step, x_ref, bq_ref, r2_ref, aw_ref, cw_ref, u_ref, v_ref, o_ref, at_ref)
    even = e % 2 == 0

    @pl.when(e == 0)
    def _():
        o_ref[...] = jnp.zeros_like(o_ref)
        step(ht0_ref, None, None)

    @pl.when(jnp.logical_not(even))
    def _():
        step(ht1_ref, ht0_ref, 0)

    @pl.when(even & (e > 0) & (e < last))
    def _():
        step(ht0_ref, ht1_ref, keys_per_tile)

    @pl.when(e == last)
    def _():
        step(None, ht1_ref, keys_per_tile)


def peer_experts(u, bq, r2, aw, cw, exp_u, exp_v, layer):
    tm, te = PEER_TM, PEER_TE
    n_tok = u.shape[0]
    assert PEER_N_TILES % 2 == 0 and n_tok % tm == 0
    tiles_per_row_block = 8 * PEER_N_KEYS // te
    col_spec = pl.BlockSpec((PEER_HEADS * PEER_N_KEYS, tm), lambda i, e: (0, i))
    row_spec = pl.BlockSpec((1, PEER_HEADS, 8, tm),
                            lambda i, e: (jnp.maximum(e - 1, 0) // tiles_per_row_block, 0, 0, i))
    return pl.pallas_call(
        _peer_kernel,
        out_shape=jax.ShapeDtypeStruct((n_tok, D_MODEL), F32),
        grid=(n_tok // tm, PEER_N_TILES + 1),
        in_specs=[pl.BlockSpec((tm, D_MODEL), lambda i, e: (i, 0)),
                  col_spec, col_spec,
                  row_spec, row_spec,
                  pl.BlockSpec((None, te, D_MODEL), lambda i, e: (layer, jnp.minimum(e, PEER_N_TILES - 1), 0)),
                  pl.BlockSpec((None, te, D_MODEL), lambda i, e: (layer, jnp.maximum(e - 1, 0), 0))],
        out_specs=pl.BlockSpec((tm, D_MODEL), lambda i, e: (i, 0)),
        scratch_shapes=[pltpu.VMEM((te, tm), F32), pltpu.VMEM((te, tm), F32), pltpu.VMEM((te, tm), BF16)],
        compiler_params=_params(("parallel", "arbitrary"), 60),
        name="peer_experts",
    )(u, bq, r2, aw, cw, exp_u, exp_v)


def peer_ffn(u, w_q, sub_keys, exp_u, exp_v, layer):
    q = matmul_resident(u, w_q, BF16, "peer_query", layer=layer)
    bq, r2, aw, cw = peer_route(q, sub_keys, layer)
    return peer_experts(u, bq, r2, aw, cw, exp_u, exp_v, layer)


def mla_mixer(u, cache_ckv, cache_krope, w_dq, q_norm, w_uq, w_dkv, kv_norm, w_uk, w_uv, w_o):
    pad = (-(MLA_Q_LORA + MLA_KV_LORA + MLA_ROPE)) % 128
    w_cat = jnp.concatenate([w_dq, w_dkv, jnp.zeros((D_MODEL, pad), F32)], axis=1)
    cq, ckv, kr = mla_down(u, w_cat, q_norm, kv_norm)
    w_uq3 = w_uq.reshape(MLA_Q_LORA, MLA_HEADS, MLA_NOPE + MLA_ROPE)
    w_q_cat = jnp.concatenate([w_uq3[:, :, :MLA_NOPE].reshape(MLA_Q_LORA, -1),
                               w_uq3[:, :, MLA_NOPE:].reshape(MLA_Q_LORA, -1)], axis=1)
    q = matmul_resident(cq, w_q_cat, BF16, "mla_q_up")
    w_nope = MLA_HEADS * MLA_NOPE
    ckv_src = jnp.concatenate([ckv, cache_ckv.reshape(-1, MLA_KV_LORA)], axis=0).astype(BF16)
    kv_up = matmul_resident(ckv_src, jnp.concatenate([w_uk, w_uv], axis=1), BF16, "mla_kv_up")
    cos, sin = (jnp.asarray(t, F32) for t in _rope_tables())
    w_rope = MLA_HEADS * MLA_ROPE
    qr_rot = rope_rows(q, N_PROMPT // 256, w_nope // w_rope, w_rope, N_SAMPLE, cos, sin, BF16)
    cos2, sin2 = (t.reshape(DEC_SEQ // 2, 128) for t in (cos[:, :MLA_ROPE], sin[:, :MLA_ROPE]))
    kr_pairs = kr[N_PROMPT:].reshape(N_SAMPLE // 2, 2 * MLA_ROPE)
    kr_rot = rope_rows(kr_pairs, 0, 0, 2 * MLA_ROPE, N_SAMPLE // 2, cos2, sin2, BF16).reshape(N_SAMPLE, MLA_ROPE)
    o_p = mla_attention_prompt(q, kv_up, kr)
    o_s = mla_attention_sample(q, qr_rot, kv_up, kr_rot, cache_krope.reshape(-1, MLA_ROPE))
    op = matmul_resident(o_p, w_o, BF16, "mla_out", x_sample=o_s)
    return op, ckv, kr


def gqa_mixer(u, cache_k, cache_v, w_qkv, sink, w_o):
    nq = GQA_HEADS * GQA_HEAD_DIM
    nk = GQA_KV_HEADS * GQA_HEAD_DIM
    q = matmul_resident(u, w_qkv, BF16, "gqa_q", n=nq)
    kv = matmul_resident(u, w_qkv, F32, "gqa_kv", col0=nq)
    cos, sin = (jnp.asarray(t, F32) for t in _rope_tables())
    q_rot = rope_rows(q, N_PROMPT // 256, 0, nq, N_SAMPLE, cos, sin, BF16)
    k_rot = rope_rows(kv, N_PROMPT // 256, 0, nk, N_SAMPLE, cos, sin, BF16)
    o_p = gqa_attention_prompt(q, kv, sink)
    o_s = gqa_attention_sample(q_rot, k_rot, kv, cache_k.reshape(-1, nk), cache_v.reshape(-1, nk), sink)
    op = matmul_resident(o_p, w_o, BF16, "gqa_out", x_sample=o_s)
    return op, kv


def fnet_mixer(u, w_out):
    f_p = fourier_real_2d(u, 0, BATCH, SEQ)
    f_s = fourier_real_2d(u, N_PROMPT // DEC_SEQ, DEC_BATCH, DEC_SEQ)
    return matmul_resident(f_p, w_out, BF16, "fnet_out", x_sample=f_s)


def conv_mixer(u, w_in, conv_w, conv_b, w_out):
    h3 = matmul(u, w_in, BF16, name="conv_in")
    y_p = gated_conv(h3, conv_w, conv_b, 0, BATCH, SEQ)
    y_s = gated_conv(h3, conv_w, conv_b, N_PROMPT // DEC_SEQ, DEC_BATCH, DEC_SEQ)
    return matmul_resident(y_p, w_out, BF16, "conv_out", x_sample=y_s)


def kernel(x_prompt, x_sample, cache_l0_ckv, cache_l0_krope, cache_l1_k, cache_l1_v, c, c_ctx, ada_w, ada_b, ln1_g, ln1_b, ln2_g, ln2_b, mla_w_dq, mla_q_norm, mla_w_uq, mla_w_dkv, mla_kv_norm, mla_w_uk, mla_w_uv, mla_w_o, gqa_w_qkv, gqa_sink, gqa_w_o, fnet_w_out, conv_w_in, conv_w, conv_b, conv_w_out, peer_w_q, peer_sub_keys, peer_u, peer_v):
    x = jnp.concatenate([x_prompt.reshape(N_PROMPT, D_MODEL), x_sample.reshape(N_SAMPLE, D_MODEL)], axis=0)
    cond = jnp.concatenate([c_ctx[None, :], c, jnp.zeros((N_COND - 1 - DEC_BATCH, D_MODEL), F32)], axis=0)
    mods_all = ada_modulation_all(cond, ada_w, ada_b).reshape(DEPTH, N_COND, 6, 1, D_MODEL)

    u = modulate_tokens(x, mods_all[0])
    new_ckv = new_krope = new_kv = None
    for i in range(DEPTH):
        mods = mods_all[i]
        if i == 0:
            delta, new_ckv, new_krope = mla_mixer(u, cache_l0_ckv, cache_l0_krope, mla_w_dq, mla_q_norm, mla_w_uq,
                                                  mla_w_dkv, mla_kv_norm, mla_w_uk, mla_w_uv, mla_w_o)
        elif i == 1:
            delta, new_kv = gqa_mixer(u, cache_l1_k, cache_l1_v, gqa_w_qkv, gqa_sink, gqa_w_o)
        elif i == 2:
            delta = fnet_mixer(u, fnet_w_out)
        else:
            delta = conv_mixer(u, conv_w_in, conv_w, conv_b, conv_w_out)
        x, u = post_norm(x, delta, mods, 2, ln1_g[i], ln1_b[i], mods, 3)
        delta = peer_ffn(u, peer_w_q, peer_sub_keys, peer_u, peer_v, i)
        if i + 1 < DEPTH:
            x, u = post_norm(x, delta, mods, 5, ln2_g[i], ln2_b[i], mods_all[i + 1], 0)
        else:
            y_prompt = post_norm(x, delta, mods, 5, ln2_g[i], ln2_b[i], rows=(0, N_PROMPT))
            y_sample = post_norm(x, delta, mods, 5, ln2_g[i], ln2_b[i], rows=(N_PROMPT, N_SAMPLE))

    nk = GQA_KV_HEADS * GQA_HEAD_DIM
    return (y_prompt.reshape(BATCH, SEQ, D_MODEL),
            y_sample.reshape(DEC_BATCH, DEC_SEQ, D_MODEL),
            new_ckv[:N_PROMPT].reshape(BATCH, SEQ, MLA_KV_LORA),
            new_krope[:N_PROMPT].reshape(BATCH, SEQ, MLA_ROPE),
            new_kv[:N_PROMPT, :nk].reshape(BATCH, SEQ, GQA_KV_HEADS, GQA_HEAD_DIM),
            new_kv[:N_PROMPT, nk:].reshape(BATCH, SEQ, GQA_KV_HEADS, GQA_HEAD_DIM))
```

```python
import functools
import math

import numpy as np
import jax
import jax.numpy as jnp
from jax import lax
from jax.experimental import pallas as pl
from jax.experimental.pallas import tpu as pltpu

F32 = jnp.float32
BF16 = jnp.bfloat16

D_MODEL = 2048
BATCH = 16
SEQ = 256
DEPTH = 4
DEC_BATCH = 2
DEC_SEQ = 1024
PAST_LEN = 256
GRID_W = 64
ROPE_BASE = 10000.0
LN_EPS = 1e-5
RMS_EPS = 1e-6
NEG_INF = -1e30
DEEPNORM_ALPHA = (2 * DEPTH) ** 0.25

MLA_HEADS = 16
MLA_Q_LORA = 512
MLA_KV_LORA = 512
MLA_NOPE = 128
MLA_ROPE = 64
MLA_V = 128
MLA_SCALE = (MLA_NOPE + MLA_ROPE) ** -0.5

GQA_HEADS = 32
GQA_KV_HEADS = 8
GQA_HEAD_DIM = 64
WINDOW = 128
GQA_SCALE = GQA_HEAD_DIM ** -0.5

FNET_GROUPS = 4
FNET_GROUP_DIM = D_MODEL // FNET_GROUPS

PEER_HEADS = 8
PEER_N_KEYS = 128
PEER_N_EXPERTS = PEER_N_KEYS * PEER_N_KEYS
PEER_HALF = 128
PEER_TOPK = 16

N_PROMPT = BATCH * SEQ
N_SAMPLE = DEC_BATCH * DEC_SEQ
N_TOK = N_PROMPT + N_SAMPLE
N_COND = 8

MIB = 1024 * 1024

NT_DIMS = (((1,), (1,)), ((), ()))
TN_DIMS = (((0,), (0,)), ((), ()))


def _params(sem, vmem_mib, flags=None):
    return pltpu.CompilerParams(dimension_semantics=sem, vmem_limit_bytes=vmem_mib * MIB, flags=flags)


def _group_of_tile(i, tm):
    return jnp.maximum((i * tm) // DEC_SEQ - (N_PROMPT // DEC_SEQ - 1), 0)


def _mod_spec(tm, which):
    return pl.BlockSpec((None, None, 1, D_MODEL), lambda i: (_group_of_tile(i, tm), which, 0, 0))


def _ada_kernel(c_ref, w_ref, b_ref, o_ref):
    c = c_ref[...]
    s = (c * jax.nn.sigmoid(c)).astype(BF16)
    o_ref[...] = jnp.dot(s, w_ref[...].astype(BF16), preferred_element_type=F32) + b_ref[...]


def ada_modulation_all(cond, ada_w, ada_b):
    tn = 1024
    n_out = 6 * D_MODEL
    return pl.pallas_call(
        _ada_kernel,
        out_shape=jax.ShapeDtypeStruct((DEPTH, N_COND, n_out), F32),
        grid=(DEPTH, n_out // tn),
        in_specs=[
            pl.BlockSpec((N_COND, D_MODEL), lambda l, j: (0, 0)),
            pl.BlockSpec((None, D_MODEL, tn), lambda l, j: (l, 0, j)),
            pl.BlockSpec((None, 1, tn), lambda l, j: (l, 0, j)),
        ],
        out_specs=pl.BlockSpec((None, N_COND, tn), lambda l, j: (l, 0, j)),
        compiler_params=_params(("parallel", "parallel"), 40),
        name="ada_modulation",
    )(cond, ada_w, ada_b.reshape(DEPTH, 1, n_out))


def _two_source_specs(tm, prompt_tiles):
    return [pl.BlockSpec((tm, D_MODEL), lambda i: (jnp.minimum(i, prompt_tiles - 1), 0)),
            pl.BlockSpec((tm, D_MODEL), lambda i: (jnp.maximum(i - prompt_tiles, 0), 0))]


def _for_token_group(prompt_tiles, xp_ref, xs_ref, body):
    i = pl.program_id(0)

    @pl.when(i < prompt_tiles)
    def _():
        body(xp_ref)

    @pl.when(i >= prompt_tiles)
    def _():
        body(xs_ref)


def _modulate_kernel(xp_ref, xs_ref, sh_ref, sc_ref, u_ref, *, prompt_tiles):
    def body(x_ref):
        u_ref[...] = (x_ref[...] * (1.0 + sc_ref[...]) + sh_ref[...]).astype(u_ref.dtype)

    _for_token_group(prompt_tiles, xp_ref, xs_ref, body)


def modulate_tokens(x_prompt, x_sample, mods):
    tm = 512
    prompt_tiles = N_PROMPT // tm
    return pl.pallas_call(
        functools.partial(_modulate_kernel, prompt_tiles=prompt_tiles),
        out_shape=jax.ShapeDtypeStruct((N_TOK, D_MODEL), BF16),
        grid=(N_TOK // tm,),
        in_specs=_two_source_specs(tm, prompt_tiles) + [_mod_spec(tm, 0), _mod_spec(tm, 1)],
        out_specs=pl.BlockSpec((tm, D_MODEL), lambda i: (i, 0)),
        compiler_params=_params(("parallel",), 40),
        name="modulate",
    )(x_prompt, x_sample, mods, mods)


def _layer_norm_rows(y, g, b):
    mu = jnp.mean(y, axis=-1, keepdims=True)
    yc = y - mu
    var = jnp.mean(yc * yc, axis=-1, keepdims=True)
    return yc * lax.rsqrt(var + LN_EPS) * g + b


def _post_norm_kernel(x_ref, d_ref, gate_ref, g_ref, b_ref, xo_ref):
    y = DEEPNORM_ALPHA * x_ref[...] + gate_ref[...] * d_ref[...].astype(F32)
    xo_ref[...] = _layer_norm_rows(y, g_ref[...], b_ref[...])


def _post_norm_mod_kernel(x_ref, d_ref, gate_ref, g_ref, b_ref, sh_ref, sc_ref, xo_ref, uo_ref):
    y = DEEPNORM_ALPHA * x_ref[...] + gate_ref[...] * d_ref[...].astype(F32)
    xn = _layer_norm_rows(y, g_ref[...], b_ref[...])
    xo_ref[...] = xn
    uo_ref[...] = (xn * (1.0 + sc_ref[...]) + sh_ref[...]).astype(uo_ref.dtype)


def _post_norm_mod_groups_kernel(xp_ref, xs_ref, *rest, prompt_tiles):
    _for_token_group(prompt_tiles, xp_ref, xs_ref, lambda x_ref: _post_norm_mod_kernel(x_ref, *rest))


def post_norm(x, delta, mods, gate_idx, g, b, next_mods=None, next_idx=None, rows=(0, N_TOK), x_sample=None):
    tm = 512
    row = pl.BlockSpec((tm, D_MODEL), lambda i: (i, 0))
    vec = pl.BlockSpec((1, D_MODEL), lambda i: (0, 0))
    tail = [delta, mods, g.reshape(1, D_MODEL), b.reshape(1, D_MODEL)]
    tail_specs = [row, _mod_spec(tm, gate_idx), vec, vec]
    if x_sample is not None:
        prompt_tiles = N_PROMPT // tm
        tail += [next_mods, next_mods]
        tail_specs += [_mod_spec(tm, next_idx), _mod_spec(tm, next_idx + 1)]
        return pl.pallas_call(
            functools.partial(_post_norm_mod_groups_kernel, prompt_tiles=prompt_tiles),
            out_shape=(jax.ShapeDtypeStruct((N_TOK, D_MODEL), F32), jax.ShapeDtypeStruct((N_TOK, D_MODEL), BF16)),
            grid=(N_TOK // tm,), in_specs=_two_source_specs(tm, prompt_tiles) + tail_specs, out_specs=(row, row),
            compiler_params=_params(("parallel",), 48), name="post_norm_mod",
        )(x, x_sample, *tail)
    ins = [x] + tail
    specs = [row] + tail_specs
    if next_mods is None:
        i0 = rows[0] // tm
        row_in = pl.BlockSpec((tm, D_MODEL), lambda i: (i0 + i, 0))
        gate = pl.BlockSpec((None, None, 1, D_MODEL), lambda i: (_group_of_tile(i0 + i, tm), gate_idx, 0, 0))
        return pl.pallas_call(
            _post_norm_kernel,
            out_shape=jax.ShapeDtypeStruct((rows[1], D_MODEL), F32),
            grid=(rows[1] // tm,), in_specs=[row_in, row_in, gate, vec, vec], out_specs=row,
            compiler_params=_params(("parallel",), 40), name="post_norm",
        )(*ins)
    ins += [next_mods, next_mods]
    specs += [_mod_spec(tm, next_idx), _mod_spec(tm, next_idx + 1)]
    return pl.pallas_call(
        _post_norm_mod_kernel,
        out_shape=(jax.ShapeDtypeStruct((N_TOK, D_MODEL), F32), jax.ShapeDtypeStruct((N_TOK, D_MODEL), BF16)),
        grid=(N_TOK // tm,), in_specs=specs, out_specs=(row, row),
        compiler_params=_params(("parallel",), 40), name="post_norm_mod",
    )(*ins)


def _mm_kernel(x_ref, w_ref, o_ref):
    o_ref[...] = jnp.dot(x_ref[...], w_ref[...].astype(BF16), preferred_element_type=F32).astype(o_ref.dtype)


def _mm_resident_kernel(*refs, prompt_tiles):
    if prompt_tiles is None:
        x_ref, w_ref, o_ref, wb_ref = refs
    else:
        x_ref, xs_ref, w_ref, o_ref, wb_ref = refs
    i = pl.program_id(0)

    @pl.when(i == 0)
    def _():
        wb_ref[...] = w_ref[...].astype(BF16)

    if prompt_tiles is None:
        o_ref[...] = jnp.dot(x_ref[...], wb_ref[...], preferred_element_type=F32).astype(o_ref.dtype)
    else:
        @pl.when(i < prompt_tiles)
        def _():
            o_ref[...] = jnp.dot(x_ref[...], wb_ref[...], preferred_element_type=F32).astype(o_ref.dtype)

        @pl.when(i >= prompt_tiles)
        def _():
            o_ref[...] = jnp.dot(xs_ref[...], wb_ref[...], preferred_element_type=F32).astype(o_ref.dtype)


def matmul_resident(x, w, out_dtype, name, x_sample=None, layer=None, col0=0, n=None):
    tm = 512
    k = x.shape[1]
    n = w.shape[-1] - col0 if n is None else n
    assert col0 % n == 0
    j0 = col0 // n
    rows = x.shape[0] + (0 if x_sample is None else x_sample.shape[0])
    once = pl.Buffered(1)
    if layer is None:
        w_spec = pl.BlockSpec((k, n), lambda i: (0, j0), pipeline_mode=once)
    else:
        w_spec = pl.BlockSpec((None, k, n), lambda i: (layer, 0, j0), pipeline_mode=once)
    if x_sample is None:
        prompt_tiles = None
        xs, x_specs = (x,), [pl.BlockSpec((tm, k), lambda i: (i, 0))]
    else:
        prompt_tiles = x.shape[0] // tm
        xs = (x, x_sample)
        x_specs = [pl.BlockSpec((tm, k), lambda i: (jnp.minimum(i, prompt_tiles - 1), 0)),
                   pl.BlockSpec((tm, k), lambda i: (jnp.maximum(i - prompt_tiles, 0), 0))]
    return pl.pallas_call(
        functools.partial(_mm_resident_kernel, prompt_tiles=prompt_tiles),
        out_shape=jax.ShapeDtypeStruct((rows, n), out_dtype),
        grid=(rows // tm,),
        in_specs=x_specs + [w_spec],
        out_specs=pl.BlockSpec((tm, n), lambda i: (i, 0)),
        scratch_shapes=[pltpu.VMEM((k, n), BF16)],
        compiler_params=_params(("arbitrary",), 48),
        name=name,
    )(*xs, w)


def matmul(x, w, out_dtype, tm=2048, tn=512, name="matmul", layer=None, col0=0, n=None):
    m, k = x.shape
    n = w.shape[-1] - col0 if n is None else n
    tm = min(tm, m)
    tn = min(tn, n)
    assert m % tm == 0 and n % tn == 0 and col0 % tn == 0, (m, n, tm, tn, col0)
    j0 = col0 // tn
    if layer is None:
        w_spec = pl.BlockSpec((k, tn), lambda i, j: (0, j0 + j))
    else:
        w_spec = pl.BlockSpec((None, k, tn), lambda i, j: (layer, 0, j0 + j))
    return pl.pallas_call(
        _mm_kernel,
        out_shape=jax.ShapeDtypeStruct((m, n), out_dtype),
        grid=(m // tm, n // tn),
        in_specs=[pl.BlockSpec((tm, k), lambda i, j: (i, 0)), w_spec],
        out_specs=pl.BlockSpec((tm, tn), lambda i, j: (i, j)),
        compiler_params=_params(("parallel", "parallel"), 48),
        name=name,
    )(x, w)


def _rms_rows(y, g):
    return y * lax.rsqrt(jnp.mean(y * y, axis=-1, keepdims=True) + RMS_EPS) * g


def _mla_down_kernel(x_ref, w_ref, qn_ref, kvn_ref, cq_ref, ckv_ref, kr_ref):
    y = jnp.dot(x_ref[...], w_ref[...].astype(BF16), preferred_element_type=F32)
    cq_ref[...] = _rms_rows(y[:, :MLA_Q_LORA], qn_ref[...]).astype(cq_ref.dtype)
    ckv_ref[...] = _rms_rows(y[:, MLA_Q_LORA:MLA_Q_LORA + MLA_KV_LORA], kvn_ref[...])
    kr_ref[...] = y[:, MLA_Q_LORA + MLA_KV_LORA:MLA_Q_LORA + MLA_KV_LORA + MLA_ROPE]


def mla_down(u, w_cat, q_norm, kv_norm):
    tm = 512
    n = w_cat.shape[1]
    return pl.pallas_call(
        _mla_down_kernel,
        out_shape=(jax.ShapeDtypeStruct((N_TOK, MLA_Q_LORA), BF16),
                   jax.ShapeDtypeStruct((N_TOK, MLA_KV_LORA), F32),
                   jax.ShapeDtypeStruct((N_TOK, MLA_ROPE), F32)),
        grid=(N_TOK // tm,),
        in_specs=[pl.BlockSpec((tm, D_MODEL), lambda i: (i, 0)),
                  pl.BlockSpec((D_MODEL, n), lambda i: (0, 0)),
                  pl.BlockSpec((1, MLA_Q_LORA), lambda i: (0, 0)),
                  pl.BlockSpec((1, MLA_KV_LORA), lambda i: (0, 0))],
        out_specs=(pl.BlockSpec((tm, MLA_Q_LORA), lambda i: (i, 0)),
                   pl.BlockSpec((tm, MLA_KV_LORA), lambda i: (i, 0)),
                   pl.BlockSpec((tm, MLA_ROPE), lambda i: (i, 0))),
        compiler_params=_params(("parallel",), 48),
        name="mla_down",
    )(u, w_cat, q_norm.reshape(1, -1), kv_norm.reshape(1, -1))


def _rope_tables():
    t = np.arange(DEC_SEQ)
    quarter = MLA_ROPE // 4
    inv_freq = ROPE_BASE ** (-np.arange(quarter, dtype=np.float64) / quarter)
    ang_row = (t // GRID_W)[:, None] * inv_freq[None, :]
    ang_col = (t % GRID_W)[:, None] * inv_freq[None, :]
    cos = np.concatenate([np.cos(ang_row)] * 2 + [np.cos(ang_col)] * 2, -1)
    sin = np.concatenate([-np.sin(ang_row), np.sin(ang_row), -np.sin(ang_col), np.sin(ang_col)], -1)
    return np.tile(cos, (1, 2)), np.tile(sin, (1, 2))


def _rope_kernel(x_ref, c_ref, s_ref, o_ref):
    x = x_ref[...].astype(F32)
    w = x.shape[-1]
    reps = w // c_ref.shape[-1]
    cos = jnp.tile(c_ref[...], (1, reps))
    sin = jnp.tile(s_ref[...], (1, reps))
    lane = lax.broadcasted_iota(jnp.int32, x.shape, 1)
    partner = jnp.where((lane % 32) < 16, pltpu.roll(x, w - 16, 1), pltpu.roll(x, 16, 1))
    o_ref[...] = (x * cos + partner * sin).astype(o_ref.dtype)


def rope_rows(x, row_block0, col_block, width, n_rows, cos, sin, out_dtype, tr=256):
    t_blocks = cos.shape[0] // tr
    return pl.pallas_call(
        _rope_kernel,
        out_shape=jax.ShapeDtypeStruct((n_rows, width), out_dtype),
        grid=(n_rows // tr,),
        in_specs=[pl.BlockSpec((tr, width), lambda i: (row_block0 + i, col_block)),
                  pl.BlockSpec((tr, 128), lambda i: (i % t_blocks, 0)),
                  pl.BlockSpec((tr, 128), lambda i: (i % t_blocks, 0))],
        out_specs=pl.BlockSpec((tr, width), lambda i: (i, 0)),
        compiler_params=_params(("parallel",), 32),
        name="axial_rope",
    )(x, cos, sin)


def _softmax_parts(parts, sink):
    m = parts[0].max(axis=-1, keepdims=True)
    for s in parts[1:]:
        m = jnp.maximum(m, s.max(axis=-1, keepdims=True))
    if sink is not None:
        m = jnp.maximum(m, sink)
    ps = [jnp.exp(s - m) for s in parts]
    l = ps[0].sum(axis=-1, keepdims=True)
    for p in ps[1:]:
        l = l + p.sum(axis=-1, keepdims=True)
    if sink is not None:
        l = l + jnp.exp(sink - m)
    return ps, 1.0 / l


def _mla_attn_kernel(*refs, with_ctx):
    if with_ctx:
        qn_ref, qr_ref, kn_ref, v_ref, kr_ref, kn2_ref, v2_ref, kr2_ref, o_ref = refs
    else:
        qn_ref, qr_ref, kn_ref, v_ref, kr_ref, o_ref = refs
    kr = kr_ref[...].astype(BF16)
    kr2 = kr2_ref[...].astype(BF16) if with_ctx else None
    for h in range(MLA_HEADS):
        n0, n1 = h * MLA_NOPE, (h + 1) * MLA_NOPE
        qn = qn_ref[:, n0:n1]
        qr = qr_ref[:, h * MLA_ROPE:(h + 1) * MLA_ROPE]
        s = (lax.dot_general(qn, kn_ref[:, n0:n1], NT_DIMS, preferred_element_type=F32)
             + lax.dot_general(qr, kr, NT_DIMS, preferred_element_type=F32)) * MLA_SCALE
        parts = [s]
        if with_ctx:
            s2 = (lax.dot_general(qn, kn2_ref[:, n0:n1], NT_DIMS, preferred_element_type=F32)
                  + lax.dot_general(qr, kr2, NT_DIMS, preferred_element_type=F32)) * MLA_SCALE
            parts.append(s2)
        ps, inv_l = _softmax_parts(parts, None)
        o = jnp.dot(ps[0].astype(BF16), v_ref[:, n0:n1], preferred_element_type=F32)
        if with_ctx:
            o = o + jnp.dot(ps[1].astype(BF16), v2_ref[:, n0:n1], preferred_element_type=F32)
        o_ref[:, n0:n1] = (o * inv_l).astype(o_ref.dtype)


def mla_attention_prompt(q, kv_up, kr):
    w = MLA_HEADS * MLA_NOPE
    return pl.pallas_call(
        functools.partial(_mla_attn_kernel, with_ctx=False),
        out_shape=jax.ShapeDtypeStruct((N_PROMPT, w), BF16),
        grid=(BATCH,),
        in_specs=[pl.BlockSpec((SEQ, w), lambda b: (b, 0)),
                  pl.BlockSpec((SEQ, MLA_HEADS * MLA_ROPE), lambda b: (b, w // (MLA_HEADS * MLA_ROPE))),
                  pl.BlockSpec((SEQ, w), lambda b: (b, 0)),
                  pl.BlockSpec((SEQ, w), lambda b: (b, 1)),
                  pl.BlockSpec((SEQ, MLA_ROPE), lambda b: (b, 0))],
        out_specs=pl.BlockSpec((SEQ, w), lambda b: (b, 0)),
        compiler_params=_params(("parallel",), 32),
        name="mla_attention_prompt",
    )(q, q, kv_up, kv_up, kr)


def mla_attention_sample(qn, qr_rot, kv_up, kr_rot, kr_cache):
    w = MLA_HEADS * MLA_NOPE
    tq = 256
    nq = DEC_SEQ // tq
    q0 = N_PROMPT // tq
    lat0 = N_PROMPT // DEC_SEQ
    ctx0 = N_TOK // PAST_LEN
    return pl.pallas_call(
        functools.partial(_mla_attn_kernel, with_ctx=True),
        out_shape=jax.ShapeDtypeStruct((N_SAMPLE, w), BF16),
        grid=(DEC_BATCH, nq),
        in_specs=[pl.BlockSpec((tq, w), lambda b, i: (q0 + b * nq + i, 0)),
                  pl.BlockSpec((tq, MLA_HEADS * MLA_ROPE), lambda b, i: (b * nq + i, 0)),
                  pl.BlockSpec((DEC_SEQ, w), lambda b, i: (lat0 + b, 0)),
                  pl.BlockSpec((DEC_SEQ, w), lambda b, i: (lat0 + b, 1)),
                  pl.BlockSpec((DEC_SEQ, MLA_ROPE), lambda b, i: (b, 0)),
                  pl.BlockSpec((PAST_LEN, w), lambda b, i: (ctx0 + b, 0)),
                  pl.BlockSpec((PAST_LEN, w), lambda b, i: (ctx0 + b, 1)),
                  pl.BlockSpec((PAST_LEN, MLA_ROPE), lambda b, i: (b, 0))],
        out_specs=pl.BlockSpec((tq, w), lambda b, i: (b * nq + i, 0)),
        compiler_params=_params(("parallel", "parallel"), 56),
        name="mla_attention_sample",
    )(qn, qr_rot, kv_up, kv_up, kr_rot, kv_up, kv_up, kr_cache)


def _gqa_attn_kernel(*refs, with_ctx, tq):
    if with_ctx:
        sink_ref, q_ref, k_ref, v_ref, k2_ref, v2_ref, o_ref = refs
    else:
        sink_ref, q_ref, k_ref, v_ref, o_ref = refs
    group = GQA_HEADS // GQA_KV_HEADS
    d = GQA_HEAD_DIM
    if with_ctx:
        t = pl.program_id(1) * tq + lax.broadcasted_iota(jnp.int32, (tq, DEC_SEQ), 0)
        s_pos = lax.broadcasted_iota(jnp.int32, (tq, DEC_SEQ), 1)
        in_window = jnp.abs(t - s_pos) <= WINDOW
    for hk in range(GQA_KV_HEADS):
        k = k_ref[:, hk * d:(hk + 1) * d].astype(BF16)
        v = v_ref[:, hk * d:(hk + 1) * d].astype(BF16)
        if with_ctx:
            k2 = k2_ref[:, hk * d:(hk + 1) * d].astype(BF16)
            v2 = v2_ref[:, hk * d:(hk + 1) * d].astype(BF16)
        for g in range(group):
            h = hk * group + g
            q = q_ref[:, h * d:(h + 1) * d]
            s = lax.dot_general(q, k, NT_DIMS, preferred_element_type=F32) * GQA_SCALE
            parts = [s]
            if with_ctx:
                parts = [jnp.where(in_window, s, NEG_INF),
                         lax.dot_general(q, k2, NT_DIMS, preferred_element_type=F32) * GQA_SCALE]
            ps, inv_l = _softmax_parts(parts, sink_ref[h])
            o = jnp.dot(ps[0].astype(BF16), v, preferred_element_type=F32)
            if with_ctx:
                o = o + jnp.dot(ps[1].astype(BF16), v2, preferred_element_type=F32)
            o_ref[:, h * d:(h + 1) * d] = (o * inv_l).astype(o_ref.dtype)


def gqa_attention_prompt(q, kv, sink):
    wq = GQA_HEADS * GQA_HEAD_DIM
    wk = GQA_KV_HEADS * GQA_HEAD_DIM
    return pl.pallas_call(
        functools.partial(_gqa_attn_kernel, with_ctx=False, tq=SEQ),
        out_shape=jax.ShapeDtypeStruct((N_PROMPT, wq), BF16),
        grid=(BATCH,),
        in_specs=[pl.BlockSpec(memory_space=pltpu.SMEM),
                  pl.BlockSpec((SEQ, wq), lambda b: (b, 0)),
                  pl.BlockSpec((SEQ, wk), lambda b: (b, 0)),
                  pl.BlockSpec((SEQ, wk), lambda b: (b, 1))],
        out_specs=pl.BlockSpec((SEQ, wq), lambda b: (b, 0)),
        compiler_params=_params(("parallel",), 32),
        name="gqa_attention_prompt",
    )(sink, q, kv, kv)


def gqa_attention_sample(q_rot, k_rot, kv, k_cache, v_cache, sink):
    wq = GQA_HEADS * GQA_HEAD_DIM
    wk = GQA_KV_HEADS * GQA_HEAD_DIM
    tq = 256
    nq = DEC_SEQ // tq
    lat0 = N_PROMPT // DEC_SEQ
    return pl.pallas_call(
        functools.partial(_gqa_attn_kernel, with_ctx=True, tq=tq),
        out_shape=jax.ShapeDtypeStruct((N_SAMPLE, wq), BF16),
        grid=(DEC_BATCH, nq),
        in_specs=[pl.BlockSpec(memory_space=pltpu.SMEM),
                  pl.BlockSpec((tq, wq), lambda b, i: (b * nq + i, 0)),
                  pl.BlockSpec((DEC_SEQ, wk), lambda b, i: (b, 0)),
                  pl.BlockSpec((DEC_SEQ, wk), lambda b, i: (lat0 + b, 1)),
                  pl.BlockSpec((PAST_LEN, wk), lambda b, i: (b, 0)),
                  pl.BlockSpec((PAST_LEN, wk), lambda b, i: (b, 0))],
        out_specs=pl.BlockSpec((tq, wq), lambda b, i: (b * nq + i, 0)),
        compiler_params=_params(("parallel", "parallel"), 56),
        name="gqa_attention_sample",
    )(sink, q_rot, k_rot, kv, k_cache, v_cache)


def _dft_tables(n):
    jk = (np.arange(n)[:, None] * np.arange(n)[None, :]) % n
    ang = 2.0 * np.pi * jk / n
    return np.cos(ang) / math.sqrt(n), np.sin(ang) / math.sqrt(n)


def _fnet_kernel(x_ref, ct_ref, st_ref, cc_ref, sc_ref, o_ref):
    x = x_ref[...]
    y1 = jnp.dot(x, cc_ref[...], preferred_element_type=F32).astype(BF16)
    y2 = jnp.dot(x, sc_ref[...], preferred_element_type=F32).astype(BF16)
    o = (jnp.dot(ct_ref[...], y1, preferred_element_type=F32)
         - jnp.dot(st_ref[...], y2, preferred_element_type=F32))
    o_ref[...] = o.astype(o_ref.dtype)


def fourier_real_2d(u, seq_block0, n_seq, t):
    ct, st = (jnp.asarray(a, BF16) for a in _dft_tables(t))
    cc, sc = (jnp.asarray(a, BF16) for a in _dft_tables(FNET_GROUP_DIM))
    gd = FNET_GROUP_DIM
    return pl.pallas_call(
        _fnet_kernel,
        out_shape=jax.ShapeDtypeStruct((n_seq * t, D_MODEL), BF16),
        grid=(n_seq, FNET_GROUPS),
        in_specs=[pl.BlockSpec((t, gd), lambda b, g: (seq_block0 + b, g)),
                  pl.BlockSpec((t, t), lambda b, g: (0, 0)),
                  pl.BlockSpec((t, t), lambda b, g: (0, 0)),
                  pl.BlockSpec((gd, gd), lambda b, g: (0, 0)),
                  pl.BlockSpec((gd, gd), lambda b, g: (0, 0))],
        out_specs=pl.BlockSpec((t, gd), lambda b, g: (b, g)),
        compiler_params=_params(("parallel", "parallel"), 32),
        name="fourier_mix",
    )(u, ct, st, cc, sc)


def _conv_kernel(b_ref, c_ref, h_ref, w_ref, cb_ref, o_ref):
    z = c_ref[...].astype(F32) * h_ref[...].astype(F32)
    t = z.shape[0]
    row = lax.broadcasted_iota(jnp.int32, z.shape, 0)
    z_prev = jnp.where(row == 0, 0.0, pltpu.roll(z, 1, 0))
    z_next = jnp.where(row == t - 1, 0.0, pltpu.roll(z, t - 1, 0))
    conv = z_prev * w_ref[0:1, :] + z * w_ref[1:2, :] + z_next * w_ref[2:3, :] + cb_ref[...]
    o_ref[...] = (b_ref[...].astype(F32) * conv).astype(o_ref.dtype)


def gated_conv(h3, conv_w, conv_b, seq_block0, n_seq, t):
    tn = 512
    nj = D_MODEL // tn
    return pl.pallas_call(
        _conv_kernel,
        out_shape=jax.ShapeDtypeStruct((n_seq * t, D_MODEL), BF16),
        grid=(n_seq, nj),
        in_specs=[pl.BlockSpec((t, tn), lambda s, j: (seq_block0 + s, j)),
                  pl.BlockSpec((t, tn), lambda s, j: (seq_block0 + s, nj + j)),
                  pl.BlockSpec((t, tn), lambda s, j: (seq_block0 + s, 2 * nj + j)),
                  pl.BlockSpec((3, tn), lambda s, j: (0, j)),
                  pl.BlockSpec((1, tn), lambda s, j: (0, j))],
        out_specs=pl.BlockSpec((t, tn), lambda s, j: (s, j)),
        compiler_params=_params(("parallel", "parallel"), 32),
        name="gated_conv",
    )(h3, h3, h3, conv_w, conv_b.reshape(1, D_MODEL))


_N_RANKS = PEER_TOPK + 1
_CAND_PAIRS = [(a, b) for a in range(_N_RANKS) for b in range(_N_RANKS) if (a + 1) * (b + 1) <= _N_RANKS]


_NO_RANK = 127.0


def _top_values_exact(s, k, want_rank):
    n = s.shape[0]
    idx = lax.broadcasted_iota(jnp.int32, s.shape, 0)
    rank = jnp.full(s.shape, _NO_RANK, F32) if want_rank else None
    out = []
    for it in range(k):
        m = jnp.max(s, axis=0, keepdims=True)
        out.append(m)
        first = jnp.min(jnp.where(s == m, idx, n), axis=0, keepdims=True)
        hit = idx == first
        s = jnp.where(hit, -jnp.inf, s)
        if want_rank:
            rank = jnp.where(hit, float(it), rank)
    return out, rank


def _all_equal(x, value):
    return jnp.min(jnp.where(x == value, 1.0, 0.0)) > 0.5


def _top_values_distinct(s, k, n_pad):
    out = []
    for it in range(k):
        m = jnp.max(s, axis=0, keepdims=True)
        out.append(m)
        s = jnp.where(s == m, -jnp.inf, s)
    removed = jnp.sum(jnp.where(s == -jnp.inf, 1.0, 0.0), axis=0, keepdims=True)
    return out, _all_equal(removed, float(k + n_pad))


def _batcher_network(n):
    def merge(lo, hi, r):
        step = r * 2
        if step < hi - lo:
            yield from merge(lo, hi, step)
            yield from merge(lo + r, hi, step)
            yield from [(i, i + r) for i in range(lo + r, hi - r, step)]
        else:
            yield (lo, lo + r)

    def sort(lo, hi):
        if hi - lo >= 1:
            mid = lo + (hi - lo) // 2
            yield from sort(lo, mid)
            yield from sort(mid + 1, hi)
            yield from merge(lo, hi, 1)

    return list(sort(0, n - 1))


_SUBLANES = 8
_KEY_TILES = PEER_N_KEYS // _SUBLANES
_SORT_KEY_TILES = _batcher_network(_KEY_TILES)


def _top_values_sorted(s, k):
    tiles = [s[j * _SUBLANES:(j + 1) * _SUBLANES, :] for j in range(_KEY_TILES)]
    for i, j in _SORT_KEY_TILES:
        tiles[i], tiles[j] = jnp.maximum(tiles[i], tiles[j]), jnp.minimum(tiles[i], tiles[j])
    out = []
    pops = jnp.zeros_like(tiles[0])
    for it in range(k):
        m = jnp.max(tiles[0], axis=0, keepdims=True)
        out.append(m)
        hit = tiles[0] == m
        pops = pops + jnp.where(hit, 1.0, 0.0)
        depth = k - 1 - it
        for j in range(min(depth, _KEY_TILES - 1)):
            tiles[j] = jnp.where(hit, tiles[j + 1], tiles[j])
        if depth >= _KEY_TILES - 1:
            tiles[-1] = jnp.where(hit, -jnp.inf, tiles[-1])
    return out, _all_equal(jnp.sum(pops, axis=0, keepdims=True), float(k))


def _ranks_from_values(s, values):
    rank = jnp.full(s.shape, _NO_RANK, F32)
    for r in reversed(range(len(values))):
        rank = jnp.where(s >= values[r], float(r), rank)
    return rank


def _dup_bf16_words(x):
    bits = pltpu.bitcast(x.astype(BF16).astype(F32), jnp.int32)
    return bits | lax.shift_right_logical(bits, jnp.int32(16))


def _route_head(h, s1, s2, fast, bq_ref, r2_ref, aw_ref, cw_ref):
    tm = s1.shape[1]
    if fast:
        v1, ok1 = _top_values_sorted(s1, _N_RANKS)
        v2, ok2 = _top_values_sorted(s2, _N_RANKS)
        rank2 = _ranks_from_values(s2, v2[:PEER_TOPK])
    else:
        v1, _ = _top_values_exact(s1, _N_RANKS, False)
        v2, rank2 = _top_values_exact(s2, _N_RANKS, True)
    n_pad = (-len(_CAND_PAIRS)) % 8
    n_cand = len(_CAND_PAIRS) + n_pad
    cand_row = lax.broadcasted_iota(jnp.int32, (n_cand, tm), 0)
    cand = jnp.full((n_cand, tm), -jnp.inf, F32)
    for r, (a, b) in enumerate(_CAND_PAIRS):
        cand = jnp.where(cand_row == r, v1[a] + v2[b], cand)
    if fast:
        top, ok3 = _top_values_distinct(cand, _N_RANKS, n_pad)
    else:
        top, _ = _top_values_exact(cand, _N_RANKS, False)
    tau = 0.5 * (top[PEER_TOPK - 1] + top[PEER_TOPK])
    z = jnp.exp(top[0] - top[0])
    for kk in range(1, PEER_TOPK):
        z = z + jnp.exp(top[kk] - top[0])
    inv_z = 1.0 / z
    count = jnp.zeros_like(s1)
    for b in range(PEER_TOPK):
        count = jnp.where(s1 >= tau - v2[b], float(b + 1), count)
    rows = slice(h * PEER_N_KEYS, (h + 1) * PEER_N_KEYS)
    bq_ref[rows, :] = (jnp.exp(s2 - v2[0]) * inv_z).astype(BF16)
    r2_ref[rows, :] = rank2.astype(BF16)
    a_half = 0.5 * jnp.exp(s1 - v1[0])
    aw_ref[:, h, :, :] = _dup_bf16_words(a_half).reshape(PEER_N_KEYS // 8, 8, tm)
    cw_ref[:, h, :, :] = _dup_bf16_words(count).reshape(PEER_N_KEYS // 8, 8, tm)
    return jnp.logical_and(jnp.logical_and(ok1, ok2), ok3) if fast else None


def _route_kernel(q_ref, sk_ref, bq_ref, r2_ref, aw_ref, cw_ref):
    outs = (bq_ref, r2_ref, aw_ref, cw_ref)

    def scores(hc):
        return lax.dot_general(sk_ref[hc].astype(BF16), q_ref[:, hc * PEER_HALF:(hc + 1) * PEER_HALF],
                               NT_DIMS, preferred_element_type=F32)

    no_ties = [_route_head(h, scores(2 * h), scores(2 * h + 1), True, *outs) for h in range(PEER_HEADS)]
    for h in range(PEER_HEADS):
        @pl.when(jnp.logical_not(no_ties[h]))
        def _():
            _route_head(h, scores(2 * h), scores(2 * h + 1), False, *outs)


def peer_route(q, sub_keys, layer):
    tm = 256
    n_tok = q.shape[0]
    sk = sub_keys.reshape(-1, PEER_HEADS * 2, PEER_N_KEYS, PEER_HALF)
    tiles = PEER_N_KEYS // 8
    col_out = jax.ShapeDtypeStruct((PEER_HEADS * PEER_N_KEYS, n_tok), BF16)
    col_spec = pl.BlockSpec((PEER_HEADS * PEER_N_KEYS, tm), lambda i: (0, i))
    row_out = jax.ShapeDtypeStruct((tiles, PEER_HEADS, 8, n_tok), jnp.int32)
    row_spec = pl.BlockSpec((tiles, PEER_HEADS, 8, tm), lambda i: (0, 0, 0, i))
    return pl.pallas_call(
        _route_kernel,
        out_shape=(col_out, col_out, row_out, row_out),
        grid=(n_tok // tm,),
        in_specs=[pl.BlockSpec((tm, PEER_HEADS * 2 * PEER_HALF), lambda i: (i, 0)),
                  pl.BlockSpec((None, PEER_HEADS * 2, PEER_N_KEYS, PEER_HALF), lambda i: (layer, 0, 0, 0))],
        out_specs=(col_spec, col_spec, row_spec, row_spec),
        compiler_params=_params(("parallel",), 32),
        name="peer_route",
    )(q, sk)


PEER_TE = 512
PEER_TM = 1024
PEER_CHUNK = 512
PEER_N_TILES = PEER_N_EXPERTS // PEER_TE


def _bf16_rows(word_row, n_rows):
    return pltpu.bitcast(jnp.broadcast_to(word_row, (n_rows // 2, word_row.shape[1])), BF16)


PEER_LANES = 256
PEER_ROWS = 128


def _peer_gated_slab(k, k0, bq_ref, r2_ref, aw_ref, cw_ref, ht_ref, at_ref):
    for j in range(ht_ref.shape[1] // PEER_LANES):
        lanes = slice(j * PEER_LANES, (j + 1) * PEER_LANES)
        for r in range(PEER_N_KEYS // PEER_ROWS):
            g = [None, None]
            for h in range(PEER_HEADS):
                count = _bf16_rows(cw_ref[0, h, k0 + k:k0 + k + 1, lanes], PEER_ROWS)
                a_half = _bf16_rows(aw_ref[0, h, k0 + k:k0 + k + 1, lanes], PEER_ROWS)
                rows = slice(h * PEER_N_KEYS + r * PEER_ROWS, h * PEER_N_KEYS + (r + 1) * PEER_ROWS)
                bq = bq_ref[rows, lanes]
                contrib = jnp.where(r2_ref[rows, lanes] < count, bq, jnp.zeros_like(bq)) * a_half
                g[h % 2] = contrib if g[h % 2] is None else g[h % 2] + contrib
            krows = slice(k * PEER_N_KEYS + r * PEER_ROWS, k * PEER_N_KEYS + (r + 1) * PEER_ROWS)
            hk = ht_ref[krows, lanes]
            gelu2 = hk * (1.0 + lax.erf(hk * math.sqrt(0.5)))
            at_ref[krows, lanes] = gelu2.astype(BF16) * (g[0] + g[1])


def _peer_step(x_ref, bq_ref, r2_ref, aw_ref, cw_ref, u_ref, v_ref, o_ref, at_ref, ht_write, ht_read, k0):
    keys_per_tile = PEER_TE // PEER_N_KEYS
    keys_per_chunk = PEER_CHUNK // PEER_N_KEYS
    tm = x_ref.shape[0]
    tok = tm // keys_per_tile
    u = u_ref[...].astype(BF16) if ht_write is not None else None
    for k in range(keys_per_tile):
        if ht_read is not None:
            _peer_gated_slab(k, k0, bq_ref, r2_ref, aw_ref, cw_ref, ht_read, at_ref)
        if ht_write is not None:
            ht_write[:, k * tok:(k + 1) * tok] = lax.dot_general(
                u, x_ref[k * tok:(k + 1) * tok, :], NT_DIMS, preferred_element_type=F32)
        if ht_read is not None and (k + 1) % keys_per_chunk == 0:
            chunk = slice((k + 1) * PEER_N_KEYS - PEER_CHUNK, (k + 1) * PEER_N_KEYS)
            o_ref[...] += lax.dot_general(at_ref[chunk, :], v_ref[chunk, :].astype(BF16), TN_DIMS,
                                          preferred_element_type=F32)


def _peer_kernel(x_ref, bq_ref, r2_ref, aw_ref, cw_ref, u_ref, v_ref, o_ref, ht0_ref, ht1_ref, at_ref):
    e = pl.program_id(1)
    last = PEER_N_TILES
    keys_per_tile = PEER_TE // PEER_N_KEYS
    assert 8 // keys_per_tile == 2
    step = functools.partial(_peer_step, x_ref, bq_ref, r2_ref, aw_ref, cw_ref, u_ref, v_ref, o_ref, at_ref)
    even = e % 2 == 0

    @pl.when(e == 0)
    def _():
        o_ref[...] = jnp.zeros_like(o_ref)
        step(ht0_ref, None, None)

    @pl.when(jnp.logical_not(even))
    def _():
        step(ht1_ref, ht0_ref, 0)

    @pl.when(even & (e > 0) & (e < last))
    def _():
        step(ht0_ref, ht1_ref, keys_per_tile)

    @pl.when(e == last)
    def _():
        step(None, ht1_ref, keys_per_tile)


def peer_experts(u, bq, r2, aw, cw, exp_u, exp_v, layer):
    tm, te = PEER_TM, PEER_TE
    n_tok = u.shape[0]
    assert PEER_N_TILES % 2 == 0 and n_tok % tm == 0
    tiles_per_row_block = 8 * PEER_N_KEYS // te
    col_spec = pl.BlockSpec((PEER_HEADS * PEER_N_KEYS, tm), lambda i, e: (0, i))
    row_spec = pl.BlockSpec((1, PEER_HEADS, 8, tm),
                            lambda i, e: (jnp.maximum(e - 1, 0) // tiles_per_row_block, 0, 0, i))
    return pl.pallas_call(
        _peer_kernel,
        out_shape=jax.ShapeDtypeStruct((n_tok, D_MODEL), F32),
        grid=(n_tok // tm, PEER_N_TILES + 1),
        in_specs=[pl.BlockSpec((tm, D_MODEL), lambda i, e: (i, 0)),
                  col_spec, col_spec,
                  row_spec, row_spec,
                  pl.BlockSpec((None, te, D_MODEL), lambda i, e: (layer, jnp.minimum(e, PEER_N_TILES - 1), 0)),
                  pl.BlockSpec((None, te, D_MODEL), lambda i, e: (layer, jnp.maximum(e - 1, 0), 0))],
        out_specs=pl.BlockSpec((tm, D_MODEL), lambda i, e: (i, 0)),
        scratch_shapes=[pltpu.VMEM((te, tm), F32), pltpu.VMEM((te, tm), F32), pltpu.VMEM((te, tm), BF16)],
        compiler_params=_params(("parallel", "arbitrary"), 60),
        name="peer_experts",
    )(u, bq, r2, aw, cw, exp_u, exp_v)


def peer_ffn(u, w_q, sub_keys, exp_u, exp_v, layer):
    q = matmul_resident(u, w_q, BF16, "peer_query", layer=layer)
    bq, r2, aw, cw = peer_route(q, sub_keys, layer)
    return peer_experts(u, bq, r2, aw, cw, exp_u, exp_v, layer)


def mla_mixer(u, cache_ckv, cache_krope, w_dq, q_norm, w_uq, w_dkv, kv_norm, w_uk, w_uv, w_o):
    pad = (-(MLA_Q_LORA + MLA_KV_LORA + MLA_ROPE)) % 128
    w_cat = jnp.concatenate([w_dq, w_dkv, jnp.zeros((D_MODEL, pad), F32)], axis=1)
    cq, ckv, kr = mla_down(u, w_cat, q_norm, kv_norm)
    w_uq3 = w_uq.reshape(MLA_Q_LORA, MLA_HEADS, MLA_NOPE + MLA_ROPE)
    w_q_cat = jnp.concatenate([w_uq3[:, :, :MLA_NOPE].reshape(MLA_Q_LORA, -1),
                               w_uq3[:, :, MLA_NOPE:].reshape(MLA_Q_LORA, -1)], axis=1)
    q = matmul_resident(cq, w_q_cat, BF16, "mla_q_up")
    w_nope = MLA_HEADS * MLA_NOPE
    ckv_src = jnp.concatenate([ckv, cache_ckv.reshape(-1, MLA_KV_LORA)], axis=0).astype(BF16)
    kv_up = matmul_resident(ckv_src, jnp.concatenate([w_uk, w_uv], axis=1), BF16, "mla_kv_up")
    cos, sin = (jnp.asarray(t, F32) for t in _rope_tables())
    w_rope = MLA_HEADS * MLA_ROPE
    qr_rot = rope_rows(q, N_PROMPT // 256, w_nope // w_rope, w_rope, N_SAMPLE, cos, sin, BF16)
    cos2, sin2 = (t.reshape(DEC_SEQ // 2, 128) for t in (cos[:, :MLA_ROPE], sin[:, :MLA_ROPE]))
    kr_pairs = kr[N_PROMPT:].reshape(N_SAMPLE // 2, 2 * MLA_ROPE)
    kr_rot = rope_rows(kr_pairs, 0, 0, 2 * MLA_ROPE, N_SAMPLE // 2, cos2, sin2, BF16).reshape(N_SAMPLE, MLA_ROPE)
    o_p = mla_attention_prompt(q, kv_up, kr)
    o_s = mla_attention_sample(q, qr_rot, kv_up, kr_rot, cache_krope.reshape(-1, MLA_ROPE))
    op = matmul_resident(o_p, w_o, BF16, "mla_out", x_sample=o_s)
    return op, ckv, kr


def gqa_mixer(u, cache_k, cache_v, w_qkv, sink, w_o):
    nq = GQA_HEADS * GQA_HEAD_DIM
    nk = GQA_KV_HEADS * GQA_HEAD_DIM
    q = matmul_resident(u, w_qkv, BF16, "gqa_q", n=nq)
    kv = matmul_resident(u, w_qkv, F32, "gqa_kv", col0=nq)
    cos, sin = (jnp.asarray(t, F32) for t in _rope_tables())
    q_rot = rope_rows(q, N_PROMPT // 256, 0, nq, N_SAMPLE, cos, sin, BF16)
    k_rot = rope_rows(kv, N_PROMPT // 256, 0, nk, N_SAMPLE, cos, sin, BF16)
    o_p = gqa_attention_prompt(q, kv, sink)
    o_s = gqa_attention_sample(q_rot, k_rot, kv, cache_k.reshape(-1, nk), cache_v.reshape(-1, nk), sink)
    op = matmul_resident(o_p, w_o, BF16, "gqa_out", x_sample=o_s)
    return op, kv


def fnet_mixer(u, w_out):
    f_p = fourier_real_2d(u, 0, BATCH, SEQ)
    f_s = fourier_real_2d(u, N_PROMPT // DEC_SEQ, DEC_BATCH, DEC_SEQ)
    return matmul_resident(f_p, w_out, BF16, "fnet_out", x_sample=f_s)


def conv_mixer(u, w_in, conv_w, conv_b, w_out):
    h3 = matmul(u, w_in, BF16, name="conv_in")
    y_p = gated_conv(h3, conv_w, conv_b, 0, BATCH, SEQ)
    y_s = gated_conv(h3, conv_w, conv_b, N_PROMPT // DEC_SEQ, DEC_BATCH, DEC_SEQ)
    return matmul_resident(y_p, w_out, BF16, "conv_out", x_sample=y_s)


def kernel(x_prompt, x_sample, cache_l0_ckv, cache_l0_krope, cache_l1_k, cache_l1_v, c, c_ctx, ada_w, ada_b, ln1_g, ln1_b, ln2_g, ln2_b, mla_w_dq, mla_q_norm, mla_w_uq, mla_w_dkv, mla_kv_norm, mla_w_uk, mla_w_uv, mla_w_o, gqa_w_qkv, gqa_sink, gqa_w_o, fnet_w_out, conv_w_in, conv_w, conv_b, conv_w_out, peer_w_q, peer_sub_keys, peer_u, peer_v):
    xp = x_prompt.reshape(N_PROMPT, D_MODEL)
    xs = x_sample.reshape(N_SAMPLE, D_MODEL)
    cond = jnp.concatenate([c_ctx[None, :], c, jnp.zeros((N_COND - 1 - DEC_BATCH, D_MODEL), F32)], axis=0)
    mods_all = ada_modulation_all(cond, ada_w, ada_b).reshape(DEPTH, N_COND, 6, 1, D_MODEL)

    u = modulate_tokens(xp, xs, mods_all[0])
    new_ckv = new_krope = new_kv = None
    for i in range(DEPTH):
        mods = mods_all[i]
        if i == 0:
            delta, new_ckv, new_krope = mla_mixer(u, cache_l0_ckv, cache_l0_krope, mla_w_dq, mla_q_norm, mla_w_uq,
                                                  mla_w_dkv, mla_kv_norm, mla_w_uk, mla_w_uv, mla_w_o)
        elif i == 1:
            delta, new_kv = gqa_mixer(u, cache_l1_k, cache_l1_v, gqa_w_qkv, gqa_sink, gqa_w_o)
        elif i == 2:
            delta = fnet_mixer(u, fnet_w_out)
        else:
            delta = conv_mixer(u, conv_w_in, conv_w, conv_b, conv_w_out)
        if i == 0:
            x, u = post_norm(xp, delta, mods, 2, ln1_g[i], ln1_b[i], mods, 3, x_sample=xs)
        else:
            x, u = post_norm(x, delta, mods, 2, ln1_g[i], ln1_b[i], mods, 3)
        delta = peer_ffn(u, peer_w_q, peer_sub_keys, peer_u, peer_v, i)
        if i + 1 < DEPTH:
            x, u = post_norm(x, delta, mods, 5, ln2_g[i], ln2_b[i], mods_all[i + 1], 0)
        else:
            y_prompt = post_norm(x, delta, mods, 5, ln2_g[i], ln2_b[i], rows=(0, N_PROMPT))
            y_sample = post_norm(x, delta, mods, 5, ln2_g[i], ln2_b[i], rows=(N_PROMPT, N_SAMPLE))

    nk = GQA_KV_HEADS * GQA_HEAD_DIM
    return (y_prompt.reshape(BATCH, SEQ, D_MODEL),
            y_sample.reshape(DEC_BATCH, DEC_SEQ, D_MODEL),
            new_ckv[:N_PROMPT].reshape(BATCH, SEQ, MLA_KV_LORA),
            new_krope[:N_PROMPT].reshape(BATCH, SEQ, MLA_ROPE),
            new_kv[:N_PROMPT, :nk].reshape(BATCH, SEQ, GQA_KV_HEADS, GQA_HEAD_DIM),
            new_kv[:N_PROMPT, nk:].reshape(BATCH, SEQ, GQA_KV_HEADS, GQA_HEAD_DIM))
```

```python
import functools
import math

import numpy as np
import jax
import jax.numpy as jnp
from jax import lax
from jax.experimental import pallas as pl
from jax.experimental.pallas import tpu as pltpu

F32 = jnp.float32
BF16 = jnp.bfloat16

D_MODEL = 2048
BATCH = 16
SEQ = 256
DEPTH = 4
DEC_BATCH = 2
DEC_SEQ = 1024
PAST_LEN = 256
GRID_W = 64
ROPE_BASE = 10000.0
LN_EPS = 1e-5
RMS_EPS = 1e-6
NEG_INF = -1e30
DEEPNORM_ALPHA = (2 * DEPTH) ** 0.25

MLA_HEADS = 16
MLA_Q_LORA = 512
MLA_KV_LORA = 512
MLA_NOPE = 128
MLA_ROPE = 64
MLA_V = 128
MLA_SCALE = (MLA_NOPE + MLA_ROPE) ** -0.5

GQA_HEADS = 32
GQA_KV_HEADS = 8
GQA_HEAD_DIM = 64
WINDOW = 128
GQA_SCALE = GQA_HEAD_DIM ** -0.5

FNET_GROUPS = 4
FNET_GROUP_DIM = D_MODEL // FNET_GROUPS

PEER_HEADS = 8
PEER_N_KEYS = 128
PEER_N_EXPERTS = PEER_N_KEYS * PEER_N_KEYS
PEER_HALF = 128
PEER_TOPK = 16

N_PROMPT = BATCH * SEQ
N_SAMPLE = DEC_BATCH * DEC_SEQ
N_TOK = N_PROMPT + N_SAMPLE
N_COND = 8

MIB = 1024 * 1024

NT_DIMS = (((1,), (1,)), ((), ()))
TN_DIMS = (((0,), (0,)), ((), ()))


def _params(sem, vmem_mib, flags=None):
    return pltpu.CompilerParams(dimension_semantics=sem, vmem_limit_bytes=vmem_mib * MIB, flags=flags)


def _group_of_tile(i, tm):
    return jnp.maximum((i * tm) // DEC_SEQ - (N_PROMPT // DEC_SEQ - 1), 0)


def _mod_spec(tm, which):
    return pl.BlockSpec((None, None, 1, D_MODEL), lambda i: (_group_of_tile(i, tm), which, 0, 0))


def _ada_kernel(c_ref, w_ref, b_ref, o_ref):
    c = c_ref[...]
    s = (c * jax.nn.sigmoid(c)).astype(BF16)
    o_ref[...] = jnp.dot(s, w_ref[...].astype(BF16), preferred_element_type=F32) + b_ref[...]


def ada_modulation_all(cond, ada_w, ada_b):
    tn = 1024
    n_out = 6 * D_MODEL
    return pl.pallas_call(
        _ada_kernel,
        out_shape=jax.ShapeDtypeStruct((DEPTH, N_COND, n_out), F32),
        grid=(DEPTH, n_out // tn),
        in_specs=[
            pl.BlockSpec((N_COND, D_MODEL), lambda l, j: (0, 0)),
            pl.BlockSpec((None, D_MODEL, tn), lambda l, j: (l, 0, j)),
            pl.BlockSpec((None, 1, tn), lambda l, j: (l, 0, j)),
        ],
        out_specs=pl.BlockSpec((None, N_COND, tn), lambda l, j: (l, 0, j)),
        compiler_params=_params(("parallel", "parallel"), 40),
        name="ada_modulation",
    )(cond, ada_w, ada_b.reshape(DEPTH, 1, n_out))


def _two_source_specs(tm, prompt_tiles):
    return [pl.BlockSpec((tm, D_MODEL), lambda i: (jnp.minimum(i, prompt_tiles - 1), 0)),
            pl.BlockSpec((tm, D_MODEL), lambda i: (jnp.maximum(i - prompt_tiles, 0), 0))]


def _for_token_group(prompt_tiles, xp_ref, xs_ref, body):
    i = pl.program_id(0)

    @pl.when(i < prompt_tiles)
    def _():
        body(xp_ref)

    @pl.when(i >= prompt_tiles)
    def _():
        body(xs_ref)


def _modulate_kernel(xp_ref, xs_ref, sh_ref, sc_ref, u_ref, *, prompt_tiles):
    def body(x_ref):
        u_ref[...] = (x_ref[...] * (1.0 + sc_ref[...]) + sh_ref[...]).astype(u_ref.dtype)

    _for_token_group(prompt_tiles, xp_ref, xs_ref, body)


def modulate_tokens(x_prompt, x_sample, mods):
    tm = 512
    prompt_tiles = N_PROMPT // tm
    return pl.pallas_call(
        functools.partial(_modulate_kernel, prompt_tiles=prompt_tiles),
        out_shape=jax.ShapeDtypeStruct((N_TOK, D_MODEL), BF16),
        grid=(N_TOK // tm,),
        in_specs=_two_source_specs(tm, prompt_tiles) + [_mod_spec(tm, 0), _mod_spec(tm, 1)],
        out_specs=pl.BlockSpec((tm, D_MODEL), lambda i: (i, 0)),
        compiler_params=_params(("parallel",), 40),
        name="modulate",
    )(x_prompt, x_sample, mods, mods)


def _layer_norm_rows(y, g, b):
    mu = jnp.mean(y, axis=-1, keepdims=True)
    yc = y - mu
    var = jnp.mean(yc * yc, axis=-1, keepdims=True)
    return yc * lax.rsqrt(var + LN_EPS) * g + b


def _post_norm_kernel(x_ref, d_ref, gate_ref, g_ref, b_ref, xo_ref):
    y = DEEPNORM_ALPHA * x_ref[...] + gate_ref[...] * d_ref[...].astype(F32)
    xo_ref[...] = _layer_norm_rows(y, g_ref[...], b_ref[...])


def _post_norm_mod_kernel(x_ref, d_ref, gate_ref, g_ref, b_ref, sh_ref, sc_ref, xo_ref, uo_ref):
    y = DEEPNORM_ALPHA * x_ref[...] + gate_ref[...] * d_ref[...].astype(F32)
    xn = _layer_norm_rows(y, g_ref[...], b_ref[...])
    xo_ref[...] = xn
    uo_ref[...] = (xn * (1.0 + sc_ref[...]) + sh_ref[...]).astype(uo_ref.dtype)


def _post_norm_mod_groups_kernel(xp_ref, xs_ref, *rest, prompt_tiles):
    _for_token_group(prompt_tiles, xp_ref, xs_ref, lambda x_ref: _post_norm_mod_kernel(x_ref, *rest))


def post_norm(x, delta, mods, gate_idx, g, b, next_mods=None, next_idx=None, rows=(0, N_TOK), x_sample=None):
    tm = 512
    row = pl.BlockSpec((tm, D_MODEL), lambda i: (i, 0))
    vec = pl.BlockSpec((1, D_MODEL), lambda i: (0, 0))
    tail = [delta, mods, g.reshape(1, D_MODEL), b.reshape(1, D_MODEL)]
    tail_specs = [row, _mod_spec(tm, gate_idx), vec, vec]
    if x_sample is not None:
        prompt_tiles = N_PROMPT // tm
        tail += [next_mods, next_mods]
        tail_specs += [_mod_spec(tm, next_idx), _mod_spec(tm, next_idx + 1)]
        return pl.pallas_call(
            functools.partial(_post_norm_mod_groups_kernel, prompt_tiles=prompt_tiles),
            out_shape=(jax.ShapeDtypeStruct((N_TOK, D_MODEL), F32), jax.ShapeDtypeStruct((N_TOK, D_MODEL), BF16)),
            grid=(N_TOK // tm,), in_specs=_two_source_specs(tm, prompt_tiles) + tail_specs, out_specs=(row, row),
            compiler_params=_params(("parallel",), 48), name="post_norm_mod",
        )(x, x_sample, *tail)
    ins = [x] + tail
    specs = [row] + tail_specs
    if next_mods is None:
        i0 = rows[0] // tm
        row_in = pl.BlockSpec((tm, D_MODEL), lambda i: (i0 + i, 0))
        gate = pl.BlockSpec((None, None, 1, D_MODEL), lambda i: (_group_of_tile(i0 + i, tm), gate_idx, 0, 0))
        return pl.pallas_call(
            _post_norm_kernel,
            out_shape=jax.ShapeDtypeStruct((rows[1], D_MODEL), F32),
            grid=(rows[1] // tm,), in_specs=[row_in, row_in, gate, vec, vec], out_specs=row,
            compiler_params=_params(("parallel",), 40), name="post_norm",
        )(*ins)
    ins += [next_mods, next_mods]
    specs += [_mod_spec(tm, next_idx), _mod_spec(tm, next_idx + 1)]
    return pl.pallas_call(
        _post_norm_mod_kernel,
        out_shape=(jax.ShapeDtypeStruct((N_TOK, D_MODEL), F32), jax.ShapeDtypeStruct((N_TOK, D_MODEL), BF16)),
        grid=(N_TOK // tm,), in_specs=specs, out_specs=(row, row),
        compiler_params=_params(("parallel",), 40), name="post_norm_mod",
    )(*ins)


def _mm_kernel(x_ref, w_ref, o_ref):
    o_ref[...] = jnp.dot(x_ref[...], w_ref[...].astype(BF16), preferred_element_type=F32).astype(o_ref.dtype)


def _mm_resident_kernel(*refs, prompt_tiles):
    if prompt_tiles is None:
        x_ref, w_ref, o_ref, wb_ref = refs
    else:
        x_ref, xs_ref, w_ref, o_ref, wb_ref = refs
    i = pl.program_id(0)

    @pl.when(i == 0)
    def _():
        wb_ref[...] = w_ref[...].astype(BF16)

    if prompt_tiles is None:
        o_ref[...] = jnp.dot(x_ref[...], wb_ref[...], preferred_element_type=F32).astype(o_ref.dtype)
    else:
        @pl.when(i < prompt_tiles)
        def _():
            o_ref[...] = jnp.dot(x_ref[...], wb_ref[...], preferred_element_type=F32).astype(o_ref.dtype)

        @pl.when(i >= prompt_tiles)
        def _():
            o_ref[...] = jnp.dot(xs_ref[...], wb_ref[...], preferred_element_type=F32).astype(o_ref.dtype)


def matmul_resident(x, w, out_dtype, name, x_sample=None, layer=None, col0=0, n=None):
    tm = 512
    k = x.shape[1]
    n = w.shape[-1] - col0 if n is None else n
    assert col0 % n == 0
    j0 = col0 // n
    rows = x.shape[0] + (0 if x_sample is None else x_sample.shape[0])
    once = pl.Buffered(1)
    if layer is None:
        w_spec = pl.BlockSpec((k, n), lambda i: (0, j0), pipeline_mode=once)
    else:
        w_spec = pl.BlockSpec((None, k, n), lambda i: (layer, 0, j0), pipeline_mode=once)
    if x_sample is None:
        prompt_tiles = None
        xs, x_specs = (x,), [pl.BlockSpec((tm, k), lambda i: (i, 0))]
    else:
        prompt_tiles = x.shape[0] // tm
        xs = (x, x_sample)
        x_specs = [pl.BlockSpec((tm, k), lambda i: (jnp.minimum(i, prompt_tiles - 1), 0)),
                   pl.BlockSpec((tm, k), lambda i: (jnp.maximum(i - prompt_tiles, 0), 0))]
    return pl.pallas_call(
        functools.partial(_mm_resident_kernel, prompt_tiles=prompt_tiles),
        out_shape=jax.ShapeDtypeStruct((rows, n), out_dtype),
        grid=(rows // tm,),
        in_specs=x_specs + [w_spec],
        out_specs=pl.BlockSpec((tm, n), lambda i: (i, 0)),
        scratch_shapes=[pltpu.VMEM((k, n), BF16)],
        compiler_params=_params(("arbitrary",), 48),
        name=name,
    )(*xs, w)


def matmul(x, w, out_dtype, tm=2048, tn=512, name="matmul", layer=None, col0=0, n=None):
    m, k = x.shape
    n = w.shape[-1] - col0 if n is None else n
    tm = min(tm, m)
    tn = min(tn, n)
    assert m % tm == 0 and n % tn == 0 and col0 % tn == 0, (m, n, tm, tn, col0)
    j0 = col0 // tn
    if layer is None:
        w_spec = pl.BlockSpec((k, tn), lambda i, j: (0, j0 + j))
    else:
        w_spec = pl.BlockSpec((None, k, tn), lambda i, j: (layer, 0, j0 + j))
    return pl.pallas_call(
        _mm_kernel,
        out_shape=jax.ShapeDtypeStruct((m, n), out_dtype),
        grid=(m // tm, n // tn),
        in_specs=[pl.BlockSpec((tm, k), lambda i, j: (i, 0)), w_spec],
        out_specs=pl.BlockSpec((tm, tn), lambda i, j: (i, j)),
        compiler_params=_params(("parallel", "parallel"), 48),
        name=name,
    )(x, w)


def _rms_rows(y, g):
    return y * lax.rsqrt(jnp.mean(y * y, axis=-1, keepdims=True) + RMS_EPS) * g


def _mla_down_kernel(x_ref, w_ref, qn_ref, kvn_ref, cq_ref, ckv_ref, kr_ref):
    y = jnp.dot(x_ref[...], w_ref[...].astype(BF16), preferred_element_type=F32)
    cq_ref[...] = _rms_rows(y[:, :MLA_Q_LORA], qn_ref[...]).astype(cq_ref.dtype)
    ckv_ref[...] = _rms_rows(y[:, MLA_Q_LORA:MLA_Q_LORA + MLA_KV_LORA], kvn_ref[...])
    kr_ref[...] = y[:, MLA_Q_LORA + MLA_KV_LORA:MLA_Q_LORA + MLA_KV_LORA + MLA_ROPE]


def mla_down(u, w_cat, q_norm, kv_norm):
    tm = 512
    n = w_cat.shape[1]
    return pl.pallas_call(
        _mla_down_kernel,
        out_shape=(jax.ShapeDtypeStruct((N_TOK, MLA_Q_LORA), BF16),
                   jax.ShapeDtypeStruct((N_TOK, MLA_KV_LORA), F32),
                   jax.ShapeDtypeStruct((N_TOK, MLA_ROPE), F32)),
        grid=(N_TOK // tm,),
        in_specs=[pl.BlockSpec((tm, D_MODEL), lambda i: (i, 0)),
                  pl.BlockSpec((D_MODEL, n), lambda i: (0, 0)),
                  pl.BlockSpec((1, MLA_Q_LORA), lambda i: (0, 0)),
                  pl.BlockSpec((1, MLA_KV_LORA), lambda i: (0, 0))],
        out_specs=(pl.BlockSpec((tm, MLA_Q_LORA), lambda i: (i, 0)),
                   pl.BlockSpec((tm, MLA_KV_LORA), lambda i: (i, 0)),
                   pl.BlockSpec((tm, MLA_ROPE), lambda i: (i, 0))),
        compiler_params=_params(("parallel",), 48),
        name="mla_down",
    )(u, w_cat, q_norm.reshape(1, -1), kv_norm.reshape(1, -1))


def _rope_tables():
    t = np.arange(DEC_SEQ)
    quarter = MLA_ROPE // 4
    inv_freq = ROPE_BASE ** (-np.arange(quarter, dtype=np.float64) / quarter)
    ang_row = (t // GRID_W)[:, None] * inv_freq[None, :]
    ang_col = (t % GRID_W)[:, None] * inv_freq[None, :]
    cos = np.concatenate([np.cos(ang_row)] * 2 + [np.cos(ang_col)] * 2, -1)
    sin = np.concatenate([-np.sin(ang_row), np.sin(ang_row), -np.sin(ang_col), np.sin(ang_col)], -1)
    return np.tile(cos, (1, 2)), np.tile(sin, (1, 2))


def _rope_kernel(x_ref, c_ref, s_ref, o_ref):
    x = x_ref[...].astype(F32)
    w = x.shape[-1]
    reps = w // c_ref.shape[-1]
    cos = jnp.tile(c_ref[...], (1, reps))
    sin = jnp.tile(s_ref[...], (1, reps))
    lane = lax.broadcasted_iota(jnp.int32, x.shape, 1)
    partner = jnp.where((lane % 32) < 16, pltpu.roll(x, w - 16, 1), pltpu.roll(x, 16, 1))
    o_ref[...] = (x * cos + partner * sin).astype(o_ref.dtype)


def rope_rows(x, row_block0, col_block, width, n_rows, cos, sin, out_dtype, tr=256):
    t_blocks = cos.shape[0] // tr
    return pl.pallas_call(
        _rope_kernel,
        out_shape=jax.ShapeDtypeStruct((n_rows, width), out_dtype),
        grid=(n_rows // tr,),
        in_specs=[pl.BlockSpec((tr, width), lambda i: (row_block0 + i, col_block)),
                  pl.BlockSpec((tr, 128), lambda i: (i % t_blocks, 0)),
                  pl.BlockSpec((tr, 128), lambda i: (i % t_blocks, 0))],
        out_specs=pl.BlockSpec((tr, width), lambda i: (i, 0)),
        compiler_params=_params(("parallel",), 32),
        name="axial_rope",
    )(x, cos, sin)


_SUM_LANES = 128


def _softmax_parts(parts, sink):
    m = parts[0].max(axis=-1, keepdims=True)
    for s in parts[1:]:
        m = jnp.maximum(m, s.max(axis=-1, keepdims=True))
    if sink is not None:
        m = jnp.maximum(m, sink)
    ps = [jnp.exp(s - m).astype(BF16) for s in parts]
    l = None
    for p in ps:
        row_sum = jnp.dot(p, jnp.ones((p.shape[1], _SUM_LANES), BF16), preferred_element_type=F32)
        l = row_sum if l is None else l + row_sum
    if sink is not None:
        l = l + jnp.exp(sink - m)
    return ps, 1.0 / l


def _mla_attn_kernel(*refs, with_ctx):
    if with_ctx:
        qn_ref, qr_ref, kn_ref, v_ref, kr_ref, kn2_ref, v2_ref, kr2_ref, o_ref = refs
    else:
        qn_ref, qr_ref, kn_ref, v_ref, kr_ref, o_ref = refs
    kr = kr_ref[...].astype(BF16)
    kr2 = kr2_ref[...].astype(BF16) if with_ctx else None
    for h in range(MLA_HEADS):
        n0, n1 = h * MLA_NOPE, (h + 1) * MLA_NOPE
        qn = qn_ref[:, n0:n1]
        qr = qr_ref[:, h * MLA_ROPE:(h + 1) * MLA_ROPE]
        s = (lax.dot_general(qn, kn_ref[:, n0:n1], NT_DIMS, preferred_element_type=F32)
             + lax.dot_general(qr, kr, NT_DIMS, preferred_element_type=F32)) * MLA_SCALE
        parts = [s]
        if with_ctx:
            s2 = (lax.dot_general(qn, kn2_ref[:, n0:n1], NT_DIMS, preferred_element_type=F32)
                  + lax.dot_general(qr, kr2, NT_DIMS, preferred_element_type=F32)) * MLA_SCALE
            parts.append(s2)
        ps, inv_l = _softmax_parts(parts, None)
        o = jnp.dot(ps[0].astype(BF16), v_ref[:, n0:n1], preferred_element_type=F32)
        if with_ctx:
            o = o + jnp.dot(ps[1].astype(BF16), v2_ref[:, n0:n1], preferred_element_type=F32)
        o_ref[:, n0:n1] = (o * inv_l[:, :MLA_V]).astype(o_ref.dtype)


def mla_attention_prompt(q, kv_up, kr):
    w = MLA_HEADS * MLA_NOPE
    return pl.pallas_call(
        functools.partial(_mla_attn_kernel, with_ctx=False),
        out_shape=jax.ShapeDtypeStruct((N_PROMPT, w), BF16),
        grid=(BATCH,),
        in_specs=[pl.BlockSpec((SEQ, w), lambda b: (b, 0)),
                  pl.BlockSpec((SEQ, MLA_HEADS * MLA_ROPE), lambda b: (b, w // (MLA_HEADS * MLA_ROPE))),
                  pl.BlockSpec((SEQ, w), lambda b: (b, 0)),
                  pl.BlockSpec((SEQ, w), lambda b: (b, 1)),
                  pl.BlockSpec((SEQ, MLA_ROPE), lambda b: (b, 0))],
        out_specs=pl.BlockSpec((SEQ, w), lambda b: (b, 0)),
        compiler_params=_params(("parallel",), 32),
        name="mla_attention_prompt",
    )(q, q, kv_up, kv_up, kr)


def mla_attention_sample(qn, qr_rot, kv_up, kr_rot, kr_cache):
    w = MLA_HEADS * MLA_NOPE
    tq = 256
    nq = DEC_SEQ // tq
    q0 = N_PROMPT // tq
    lat0 = N_PROMPT // DEC_SEQ
    ctx0 = N_TOK // PAST_LEN
    return pl.pallas_call(
        functools.partial(_mla_attn_kernel, with_ctx=True),
        out_shape=jax.ShapeDtypeStruct((N_SAMPLE, w), BF16),
        grid=(DEC_BATCH, nq),
        in_specs=[pl.BlockSpec((tq, w), lambda b, i: (q0 + b * nq + i, 0)),
                  pl.BlockSpec((tq, MLA_HEADS * MLA_ROPE), lambda b, i: (b * nq + i, 0)),
                  pl.BlockSpec((DEC_SEQ, w), lambda b, i: (lat0 + b, 0)),
                  pl.BlockSpec((DEC_SEQ, w), lambda b, i: (lat0 + b, 1)),
                  pl.BlockSpec((DEC_SEQ, MLA_ROPE), lambda b, i: (b, 0)),
                  pl.BlockSpec((PAST_LEN, w), lambda b, i: (ctx0 + b, 0)),
                  pl.BlockSpec((PAST_LEN, w), lambda b, i: (ctx0 + b, 1)),
                  pl.BlockSpec((PAST_LEN, MLA_ROPE), lambda b, i: (b, 0))],
        out_specs=pl.BlockSpec((tq, w), lambda b, i: (b * nq + i, 0)),
        compiler_params=_params(("parallel", "parallel"), 56),
        name="mla_attention_sample",
    )(qn, qr_rot, kv_up, kv_up, kr_rot, kv_up, kv_up, kr_cache)


def _gqa_attn_kernel(*refs, with_ctx, tq):
    if with_ctx:
        sink_ref, q_ref, k_ref, v_ref, k2_ref, v2_ref, o_ref = refs
    else:
        sink_ref, q_ref, k_ref, v_ref, o_ref = refs
    group = GQA_HEADS // GQA_KV_HEADS
    d = GQA_HEAD_DIM
    if with_ctx:
        t = pl.program_id(1) * tq + lax.broadcasted_iota(jnp.int32, (tq, DEC_SEQ), 0)
        s_pos = lax.broadcasted_iota(jnp.int32, (tq, DEC_SEQ), 1)
        in_window = jnp.abs(t - s_pos) <= WINDOW
    for hk in range(GQA_KV_HEADS):
        k = k_ref[:, hk * d:(hk + 1) * d].astype(BF16)
        v = v_ref[:, hk * d:(hk + 1) * d].astype(BF16)
        if with_ctx:
            k2 = k2_ref[:, hk * d:(hk + 1) * d].astype(BF16)
            v2 = v2_ref[:, hk * d:(hk + 1) * d].astype(BF16)
        for g in range(group):
            h = hk * group + g
            q = q_ref[:, h * d:(h + 1) * d]
            s = lax.dot_general(q, k, NT_DIMS, preferred_element_type=F32) * GQA_SCALE
            parts = [s]
            if with_ctx:
                parts = [jnp.where(in_window, s, NEG_INF),
                         lax.dot_general(q, k2, NT_DIMS, preferred_element_type=F32) * GQA_SCALE]
            ps, inv_l = _softmax_parts(parts, sink_ref[h])
            o = jnp.dot(ps[0].astype(BF16), v, preferred_element_type=F32)
            if with_ctx:
                o = o + jnp.dot(ps[1].astype(BF16), v2, preferred_element_type=F32)
            o_ref[:, h * d:(h + 1) * d] = (o * inv_l[:, :d]).astype(o_ref.dtype)


def gqa_attention_prompt(q, kv, sink):
    wq = GQA_HEADS * GQA_HEAD_DIM
    wk = GQA_KV_HEADS * GQA_HEAD_DIM
    return pl.pallas_call(
        functools.partial(_gqa_attn_kernel, with_ctx=False, tq=SEQ),
        out_shape=jax.ShapeDtypeStruct((N_PROMPT, wq), BF16),
        grid=(BATCH,),
        in_specs=[pl.BlockSpec(memory_space=pltpu.SMEM),
                  pl.BlockSpec((SEQ, wq), lambda b: (b, 0)),
                  pl.BlockSpec((SEQ, wk), lambda b: (b, 0)),
                  pl.BlockSpec((SEQ, wk), lambda b: (b, 1))],
        out_specs=pl.BlockSpec((SEQ, wq), lambda b: (b, 0)),
        compiler_params=_params(("parallel",), 32),
        name="gqa_attention_prompt",
    )(sink, q, kv, kv)


def gqa_attention_sample(q_rot, k_rot, kv, k_cache, v_cache, sink):
    wq = GQA_HEADS * GQA_HEAD_DIM
    wk = GQA_KV_HEADS * GQA_HEAD_DIM
    tq = 256
    nq = DEC_SEQ // tq
    lat0 = N_PROMPT // DEC_SEQ
    return pl.pallas_call(
        functools.partial(_gqa_attn_kernel, with_ctx=True, tq=tq),
        out_shape=jax.ShapeDtypeStruct((N_SAMPLE, wq), BF16),
        grid=(DEC_BATCH, nq),
        in_specs=[pl.BlockSpec(memory_space=pltpu.SMEM),
                  pl.BlockSpec((tq, wq), lambda b, i: (b * nq + i, 0)),
                  pl.BlockSpec((DEC_SEQ, wk), lambda b, i: (b, 0)),
                  pl.BlockSpec((DEC_SEQ, wk), lambda b, i: (lat0 + b, 1)),
                  pl.BlockSpec((PAST_LEN, wk), lambda b, i: (b, 0)),
                  pl.BlockSpec((PAST_LEN, wk), lambda b, i: (b, 0))],
        out_specs=pl.BlockSpec((tq, wq), lambda b, i: (b * nq + i, 0)),
        compiler_params=_params(("parallel", "parallel"), 56),
        name="gqa_attention_sample",
    )(sink, q_rot, k_rot, kv, k_cache, v_cache)


def _dft_tables(n):
    jk = (np.arange(n)[:, None] * np.arange(n)[None, :]) % n
    ang = 2.0 * np.pi * jk / n
    return np.cos(ang) / math.sqrt(n), np.sin(ang) / math.sqrt(n)


def _fnet_kernel(x_ref, ct_ref, st_ref, cc_ref, sc_ref, o_ref):
    x = x_ref[...]
    y1 = jnp.dot(x, cc_ref[...], preferred_element_type=F32).astype(BF16)
    y2 = jnp.dot(x, sc_ref[...], preferred_element_type=F32).astype(BF16)
    o = (jnp.dot(ct_ref[...], y1, preferred_element_type=F32)
         - jnp.dot(st_ref[...], y2, preferred_element_type=F32))
    o_ref[...] = o.astype(o_ref.dtype)


def fourier_real_2d(u, seq_block0, n_seq, t):
    ct, st = (jnp.asarray(a, BF16) for a in _dft_tables(t))
    cc, sc = (jnp.asarray(a, BF16) for a in _dft_tables(FNET_GROUP_DIM))
    gd = FNET_GROUP_DIM
    return pl.pallas_call(
        _fnet_kernel,
        out_shape=jax.ShapeDtypeStruct((n_seq * t, D_MODEL), BF16),
        grid=(n_seq, FNET_GROUPS),
        in_specs=[pl.BlockSpec((t, gd), lambda b, g: (seq_block0 + b, g)),
                  pl.BlockSpec((t, t), lambda b, g: (0, 0)),
                  pl.BlockSpec((t, t), lambda b, g: (0, 0)),
                  pl.BlockSpec((gd, gd), lambda b, g: (0, 0)),
                  pl.BlockSpec((gd, gd), lambda b, g: (0, 0))],
        out_specs=pl.BlockSpec((t, gd), lambda b, g: (b, g)),
        compiler_params=_params(("parallel", "parallel"), 32),
        name="fourier_mix",
    )(u, ct, st, cc, sc)


def _conv_kernel(b_ref, c_ref, h_ref, w_ref, cb_ref, o_ref):
    z = c_ref[...].astype(F32) * h_ref[...].astype(F32)
    t = z.shape[0]
    row = lax.broadcasted_iota(jnp.int32, z.shape, 0)
    z_prev = jnp.where(row == 0, 0.0, pltpu.roll(z, 1, 0))
    z_next = jnp.where(row == t - 1, 0.0, pltpu.roll(z, t - 1, 0))
    conv = z_prev * w_ref[0:1, :] + z * w_ref[1:2, :] + z_next * w_ref[2:3, :] + cb_ref[...]
    o_ref[...] = (b_ref[...].astype(F32) * conv).astype(o_ref.dtype)


def gated_conv(h3, conv_w, conv_b, seq_block0, n_seq, t):
    tn = 512
    nj = D_MODEL // tn
    return pl.pallas_call(
        _conv_kernel,
        out_shape=jax.ShapeDtypeStruct((n_seq * t, D_MODEL), BF16),
        grid=(n_seq, nj),
        in_specs=[pl.BlockSpec((t, tn), lambda s, j: (seq_block0 + s, j)),
                  pl.BlockSpec((t, tn), lambda s, j: (seq_block0 + s, nj + j)),
                  pl.BlockSpec((t, tn), lambda s, j: (seq_block0 + s, 2 * nj + j)),
                  pl.BlockSpec((3, tn), lambda s, j: (0, j)),
                  pl.BlockSpec((1, tn), lambda s, j: (0, j))],
        out_specs=pl.BlockSpec((t, tn), lambda s, j: (s, j)),
        compiler_params=_params(("parallel", "parallel"), 32),
        name="gated_conv",
    )(h3, h3, h3, conv_w, conv_b.reshape(1, D_MODEL))


_N_RANKS = PEER_TOPK + 1
_CAND_PAIRS = [(a, b) for a in range(_N_RANKS) for b in range(_N_RANKS) if (a + 1) * (b + 1) <= _N_RANKS]


_NO_RANK = 127.0


def _top_values_exact(s, k, want_rank):
    n = s.shape[0]
    idx = lax.broadcasted_iota(jnp.int32, s.shape, 0)
    rank = jnp.full(s.shape, _NO_RANK, F32) if want_rank else None
    out = []
    for it in range(k):
        m = jnp.max(s, axis=0, keepdims=True)
        out.append(m)
        first = jnp.min(jnp.where(s == m, idx, n), axis=0, keepdims=True)
        hit = idx == first
        s = jnp.where(hit, -jnp.inf, s)
        if want_rank:
            rank = jnp.where(hit, float(it), rank)
    return out, rank


def _all_equal(x, value):
    return jnp.min(jnp.where(x == value, 1.0, 0.0)) > 0.5


def _top_values_distinct(s, k, n_pad):
    out = []
    for it in range(k):
        m = jnp.max(s, axis=0, keepdims=True)
        out.append(m)
        s = jnp.where(s == m, -jnp.inf, s)
    removed = jnp.sum(jnp.where(s == -jnp.inf, 1.0, 0.0), axis=0, keepdims=True)
    return out, _all_equal(removed, float(k + n_pad))


def _batcher_network(n):
    def merge(lo, hi, r):
        step = r * 2
        if step < hi - lo:
            yield from merge(lo, hi, step)
            yield from merge(lo + r, hi, step)
            yield from [(i, i + r) for i in range(lo + r, hi - r, step)]
        else:
            yield (lo, lo + r)

    def sort(lo, hi):
        if hi - lo >= 1:
            mid = lo + (hi - lo) // 2
            yield from sort(lo, mid)
            yield from sort(mid + 1, hi)
            yield from merge(lo, hi, 1)

    return list(sort(0, n - 1))


_SUBLANES = 8
_KEY_TILES = PEER_N_KEYS // _SUBLANES
_SORT_KEY_TILES = _batcher_network(_KEY_TILES)


def _top_values_sorted(s, k):
    tiles = [s[j * _SUBLANES:(j + 1) * _SUBLANES, :] for j in range(_KEY_TILES)]
    for i, j in _SORT_KEY_TILES:
        tiles[i], tiles[j] = jnp.maximum(tiles[i], tiles[j]), jnp.minimum(tiles[i], tiles[j])
    out = []
    pops = jnp.zeros_like(tiles[0])
    for it in range(k):
        m = jnp.max(tiles[0], axis=0, keepdims=True)
        out.append(m)
        hit = tiles[0] == m
        pops = pops + jnp.where(hit, 1.0, 0.0)
        depth = k - 1 - it
        for j in range(min(depth, _KEY_TILES - 1)):
            tiles[j] = jnp.where(hit, tiles[j + 1], tiles[j])
        if depth >= _KEY_TILES - 1:
            tiles[-1] = jnp.where(hit, -jnp.inf, tiles[-1])
    return out, _all_equal(jnp.sum(pops, axis=0, keepdims=True), float(k))


def _ranks_from_values(s, values):
    rank = jnp.full(s.shape, _NO_RANK, F32)
    for r in reversed(range(len(values))):
        rank = jnp.where(s >= values[r], float(r), rank)
    return rank


def _dup_bf16_words(x):
    bits = pltpu.bitcast(x.astype(BF16).astype(F32), jnp.int32)
    return bits | lax.shift_right_logical(bits, jnp.int32(16))


def _route_head(h, s1, s2, fast, bq_ref, r2_ref, aw_ref, cw_ref):
    tm = s1.shape[1]
    if fast:
        v1, ok1 = _top_values_sorted(s1, _N_RANKS)
        v2, ok2 = _top_values_sorted(s2, _N_RANKS)
        rank2 = _ranks_from_values(s2, v2[:PEER_TOPK])
    else:
        v1, _ = _top_values_exact(s1, _N_RANKS, False)
        v2, rank2 = _top_values_exact(s2, _N_RANKS, True)
    n_pad = (-len(_CAND_PAIRS)) % 8
    n_cand = len(_CAND_PAIRS) + n_pad
    cand_row = lax.broadcasted_iota(jnp.int32, (n_cand, tm), 0)
    cand = jnp.full((n_cand, tm), -jnp.inf, F32)
    for r, (a, b) in enumerate(_CAND_PAIRS):
        cand = jnp.where(cand_row == r, v1[a] + v2[b], cand)
    if fast:
        top, ok3 = _top_values_distinct(cand, _N_RANKS, n_pad)
    else:
        top, _ = _top_values_exact(cand, _N_RANKS, False)
    tau = 0.5 * (top[PEER_TOPK - 1] + top[PEER_TOPK])
    z = jnp.exp(top[0] - top[0])
    for kk in range(1, PEER_TOPK):
        z = z + jnp.exp(top[kk] - top[0])
    inv_z = 1.0 / z
    count = jnp.zeros_like(s1)
    for b in range(PEER_TOPK):
        count = jnp.where(s1 >= tau - v2[b], float(b + 1), count)
    rows = slice(h * PEER_N_KEYS, (h + 1) * PEER_N_KEYS)
    bq_ref[rows, :] = (jnp.exp(s2 - v2[0]) * inv_z).astype(BF16)
    r2_ref[rows, :] = rank2.astype(BF16)
    a_half = 0.5 * jnp.exp(s1 - v1[0])
    aw_ref[:, h, :, :] = _dup_bf16_words(a_half).reshape(PEER_N_KEYS // 8, 8, tm)
    cw_ref[:, h, :, :] = _dup_bf16_words(count).reshape(PEER_N_KEYS // 8, 8, tm)
    return jnp.logical_and(jnp.logical_and(ok1, ok2), ok3) if fast else None


def _route_kernel(q_ref, sk_ref, bq_ref, r2_ref, aw_ref, cw_ref):
    outs = (bq_ref, r2_ref, aw_ref, cw_ref)

    def scores(hc):
        return lax.dot_general(sk_ref[hc].astype(BF16), q_ref[:, hc * PEER_HALF:(hc + 1) * PEER_HALF],
                               NT_DIMS, preferred_element_type=F32)

    no_ties = [_route_head(h, scores(2 * h), scores(2 * h + 1), True, *outs) for h in range(PEER_HEADS)]
    for h in range(PEER_HEADS):
        @pl.when(jnp.logical_not(no_ties[h]))
        def _():
            _route_head(h, scores(2 * h), scores(2 * h + 1), False, *outs)


def peer_route(q, sub_keys, layer):
    tm = 256
    n_tok = q.shape[0]
    sk = sub_keys.reshape(-1, PEER_HEADS * 2, PEER_N_KEYS, PEER_HALF)
    tiles = PEER_N_KEYS // 8
    col_out = jax.ShapeDtypeStruct((PEER_HEADS * PEER_N_KEYS, n_tok), BF16)
    col_spec = pl.BlockSpec((PEER_HEADS * PEER_N_KEYS, tm), lambda i: (0, i))
    row_out = jax.ShapeDtypeStruct((tiles, PEER_HEADS, 8, n_tok), jnp.int32)
    row_spec = pl.BlockSpec((tiles, PEER_HEADS, 8, tm), lambda i: (0, 0, 0, i))
    return pl.pallas_call(
        _route_kernel,
        out_shape=(col_out, col_out, row_out, row_out),
        grid=(n_tok // tm,),
        in_specs=[pl.BlockSpec((tm, PEER_HEADS * 2 * PEER_HALF), lambda i: (i, 0)),
                  pl.BlockSpec((None, PEER_HEADS * 2, PEER_N_KEYS, PEER_HALF), lambda i: (layer, 0, 0, 0))],
        out_specs=(col_spec, col_spec, row_spec, row_spec),
        compiler_params=_params(("parallel",), 32),
        name="peer_route",
    )(q, sk)


PEER_TE = 512
PEER_TM = 1024
PEER_CHUNK = 512
PEER_N_TILES = PEER_N_EXPERTS // PEER_TE


def _bf16_rows(word_row, n_rows):
    return pltpu.bitcast(jnp.broadcast_to(word_row, (n_rows // 2, word_row.shape[1])), BF16)


PEER_LANES = 256
PEER_ROWS = 128


def _peer_gated_slab(k, k0, bq_ref, r2_ref, aw_ref, cw_ref, ht_ref, at_ref):
    for j in range(ht_ref.shape[1] // PEER_LANES):
        lanes = slice(j * PEER_LANES, (j + 1) * PEER_LANES)
        for r in range(PEER_N_KEYS // PEER_ROWS):
            g = [None, None]
            for h in range(PEER_HEADS):
                count = _bf16_rows(cw_ref[0, h, k0 + k:k0 + k + 1, lanes], PEER_ROWS)
                a_half = _bf16_rows(aw_ref[0, h, k0 + k:k0 + k + 1, lanes], PEER_ROWS)
                rows = slice(h * PEER_N_KEYS + r * PEER_ROWS, h * PEER_N_KEYS + (r + 1) * PEER_ROWS)
                bq = bq_ref[rows, lanes]
                contrib = jnp.where(r2_ref[rows, lanes] < count, bq, jnp.zeros_like(bq)) * a_half
                g[h % 2] = contrib if g[h % 2] is None else g[h % 2] + contrib
            krows = slice(k * PEER_N_KEYS + r * PEER_ROWS, k * PEER_N_KEYS + (r + 1) * PEER_ROWS)
            hk = ht_ref[krows, lanes]
            gelu2 = hk * (1.0 + lax.erf(hk * math.sqrt(0.5)))
            at_ref[krows, lanes] = gelu2.astype(BF16) * (g[0] + g[1])


def _peer_step(x_ref, bq_ref, r2_ref, aw_ref, cw_ref, u_ref, v_ref, o_ref, at_ref, ht_write, ht_read, k0):
    keys_per_tile = PEER_TE // PEER_N_KEYS
    keys_per_chunk = PEER_CHUNK // PEER_N_KEYS
    tm = x_ref.shape[0]
    tok = tm // keys_per_tile
    u = u_ref[...].astype(BF16) if ht_write is not None else None
    for k in range(keys_per_tile):
        if ht_read is not None:
            _peer_gated_slab(k, k0, bq_ref, r2_ref, aw_ref, cw_ref, ht_read, at_ref)
        if ht_write is not None:
            ht_write[:, k * tok:(k + 1) * tok] = lax.dot_general(
                u, x_ref[k * tok:(k + 1) * tok, :], NT_DIMS, preferred_element_type=F32)
        if ht_read is not None and (k + 1) % keys_per_chunk == 0:
            chunk = slice((k + 1) * PEER_N_KEYS - PEER_CHUNK, (k + 1) * PEER_N_KEYS)
            o_ref[...] += lax.dot_general(at_ref[chunk, :], v_ref[chunk, :].astype(BF16), TN_DIMS,
                                          preferred_element_type=F32)


def _peer_kernel(x_ref, bq_ref, r2_ref, aw_ref, cw_ref, u_ref, v_ref, o_ref, ht0_ref, ht1_ref, at_ref):
    e = pl.program_id(1)
    last = PEER_N_TILES
    keys_per_tile = PEER_TE // PEER_N_KEYS
    assert 8 // keys_per_tile == 2
    step = functools.partial(_peer_step, x_ref, bq_ref, r2_ref, aw_ref, cw_ref, u_ref, v_ref, o_ref, at_ref)
    even = e % 2 == 0

    @pl.when(e == 0)
    def _():
        o_ref[...] = jnp.zeros_like(o_ref)
        step(ht0_ref, None, None)

    @pl.when(jnp.logical_not(even))
    def _():
        step(ht1_ref, ht0_ref, 0)

    @pl.when(even & (e > 0) & (e < last))
    def _():
        step(ht0_ref, ht1_ref, keys_per_tile)

    @pl.when(e == last)
    def _():
        step(None, ht1_ref, keys_per_tile)


def peer_experts(u, bq, r2, aw, cw, exp_u, exp_v, layer):
    tm, te = PEER_TM, PEER_TE
    n_tok = u.shape[0]
    assert PEER_N_TILES % 2 == 0 and n_tok % tm == 0
    tiles_per_row_block = 8 * PEER_N_KEYS // te
    col_spec = pl.BlockSpec((PEER_HEADS * PEER_N_KEYS, tm), lambda i, e: (0, i))
    row_spec = pl.BlockSpec((1, PEER_HEADS, 8, tm),
                            lambda i, e: (jnp.maximum(e - 1, 0) // tiles_per_row_block, 0, 0, i))
    return pl.pallas_call(
        _peer_kernel,
        out_shape=jax.ShapeDtypeStruct((n_tok, D_MODEL), F32),
        grid=(n_tok // tm, PEER_N_TILES + 1),
        in_specs=[pl.BlockSpec((tm, D_MODEL), lambda i, e: (i, 0)),
                  col_spec, col_spec,
                  row_spec, row_spec,
                  pl.BlockSpec((None, te, D_MODEL), lambda i, e: (layer, jnp.minimum(e, PEER_N_TILES - 1), 0)),
                  pl.BlockSpec((None, te, D_MODEL), lambda i, e: (layer, jnp.maximum(e - 1, 0), 0))],
        out_specs=pl.BlockSpec((tm, D_MODEL), lambda i, e: (i, 0)),
        scratch_shapes=[pltpu.VMEM((te, tm), F32), pltpu.VMEM((te, tm), F32), pltpu.VMEM((te, tm), BF16)],
        compiler_params=_params(("parallel", "arbitrary"), 60),
        name="peer_experts",
    )(u, bq, r2, aw, cw, exp_u, exp_v)


def peer_ffn(u, w_q, sub_keys, exp_u, exp_v, layer):
    q = matmul_resident(u, w_q, BF16, "peer_query", layer=layer)
    bq, r2, aw, cw = peer_route(q, sub_keys, layer)
    return peer_experts(u, bq, r2, aw, cw, exp_u, exp_v, layer)


def mla_mixer(u, cache_ckv, cache_krope, w_dq, q_norm, w_uq, w_dkv, kv_norm, w_uk, w_uv, w_o):
    pad = (-(MLA_Q_LORA + MLA_KV_LORA + MLA_ROPE)) % 128
    w_cat = jnp.concatenate([w_dq, w_dkv, jnp.zeros((D_MODEL, pad), F32)], axis=1)
    cq, ckv, kr = mla_down(u, w_cat, q_norm, kv_norm)
    w_uq3 = w_uq.reshape(MLA_Q_LORA, MLA_HEADS, MLA_NOPE + MLA_ROPE)
    w_q_cat = jnp.concatenate([w_uq3[:, :, :MLA_NOPE].reshape(MLA_Q_LORA, -1),
                               w_uq3[:, :, MLA_NOPE:].reshape(MLA_Q_LORA, -1)], axis=1)
    q = matmul_resident(cq, w_q_cat, BF16, "mla_q_up")
    w_nope = MLA_HEADS * MLA_NOPE
    ckv_src = jnp.concatenate([ckv, cache_ckv.reshape(-1, MLA_KV_LORA)], axis=0).astype(BF16)
    kv_up = matmul_resident(ckv_src, jnp.concatenate([w_uk, w_uv], axis=1), BF16, "mla_kv_up")
    cos, sin = (jnp.asarray(t, F32) for t in _rope_tables())
    w_rope = MLA_HEADS * MLA_ROPE
    qr_rot = rope_rows(q, N_PROMPT // 256, w_nope // w_rope, w_rope, N_SAMPLE, cos, sin, BF16)
    cos2, sin2 = (t.reshape(DEC_SEQ // 2, 128) for t in (cos[:, :MLA_ROPE], sin[:, :MLA_ROPE]))
    kr_pairs = kr[N_PROMPT:].reshape(N_SAMPLE // 2, 2 * MLA_ROPE)
    kr_rot = rope_rows(kr_pairs, 0, 0, 2 * MLA_ROPE, N_SAMPLE // 2, cos2, sin2, BF16).reshape(N_SAMPLE, MLA_ROPE)
    o_p = mla_attention_prompt(q, kv_up, kr)
    o_s = mla_attention_sample(q, qr_rot, kv_up, kr_rot, cache_krope.reshape(-1, MLA_ROPE))
    op = matmul_resident(o_p, w_o, BF16, "mla_out", x_sample=o_s)
    return op, ckv, kr


def gqa_mixer(u, cache_k, cache_v, w_qkv, sink, w_o):
    nq = GQA_HEADS * GQA_HEAD_DIM
    nk = GQA_KV_HEADS * GQA_HEAD_DIM
    q = matmul_resident(u, w_qkv, BF16, "gqa_q", n=nq)
    kv = matmul_resident(u, w_qkv, F32, "gqa_kv", col0=nq)
    cos, sin = (jnp.asarray(t, F32) for t in _rope_tables())
    q_rot = rope_rows(q, N_PROMPT // 256, 0, nq, N_SAMPLE, cos, sin, BF16)
    k_rot = rope_rows(kv, N_PROMPT // 256, 0, nk, N_SAMPLE, cos, sin, BF16)
    o_p = gqa_attention_prompt(q, kv, sink)
    o_s = gqa_attention_sample(q_rot, k_rot, kv, cache_k.reshape(-1, nk), cache_v.reshape(-1, nk), sink)
    op = matmul_resident(o_p, w_o, BF16, "gqa_out", x_sample=o_s)
    return op, kv


def fnet_mixer(u, w_out):
    f_p = fourier_real_2d(u, 0, BATCH, SEQ)
    f_s = fourier_real_2d(u, N_PROMPT // DEC_SEQ, DEC_BATCH, DEC_SEQ)
    return matmul_resident(f_p, w_out, BF16, "fnet_out", x_sample=f_s)


def conv_mixer(u, w_in, conv_w, conv_b, w_out):
    h3 = matmul(u, w_in, BF16, name="conv_in")
    y_p = gated_conv(h3, conv_w, conv_b, 0, BATCH, SEQ)
    y_s = gated_conv(h3, conv_w, conv_b, N_PROMPT // DEC_SEQ, DEC_BATCH, DEC_SEQ)
    return matmul_resident(y_p, w_out, BF16, "conv_out", x_sample=y_s)


def kernel(x_prompt, x_sample, cache_l0_ckv, cache_l0_krope, cache_l1_k, cache_l1_v, c, c_ctx, ada_w, ada_b, ln1_g, ln1_b, ln2_g, ln2_b, mla_w_dq, mla_q_norm, mla_w_uq, mla_w_dkv, mla_kv_norm, mla_w_uk, mla_w_uv, mla_w_o, gqa_w_qkv, gqa_sink, gqa_w_o, fnet_w_out, conv_w_in, conv_w, conv_b, conv_w_out, peer_w_q, peer_sub_keys, peer_u, peer_v):
    xp = x_prompt.reshape(N_PROMPT, D_MODEL)
    xs = x_sample.reshape(N_SAMPLE, D_MODEL)
    cond = jnp.concatenate([c_ctx[None, :], c, jnp.zeros((N_COND - 1 - DEC_BATCH, D_MODEL), F32)], axis=0)
    mods_all = ada_modulation_all(cond, ada_w, ada_b).reshape(DEPTH, N_COND, 6, 1, D_MODEL)

    u = modulate_tokens(xp, xs, mods_all[0])
    new_ckv = new_krope = new_kv = None
    for i in range(DEPTH):
        mods = mods_all[i]
        if i == 0:
            delta, new_ckv, new_krope = mla_mixer(u, cache_l0_ckv, cache_l0_krope, mla_w_dq, mla_q_norm, mla_w_uq,
                                                  mla_w_dkv, mla_kv_norm, mla_w_uk, mla_w_uv, mla_w_o)
        elif i == 1:
            delta, new_kv = gqa_mixer(u, cache_l1_k, cache_l1_v, gqa_w_qkv, gqa_sink, gqa_w_o)
        elif i == 2:
            delta = fnet_mixer(u, fnet_w_out)
        else:
            delta = conv_mixer(u, conv_w_in, conv_w, conv_b, conv_w_out)
        if i == 0:
            x, u = post_norm(xp, delta, mods, 2, ln1_g[i], ln1_b[i], mods, 3, x_sample=xs)
        else:
            x, u = post_norm(x, delta, mods, 2, ln1_g[i], ln1_b[i], mods, 3)
        delta = peer_ffn(u, peer_w_q, peer_sub_keys, peer_u, peer_v, i)
        if i + 1 < DEPTH:
            x, u = post_norm(x, delta, mods, 5, ln2_g[i], ln2_b[i], mods_all[i + 1], 0)
        else:
            y_prompt = post_norm(x, delta, mods, 5, ln2_g[i], ln2_b[i], rows=(0, N_PROMPT))
            y_sample = post_norm(x, delta, mods, 5, ln2_g[i], ln2_b[i], rows=(N_PROMPT, N_SAMPLE))

    nk = GQA_KV_HEADS * GQA_HEAD_DIM
    return (y_prompt.reshape(BATCH, SEQ, D_MODEL),
            y_sample.reshape(DEC_BATCH, DEC_SEQ, D_MODEL),
            new_ckv[:N_PROMPT].reshape(BATCH, SEQ, MLA_KV_LORA),
            new_krope[:N_PROMPT].reshape(BATCH, SEQ, MLA_ROPE),
            new_kv[:N_PROMPT, :nk].reshape(BATCH, SEQ, GQA_KV_HEADS, GQA_HEAD_DIM),
            new_kv[:N_PROMPT, nk:].reshape(BATCH, SEQ, GQA_KV_HEADS, GQA_HEAD_DIM))
```

```python
import functools
import math

import numpy as np
import jax
import jax.numpy as jnp
from jax import lax
from jax.experimental import pallas as pl
from jax.experimental.pallas import tpu as pltpu

F32 = jnp.float32
BF16 = jnp.bfloat16

D_MODEL = 2048
BATCH = 16
SEQ = 256
DEPTH = 4
DEC_BATCH = 2
DEC_SEQ = 1024
PAST_LEN = 256
GRID_W = 64
ROPE_BASE = 10000.0
LN_EPS = 1e-5
RMS_EPS = 1e-6
NEG_INF = -1e30
DEEPNORM_ALPHA = (2 * DEPTH) ** 0.25

MLA_HEADS = 16
MLA_Q_LORA = 512
MLA_KV_LORA = 512
MLA_NOPE = 128
MLA_ROPE = 64
MLA_V = 128
MLA_SCALE = (MLA_NOPE + MLA_ROPE) ** -0.5

GQA_HEADS = 32
GQA_KV_HEADS = 8
GQA_HEAD_DIM = 64
WINDOW = 128
GQA_SCALE = GQA_HEAD_DIM ** -0.5

FNET_GROUPS = 4
FNET_GROUP_DIM = D_MODEL // FNET_GROUPS

PEER_HEADS = 8
PEER_N_KEYS = 128
PEER_N_EXPERTS = PEER_N_KEYS * PEER_N_KEYS
PEER_HALF = 128
PEER_TOPK = 16

N_PROMPT = BATCH * SEQ
N_SAMPLE = DEC_BATCH * DEC_SEQ
N_TOK = N_PROMPT + N_SAMPLE
N_COND = 8

MIB = 1024 * 1024

NT_DIMS = (((1,), (1,)), ((), ()))
TN_DIMS = (((0,), (0,)), ((), ()))


def _params(sem, vmem_mib, flags=None):
    return pltpu.CompilerParams(dimension_semantics=sem, vmem_limit_bytes=vmem_mib * MIB, flags=flags)


def _group_of_tile(i, tm):
    return jnp.maximum((i * tm) // DEC_SEQ - (N_PROMPT // DEC_SEQ - 1), 0)


def _mod_spec(tm, which):
    return pl.BlockSpec((None, None, 1, D_MODEL), lambda i: (_group_of_tile(i, tm), which, 0, 0))


def _ada_kernel(c_ref, w_ref, b_ref, o_ref):
    c = c_ref[...]
    s = (c * jax.nn.sigmoid(c)).astype(BF16)
    o_ref[...] = jnp.dot(s, w_ref[...].astype(BF16), preferred_element_type=F32) + b_ref[...]


def ada_modulation_all(cond, ada_w, ada_b):
    tn = 1024
    n_out = 6 * D_MODEL
    return pl.pallas_call(
        _ada_kernel,
        out_shape=jax.ShapeDtypeStruct((DEPTH, N_COND, n_out), F32),
        grid=(DEPTH, n_out // tn),
        in_specs=[
            pl.BlockSpec((N_COND, D_MODEL), lambda l, j: (0, 0)),
            pl.BlockSpec((None, D_MODEL, tn), lambda l, j: (l, 0, j)),
            pl.BlockSpec((None, 1, tn), lambda l, j: (l, 0, j)),
        ],
        out_specs=pl.BlockSpec((None, N_COND, tn), lambda l, j: (l, 0, j)),
        compiler_params=_params(("parallel", "parallel"), 40),
        name="ada_modulation",
    )(cond, ada_w, ada_b.reshape(DEPTH, 1, n_out))


def _two_source_specs(tm, prompt_tiles):
    return [pl.BlockSpec((tm, D_MODEL), lambda i: (jnp.minimum(i, prompt_tiles - 1), 0)),
            pl.BlockSpec((tm, D_MODEL), lambda i: (jnp.maximum(i - prompt_tiles, 0), 0))]


def _for_token_group(prompt_tiles, xp_ref, xs_ref, body):
    i = pl.program_id(0)

    @pl.when(i < prompt_tiles)
    def _():
        body(xp_ref)

    @pl.when(i >= prompt_tiles)
    def _():
        body(xs_ref)


def _modulate_kernel(xp_ref, xs_ref, sh_ref, sc_ref, u_ref, *, prompt_tiles):
    def body(x_ref):
        u_ref[...] = (x_ref[...] * (1.0 + sc_ref[...]) + sh_ref[...]).astype(u_ref.dtype)

    _for_token_group(prompt_tiles, xp_ref, xs_ref, body)


def modulate_tokens(x_prompt, x_sample, mods):
    tm = 512
    prompt_tiles = N_PROMPT // tm
    return pl.pallas_call(
        functools.partial(_modulate_kernel, prompt_tiles=prompt_tiles),
        out_shape=jax.ShapeDtypeStruct((N_TOK, D_MODEL), BF16),
        grid=(N_TOK // tm,),
        in_specs=_two_source_specs(tm, prompt_tiles) + [_mod_spec(tm, 0), _mod_spec(tm, 1)],
        out_specs=pl.BlockSpec((tm, D_MODEL), lambda i: (i, 0)),
        compiler_params=_params(("parallel",), 40),
        name="modulate",
    )(x_prompt, x_sample, mods, mods)


def _layer_norm_rows(y, g, b):
    mu = jnp.mean(y, axis=-1, keepdims=True)
    yc = y - mu
    var = jnp.mean(yc * yc, axis=-1, keepdims=True)
    return yc * lax.rsqrt(var + LN_EPS) * g + b


def _post_norm_kernel(x_ref, d_ref, gate_ref, g_ref, b_ref, xo_ref):
    y = DEEPNORM_ALPHA * x_ref[...] + gate_ref[...] * d_ref[...].astype(F32)
    xo_ref[...] = _layer_norm_rows(y, g_ref[...], b_ref[...])


def _post_norm_mod_kernel(x_ref, d_ref, gate_ref, g_ref, b_ref, sh_ref, sc_ref, xo_ref, uo_ref):
    y = DEEPNORM_ALPHA * x_ref[...] + gate_ref[...] * d_ref[...].astype(F32)
    xn = _layer_norm_rows(y, g_ref[...], b_ref[...])
    xo_ref[...] = xn
    uo_ref[...] = (xn * (1.0 + sc_ref[...]) + sh_ref[...]).astype(uo_ref.dtype)


def _post_norm_mod_groups_kernel(xp_ref, xs_ref, *rest, prompt_tiles):
    _for_token_group(prompt_tiles, xp_ref, xs_ref, lambda x_ref: _post_norm_mod_kernel(x_ref, *rest))


def post_norm(x, delta, mods, gate_idx, g, b, next_mods=None, next_idx=None, rows=(0, N_TOK), x_sample=None):
    tm = 512
    row = pl.BlockSpec((tm, D_MODEL), lambda i: (i, 0))
    vec = pl.BlockSpec((1, D_MODEL), lambda i: (0, 0))
    tail = [delta, mods, g.reshape(1, D_MODEL), b.reshape(1, D_MODEL)]
    tail_specs = [row, _mod_spec(tm, gate_idx), vec, vec]
    if x_sample is not None:
        prompt_tiles = N_PROMPT // tm
        tail += [next_mods, next_mods]
        tail_specs += [_mod_spec(tm, next_idx), _mod_spec(tm, next_idx + 1)]
        return pl.pallas_call(
            functools.partial(_post_norm_mod_groups_kernel, prompt_tiles=prompt_tiles),
            out_shape=(jax.ShapeDtypeStruct((N_TOK, D_MODEL), F32), jax.ShapeDtypeStruct((N_TOK, D_MODEL), BF16)),
            grid=(N_TOK // tm,), in_specs=_two_source_specs(tm, prompt_tiles) + tail_specs, out_specs=(row, row),
            compiler_params=_params(("parallel",), 48), name="post_norm_mod",
        )(x, x_sample, *tail)
    ins = [x] + tail
    specs = [row] + tail_specs
    if next_mods is None:
        i0 = rows[0] // tm
        row_in = pl.BlockSpec((tm, D_MODEL), lambda i: (i0 + i, 0))
        gate = pl.BlockSpec((None, None, 1, D_MODEL), lambda i: (_group_of_tile(i0 + i, tm), gate_idx, 0, 0))
        return pl.pallas_call(
            _post_norm_kernel,
            out_shape=jax.ShapeDtypeStruct((rows[1], D_MODEL), F32),
            grid=(rows[1] // tm,), in_specs=[row_in, row_in, gate, vec, vec], out_specs=row,
            compiler_params=_params(("parallel",), 40), name="post_norm",
        )(*ins)
    ins += [next_mods, next_mods]
    specs += [_mod_spec(tm, next_idx), _mod_spec(tm, next_idx + 1)]
    return pl.pallas_call(
        _post_norm_mod_kernel,
        out_shape=(jax.ShapeDtypeStruct((N_TOK, D_MODEL), F32), jax.ShapeDtypeStruct((N_TOK, D_MODEL), BF16)),
        grid=(N_TOK // tm,), in_specs=specs, out_specs=(row, row),
        compiler_params=_params(("parallel",), 40), name="post_norm_mod",
    )(*ins)


def _mm_kernel(x_ref, w_ref, o_ref):
    o_ref[...] = jnp.dot(x_ref[...], w_ref[...].astype(BF16), preferred_element_type=F32).astype(o_ref.dtype)


def _mm_resident_kernel(*refs, prompt_tiles):
    if prompt_tiles is None:
        x_ref, w_ref, o_ref, wb_ref = refs
    else:
        x_ref, xs_ref, w_ref, o_ref, wb_ref = refs
    i = pl.program_id(0)

    @pl.when(i == 0)
    def _():
        wb_ref[...] = w_ref[...].astype(BF16)

    if prompt_tiles is None:
        o_ref[...] = jnp.dot(x_ref[...], wb_ref[...], preferred_element_type=F32).astype(o_ref.dtype)
    else:
        @pl.when(i < prompt_tiles)
        def _():
            o_ref[...] = jnp.dot(x_ref[...], wb_ref[...], preferred_element_type=F32).astype(o_ref.dtype)

        @pl.when(i >= prompt_tiles)
        def _():
            o_ref[...] = jnp.dot(xs_ref[...], wb_ref[...], preferred_element_type=F32).astype(o_ref.dtype)


def matmul_resident(x, w, out_dtype, name, x_sample=None, layer=None, col0=0, n=None):
    tm = 512
    k = x.shape[1]
    n = w.shape[-1] - col0 if n is None else n
    assert col0 % n == 0
    j0 = col0 // n
    rows = x.shape[0] + (0 if x_sample is None else x_sample.shape[0])
    once = pl.Buffered(1)
    if layer is None:
        w_spec = pl.BlockSpec((k, n), lambda i: (0, j0), pipeline_mode=once)
    else:
        w_spec = pl.BlockSpec((None, k, n), lambda i: (layer, 0, j0), pipeline_mode=once)
    if x_sample is None:
        prompt_tiles = None
        xs, x_specs = (x,), [pl.BlockSpec((tm, k), lambda i: (i, 0))]
    else:
        prompt_tiles = x.shape[0] // tm
        xs = (x, x_sample)
        x_specs = [pl.BlockSpec((tm, k), lambda i: (jnp.minimum(i, prompt_tiles - 1), 0)),
                   pl.BlockSpec((tm, k), lambda i: (jnp.maximum(i - prompt_tiles, 0), 0))]
    return pl.pallas_call(
        functools.partial(_mm_resident_kernel, prompt_tiles=prompt_tiles),
        out_shape=jax.ShapeDtypeStruct((rows, n), out_dtype),
        grid=(rows // tm,),
        in_specs=x_specs + [w_spec],
        out_specs=pl.BlockSpec((tm, n), lambda i: (i, 0)),
        scratch_shapes=[pltpu.VMEM((k, n), BF16)],
        compiler_params=_params(("arbitrary",), 48),
        name=name,
    )(*xs, w)


def matmul(x, w, out_dtype, tm=2048, tn=512, name="matmul", layer=None, col0=0, n=None):
    m, k = x.shape
    n = w.shape[-1] - col0 if n is None else n
    tm = min(tm, m)
    tn = min(tn, n)
    assert m % tm == 0 and n % tn == 0 and col0 % tn == 0, (m, n, tm, tn, col0)
    j0 = col0 // tn
    if layer is None:
        w_spec = pl.BlockSpec((k, tn), lambda i, j: (0, j0 + j))
    else:
        w_spec = pl.BlockSpec((None, k, tn), lambda i, j: (layer, 0, j0 + j))
    return pl.pallas_call(
        _mm_kernel,
        out_shape=jax.ShapeDtypeStruct((m, n), out_dtype),
        grid=(m // tm, n // tn),
        in_specs=[pl.BlockSpec((tm, k), lambda i, j: (i, 0)), w_spec],
        out_specs=pl.BlockSpec((tm, tn), lambda i, j: (i, j)),
        compiler_params=_params(("parallel", "parallel"), 48),
        name=name,
    )(x, w)


def _rms_rows(y, g):
    return y * lax.rsqrt(jnp.mean(y * y, axis=-1, keepdims=True) + RMS_EPS) * g


def _mla_down_kernel(x_ref, w_ref, qn_ref, kvn_ref, cq_ref, ckv_ref, kr_ref):
    y = jnp.dot(x_ref[...], w_ref[...].astype(BF16), preferred_element_type=F32)
    cq_ref[...] = _rms_rows(y[:, :MLA_Q_LORA], qn_ref[...]).astype(cq_ref.dtype)
    ckv_ref[...] = _rms_rows(y[:, MLA_Q_LORA:MLA_Q_LORA + MLA_KV_LORA], kvn_ref[...])
    kr_ref[...] = y[:, MLA_Q_LORA + MLA_KV_LORA:MLA_Q_LORA + MLA_KV_LORA + MLA_ROPE]


def mla_down(u, w_cat, q_norm, kv_norm):
    tm = 512
    n = w_cat.shape[1]
    return pl.pallas_call(
        _mla_down_kernel,
        out_shape=(jax.ShapeDtypeStruct((N_TOK, MLA_Q_LORA), BF16),
                   jax.ShapeDtypeStruct((N_TOK, MLA_KV_LORA), F32),
                   jax.ShapeDtypeStruct((N_TOK, MLA_ROPE), F32)),
        grid=(N_TOK // tm,),
        in_specs=[pl.BlockSpec((tm, D_MODEL), lambda i: (i, 0)),
                  pl.BlockSpec((D_MODEL, n), lambda i: (0, 0)),
                  pl.BlockSpec((1, MLA_Q_LORA), lambda i: (0, 0)),
                  pl.BlockSpec((1, MLA_KV_LORA), lambda i: (0, 0))],
        out_specs=(pl.BlockSpec((tm, MLA_Q_LORA), lambda i: (i, 0)),
                   pl.BlockSpec((tm, MLA_KV_LORA), lambda i: (i, 0)),
                   pl.BlockSpec((tm, MLA_ROPE), lambda i: (i, 0))),
        compiler_params=_params(("parallel",), 48),
        name="mla_down",
    )(u, w_cat, q_norm.reshape(1, -1), kv_norm.reshape(1, -1))


def _rope_tables():
    t = np.arange(DEC_SEQ)
    quarter = MLA_ROPE // 4
    inv_freq = ROPE_BASE ** (-np.arange(quarter, dtype=np.float64) / quarter)
    ang_row = (t // GRID_W)[:, None] * inv_freq[None, :]
    ang_col = (t % GRID_W)[:, None] * inv_freq[None, :]
    cos = np.concatenate([np.cos(ang_row)] * 2 + [np.cos(ang_col)] * 2, -1)
    sin = np.concatenate([-np.sin(ang_row), np.sin(ang_row), -np.sin(ang_col), np.sin(ang_col)], -1)
    return np.tile(cos, (1, 2)), np.tile(sin, (1, 2))


def _rope_kernel(x_ref, c_ref, s_ref, o_ref):
    x = x_ref[...].astype(F32)
    w = x.shape[-1]
    reps = w // c_ref.shape[-1]
    cos = jnp.tile(c_ref[...], (1, reps))
    sin = jnp.tile(s_ref[...], (1, reps))
    lane = lax.broadcasted_iota(jnp.int32, x.shape, 1)
    partner = jnp.where((lane % 32) < 16, pltpu.roll(x, w - 16, 1), pltpu.roll(x, 16, 1))
    o_ref[...] = (x * cos + partner * sin).astype(o_ref.dtype)


def rope_rows(x, row_block0, col_block, width, n_rows, cos, sin, out_dtype, tr=256):
    t_blocks = cos.shape[0] // tr
    return pl.pallas_call(
        _rope_kernel,
        out_shape=jax.ShapeDtypeStruct((n_rows, width), out_dtype),
        grid=(n_rows // tr,),
        in_specs=[pl.BlockSpec((tr, width), lambda i: (row_block0 + i, col_block)),
                  pl.BlockSpec((tr, 128), lambda i: (i % t_blocks, 0)),
                  pl.BlockSpec((tr, 128), lambda i: (i % t_blocks, 0))],
        out_specs=pl.BlockSpec((tr, width), lambda i: (i, 0)),
        compiler_params=_params(("parallel",), 32),
        name="axial_rope",
    )(x, cos, sin)


def _softmax_parts(parts, sink):
    m = parts[0].max(axis=-1, keepdims=True)
    for s in parts[1:]:
        m = jnp.maximum(m, s.max(axis=-1, keepdims=True))
    if sink is not None:
        m = jnp.maximum(m, sink)
    ps = [jnp.exp(s - m) for s in parts]
    l = ps[0].sum(axis=-1, keepdims=True)
    for p in ps[1:]:
        l = l + p.sum(axis=-1, keepdims=True)
    if sink is not None:
        l = l + jnp.exp(sink - m)
    return ps, 1.0 / l


def _mla_attn_kernel(*refs, with_ctx):
    if with_ctx:
        qn_ref, qr_ref, kn_ref, v_ref, kr_ref, kn2_ref, v2_ref, kr2_ref, o_ref = refs
    else:
        qn_ref, qr_ref, kn_ref, v_ref, kr_ref, o_ref = refs
    kr = kr_ref[...].astype(BF16)
    kr2 = kr2_ref[...].astype(BF16) if with_ctx else None
    for h in range(MLA_HEADS):
        n0, n1 = h * MLA_NOPE, (h + 1) * MLA_NOPE
        qn = qn_ref[:, n0:n1]
        qr = qr_ref[:, h * MLA_ROPE:(h + 1) * MLA_ROPE]
        s = (lax.dot_general(qn, kn_ref[:, n0:n1], NT_DIMS, preferred_element_type=F32)
             + lax.dot_general(qr, kr, NT_DIMS, preferred_element_type=F32)) * MLA_SCALE
        parts = [s]
        if with_ctx:
            s2 = (lax.dot_general(qn, kn2_ref[:, n0:n1], NT_DIMS, preferred_element_type=F32)
                  + lax.dot_general(qr, kr2, NT_DIMS, preferred_element_type=F32)) * MLA_SCALE
            parts.append(s2)
        ps, inv_l = _softmax_parts(parts, None)
        o = jnp.dot(ps[0].astype(BF16), v_ref[:, n0:n1], preferred_element_type=F32)
        if with_ctx:
            o = o + jnp.dot(ps[1].astype(BF16), v2_ref[:, n0:n1], preferred_element_type=F32)
        o_ref[:, n0:n1] = (o * inv_l).astype(o_ref.dtype)


def mla_attention_prompt(q, kv_up, kr):
    w = MLA_HEADS * MLA_NOPE
    return pl.pallas_call(
        functools.partial(_mla_attn_kernel, with_ctx=False),
        out_shape=jax.ShapeDtypeStruct((N_PROMPT, w), BF16),
        grid=(BATCH,),
        in_specs=[pl.BlockSpec((SEQ, w), lambda b: (b, 0)),
                  pl.BlockSpec((SEQ, MLA_HEADS * MLA_ROPE), lambda b: (b, w // (MLA_HEADS * MLA_ROPE))),
                  pl.BlockSpec((SEQ, w), lambda b: (b, 0)),
                  pl.BlockSpec((SEQ, w), lambda b: (b, 1)),
                  pl.BlockSpec((SEQ, MLA_ROPE), lambda b: (b, 0))],
        out_specs=pl.BlockSpec((SEQ, w), lambda b: (b, 0)),
        compiler_params=_params(("parallel",), 32),
        name="mla_attention_prompt",
    )(q, q, kv_up, kv_up, kr)


def mla_attention_sample(qn, qr_rot, kv_up, kr_rot, kr_cache):
    w = MLA_HEADS * MLA_NOPE
    tq = 256
    nq = DEC_SEQ // tq
    q0 = N_PROMPT // tq
    lat0 = N_PROMPT // DEC_SEQ
    ctx0 = N_TOK // PAST_LEN
    return pl.pallas_call(
        functools.partial(_mla_attn_kernel, with_ctx=True),
        out_shape=jax.ShapeDtypeStruct((N_SAMPLE, w), BF16),
        grid=(DEC_BATCH, nq),
        in_specs=[pl.BlockSpec((tq, w), lambda b, i: (q0 + b * nq + i, 0)),
                  pl.BlockSpec((tq, MLA_HEADS * MLA_ROPE), lambda b, i: (b * nq + i, 0)),
                  pl.BlockSpec((DEC_SEQ, w), lambda b, i: (lat0 + b, 0)),
                  pl.BlockSpec((DEC_SEQ, w), lambda b, i: (lat0 + b, 1)),
                  pl.BlockSpec((DEC_SEQ, MLA_ROPE), lambda b, i: (b, 0)),
                  pl.BlockSpec((PAST_LEN, w), lambda b, i: (ctx0 + b, 0)),
                  pl.BlockSpec((PAST_LEN, w), lambda b, i: (ctx0 + b, 1)),
                  pl.BlockSpec((PAST_LEN, MLA_ROPE), lambda b, i: (b, 0))],
        out_specs=pl.BlockSpec((tq, w), lambda b, i: (b * nq + i, 0)),
        compiler_params=_params(("parallel", "parallel"), 56),
        name="mla_attention_sample",
    )(qn, qr_rot, kv_up, kv_up, kr_rot, kv_up, kv_up, kr_cache)


def _gqa_attn_kernel(*refs, with_ctx, tq):
    if with_ctx:
        sink_ref, q_ref, k_ref, v_ref, k2_ref, v2_ref, o_ref = refs
    else:
        sink_ref, q_ref, k_ref, v_ref, o_ref = refs
    group = GQA_HEADS // GQA_KV_HEADS
    d = GQA_HEAD_DIM
    if with_ctx:
        t = pl.program_id(1) * tq + lax.broadcasted_iota(jnp.int32, (tq, DEC_SEQ), 0)
        s_pos = lax.broadcasted_iota(jnp.int32, (tq, DEC_SEQ), 1)
        in_window = jnp.abs(t - s_pos) <= WINDOW
    for hk in range(GQA_KV_HEADS):
        k = k_ref[:, hk * d:(hk + 1) * d].astype(BF16)
        v = v_ref[:, hk * d:(hk + 1) * d].astype(BF16)
        if with_ctx:
            k2 = k2_ref[:, hk * d:(hk + 1) * d].astype(BF16)
            v2 = v2_ref[:, hk * d:(hk + 1) * d].astype(BF16)
        for g in range(group):
            h = hk * group + g
            q = q_ref[:, h * d:(h + 1) * d]
            s = lax.dot_general(q, k, NT_DIMS, preferred_element_type=F32) * GQA_SCALE
            parts = [s]
            if with_ctx:
                parts = [jnp.where(in_window, s, NEG_INF),
                         lax.dot_general(q, k2, NT_DIMS, preferred_element_type=F32) * GQA_SCALE]
            ps, inv_l = _softmax_parts(parts, sink_ref[h])
            o = jnp.dot(ps[0].astype(BF16), v, preferred_element_type=F32)
            if with_ctx:
                o = o + jnp.dot(ps[1].astype(BF16), v2, preferred_element_type=F32)
            o_ref[:, h * d:(h + 1) * d] = (o * inv_l).astype(o_ref.dtype)


def gqa_attention_prompt(q, kv, sink):
    wq = GQA_HEADS * GQA_HEAD_DIM
    wk = GQA_KV_HEADS * GQA_HEAD_DIM
    return pl.pallas_call(
        functools.partial(_gqa_attn_kernel, with_ctx=False, tq=SEQ),
        out_shape=jax.ShapeDtypeStruct((N_PROMPT, wq), BF16),
        grid=(BATCH,),
        in_specs=[pl.BlockSpec(memory_space=pltpu.SMEM),
                  pl.BlockSpec((SEQ, wq), lambda b: (b, 0)),
                  pl.BlockSpec((SEQ, wk), lambda b: (b, 0)),
                  pl.BlockSpec((SEQ, wk), lambda b: (b, 1))],
        out_specs=pl.BlockSpec((SEQ, wq), lambda b: (b, 0)),
        compiler_params=_params(("parallel",), 32),
        name="gqa_attention_prompt",
    )(sink, q, kv, kv)


def gqa_attention_sample(q_rot, k_rot, kv, k_cache, v_cache, sink):
    wq = GQA_HEADS * GQA_HEAD_DIM
    wk = GQA_KV_HEADS * GQA_HEAD_DIM
    tq = 128
    nq = DEC_SEQ // tq
    lat0 = N_PROMPT // DEC_SEQ
    return pl.pallas_call(
        functools.partial(_gqa_attn_kernel, with_ctx=True, tq=tq),
        out_shape=jax.ShapeDtypeStruct((N_SAMPLE, wq), BF16),
        grid=(DEC_BATCH, nq),
        in_specs=[pl.BlockSpec(memory_space=pltpu.SMEM),
                  pl.BlockSpec((tq, wq), lambda b, i: (b * nq + i, 0)),
                  pl.BlockSpec((DEC_SEQ, wk), lambda b, i: (b, 0)),
                  pl.BlockSpec((DEC_SEQ, wk), lambda b, i: (lat0 + b, 1)),
                  pl.BlockSpec((PAST_LEN, wk), lambda b, i: (b, 0)),
                  pl.BlockSpec((PAST_LEN, wk), lambda b, i: (b, 0))],
        out_specs=pl.BlockSpec((tq, wq), lambda b, i: (b * nq + i, 0)),
        compiler_params=_params(("parallel", "parallel"), 56),
        name="gqa_attention_sample",
    )(sink, q_rot, k_rot, kv, k_cache, v_cache)


def _dft_tables(n):
    jk = (np.arange(n)[:, None] * np.arange(n)[None, :]) % n
    ang = 2.0 * np.pi * jk / n
    return np.cos(ang) / math.sqrt(n), np.sin(ang) / math.sqrt(n)


def _fnet_kernel(x_ref, ct_ref, st_ref, cc_ref, sc_ref, o_ref):
    x = x_ref[...]
    y1 = jnp.dot(x, cc_ref[...], preferred_element_type=F32).astype(BF16)
    y2 = jnp.dot(x, sc_ref[...], preferred_element_type=F32).astype(BF16)
    o = (jnp.dot(ct_ref[...], y1, preferred_element_type=F32)
         - jnp.dot(st_ref[...], y2, preferred_element_type=F32))
    o_ref[...] = o.astype(o_ref.dtype)


def fourier_real_2d(u, seq_block0, n_seq, t):
    ct, st = (jnp.asarray(a, BF16) for a in _dft_tables(t))
    cc, sc = (jnp.asarray(a, BF16) for a in _dft_tables(FNET_GROUP_DIM))
    gd = FNET_GROUP_DIM
    return pl.pallas_call(
        _fnet_kernel,
        out_shape=jax.ShapeDtypeStruct((n_seq * t, D_MODEL), BF16),
        grid=(n_seq, FNET_GROUPS),
        in_specs=[pl.BlockSpec((t, gd), lambda b, g: (seq_block0 + b, g)),
                  pl.BlockSpec((t, t), lambda b, g: (0, 0)),
                  pl.BlockSpec((t, t), lambda b, g: (0, 0)),
                  pl.BlockSpec((gd, gd), lambda b, g: (0, 0)),
                  pl.BlockSpec((gd, gd), lambda b, g: (0, 0))],
        out_specs=pl.BlockSpec((t, gd), lambda b, g: (b, g)),
        compiler_params=_params(("parallel", "parallel"), 32),
        name="fourier_mix",
    )(u, ct, st, cc, sc)


def _conv_kernel(b_ref, c_ref, h_ref, w_ref, cb_ref, o_ref):
    z = c_ref[...].astype(F32) * h_ref[...].astype(F32)
    t = z.shape[0]
    row = lax.broadcasted_iota(jnp.int32, z.shape, 0)
    z_prev = jnp.where(row == 0, 0.0, pltpu.roll(z, 1, 0))
    z_next = jnp.where(row == t - 1, 0.0, pltpu.roll(z, t - 1, 0))
    conv = z_prev * w_ref[0:1, :] + z * w_ref[1:2, :] + z_next * w_ref[2:3, :] + cb_ref[...]
    o_ref[...] = (b_ref[...].astype(F32) * conv).astype(o_ref.dtype)


def gated_conv(h3, conv_w, conv_b, seq_block0, n_seq, t):
    tn = 512
    nj = D_MODEL // tn
    return pl.pallas_call(
        _conv_kernel,
        out_shape=jax.ShapeDtypeStruct((n_seq * t, D_MODEL), BF16),
        grid=(n_seq, nj),
        in_specs=[pl.BlockSpec((t, tn), lambda s, j: (seq_block0 + s, j)),
                  pl.BlockSpec((t, tn), lambda s, j: (seq_block0 + s, nj + j)),
                  pl.BlockSpec((t, tn), lambda s, j: (seq_block0 + s, 2 * nj + j)),
                  pl.BlockSpec((3, tn), lambda s, j: (0, j)),
                  pl.BlockSpec((1, tn), lambda s, j: (0, j))],
        out_specs=pl.BlockSpec((t, tn), lambda s, j: (s, j)),
        compiler_params=_params(("parallel", "parallel"), 32),
        name="gated_conv",
    )(h3, h3, h3, conv_w, conv_b.reshape(1, D_MODEL))


_N_RANKS = PEER_TOPK + 1
_CAND_PAIRS = [(a, b) for a in range(_N_RANKS) for b in range(_N_RANKS) if (a + 1) * (b + 1) <= _N_RANKS]


_NO_RANK = 127.0


def _top_values_exact(s, k, want_rank):
    n = s.shape[0]
    idx = lax.broadcasted_iota(jnp.int32, s.shape, 0)
    rank = jnp.full(s.shape, _NO_RANK, F32) if want_rank else None
    out = []
    for it in range(k):
        m = jnp.max(s, axis=0, keepdims=True)
        out.append(m)
        first = jnp.min(jnp.where(s == m, idx, n), axis=0, keepdims=True)
        hit = idx == first
        s = jnp.where(hit, -jnp.inf, s)
        if want_rank:
            rank = jnp.where(hit, float(it), rank)
    return out, rank


def _all_equal(x, value):
    return jnp.min(jnp.where(x == value, 1.0, 0.0)) > 0.5


def _top_values_distinct(s, k, n_pad):
    out = []
    for it in range(k):
        m = jnp.max(s, axis=0, keepdims=True)
        out.append(m)
        s = jnp.where(s == m, -jnp.inf, s)
    removed = jnp.sum(jnp.where(s == -jnp.inf, 1.0, 0.0), axis=0, keepdims=True)
    return out, _all_equal(removed, float(k + n_pad))


def _batcher_network(n):
    def merge(lo, hi, r):
        step = r * 2
        if step < hi - lo:
            yield from merge(lo, hi, step)
            yield from merge(lo + r, hi, step)
            yield from [(i, i + r) for i in range(lo + r, hi - r, step)]
        else:
            yield (lo, lo + r)

    def sort(lo, hi):
        if hi - lo >= 1:
            mid = lo + (hi - lo) // 2
            yield from sort(lo, mid)
            yield from sort(mid + 1, hi)
            yield from merge(lo, hi, 1)

    return list(sort(0, n - 1))


_SUBLANES = 8
_KEY_TILES = PEER_N_KEYS // _SUBLANES
_SORT_KEY_TILES = _batcher_network(_KEY_TILES)


def _top_values_sorted(s, k):
    tiles = [s[j * _SUBLANES:(j + 1) * _SUBLANES, :] for j in range(_KEY_TILES)]
    for i, j in _SORT_KEY_TILES:
        tiles[i], tiles[j] = jnp.maximum(tiles[i], tiles[j]), jnp.minimum(tiles[i], tiles[j])
    out = []
    pops = jnp.zeros_like(tiles[0])
    for it in range(k):
        m = jnp.max(tiles[0], axis=0, keepdims=True)
        out.append(m)
        hit = tiles[0] == m
        pops = pops + jnp.where(hit, 1.0, 0.0)
        depth = k - 1 - it
        for j in range(min(depth, _KEY_TILES - 1)):
            tiles[j] = jnp.where(hit, tiles[j + 1], tiles[j])
        if depth >= _KEY_TILES - 1:
            tiles[-1] = jnp.where(hit, -jnp.inf, tiles[-1])
    return out, _all_equal(jnp.sum(pops, axis=0, keepdims=True), float(k))


def _ranks_from_values(s, values):
    rank = jnp.full(s.shape, _NO_RANK, F32)
    for r in reversed(range(len(values))):
        rank = jnp.where(s >= values[r], float(r), rank)
    return rank


def _dup_bf16_words(x):
    bits = pltpu.bitcast(x.astype(BF16).astype(F32), jnp.int32)
    return bits | lax.shift_right_logical(bits, jnp.int32(16))


def _route_head(h, s1, s2, fast, bq_ref, r2_ref, aw_ref, cw_ref):
    tm = s1.shape[1]
    if fast:
        v1, ok1 = _top_values_sorted(s1, _N_RANKS)
        v2, ok2 = _top_values_sorted(s2, _N_RANKS)
        rank2 = _ranks_from_values(s2, v2[:PEER_TOPK])
    else:
        v1, _ = _top_values_exact(s1, _N_RANKS, False)
        v2, rank2 = _top_values_exact(s2, _N_RANKS, True)
    n_pad = (-len(_CAND_PAIRS)) % 8
    n_cand = len(_CAND_PAIRS) + n_pad
    cand_row = lax.broadcasted_iota(jnp.int32, (n_cand, tm), 0)
    cand = jnp.full((n_cand, tm), -jnp.inf, F32)
    for r, (a, b) in enumerate(_CAND_PAIRS):
        cand = jnp.where(cand_row == r, v1[a] + v2[b], cand)
    if fast:
        top, ok3 = _top_values_distinct(cand, _N_RANKS, n_pad)
    else:
        top, _ = _top_values_exact(cand, _N_RANKS, False)
    tau = 0.5 * (top[PEER_TOPK - 1] + top[PEER_TOPK])
    z = jnp.exp(top[0] - top[0])
    for kk in range(1, PEER_TOPK):
        z = z + jnp.exp(top[kk] - top[0])
    inv_z = 1.0 / z
    count = jnp.zeros_like(s1)
    for b in range(PEER_TOPK):
        count = jnp.where(s1 >= tau - v2[b], float(b + 1), count)
    rows = slice(h * PEER_N_KEYS, (h + 1) * PEER_N_KEYS)
    bq_ref[rows, :] = (jnp.exp(s2 - v2[0]) * inv_z).astype(BF16)
    r2_ref[rows, :] = rank2.astype(BF16)
    a_half = 0.5 * jnp.exp(s1 - v1[0])
    aw_ref[:, h, :, :] = _dup_bf16_words(a_half).reshape(PEER_N_KEYS // 8, 8, tm)
    cw_ref[:, h, :, :] = _dup_bf16_words(count).reshape(PEER_N_KEYS // 8, 8, tm)
    return jnp.logical_and(jnp.logical_and(ok1, ok2), ok3) if fast else None


def _route_kernel(q_ref, sk_ref, bq_ref, r2_ref, aw_ref, cw_ref):
    outs = (bq_ref, r2_ref, aw_ref, cw_ref)

    def scores(hc):
        return lax.dot_general(sk_ref[hc].astype(BF16), q_ref[:, hc * PEER_HALF:(hc + 1) * PEER_HALF],
                               NT_DIMS, preferred_element_type=F32)

    no_ties = [_route_head(h, scores(2 * h), scores(2 * h + 1), True, *outs) for h in range(PEER_HEADS)]
    for h in range(PEER_HEADS):
        @pl.when(jnp.logical_not(no_ties[h]))
        def _():
            _route_head(h, scores(2 * h), scores(2 * h + 1), False, *outs)


def peer_route(q, sub_keys, layer):
    tm = 256
    n_tok = q.shape[0]
    sk = sub_keys.reshape(-1, PEER_HEADS * 2, PEER_N_KEYS, PEER_HALF)
    tiles = PEER_N_KEYS // 8
    col_out = jax.ShapeDtypeStruct((PEER_HEADS * PEER_N_KEYS, n_tok), BF16)
    col_spec = pl.BlockSpec((PEER_HEADS * PEER_N_KEYS, tm), lambda i: (0, i))
    row_out = jax.ShapeDtypeStruct((tiles, PEER_HEADS, 8, n_tok), jnp.int32)
    row_spec = pl.BlockSpec((tiles, PEER_HEADS, 8, tm), lambda i: (0, 0, 0, i))
    return pl.pallas_call(
        _route_kernel,
        out_shape=(col_out, col_out, row_out, row_out),
        grid=(n_tok // tm,),
        in_specs=[pl.BlockSpec((tm, PEER_HEADS * 2 * PEER_HALF), lambda i: (i, 0)),
                  pl.BlockSpec((None, PEER_HEADS * 2, PEER_N_KEYS, PEER_HALF), lambda i: (layer, 0, 0, 0))],
        out_specs=(col_spec, col_spec, row_spec, row_spec),
        compiler_params=_params(("parallel",), 32),
        name="peer_route",
    )(q, sk)


PEER_TE = 512
PEER_TM = 1024
PEER_CHUNK = 512
PEER_N_TILES = PEER_N_EXPERTS // PEER_TE


def _bf16_rows(word_row, n_rows):
    return pltpu.bitcast(jnp.broadcast_to(word_row, (n_rows // 2, word_row.shape[1])), BF16)


PEER_LANES = 256
PEER_ROWS = 128


def _peer_gated_slab(k, k0, bq_ref, r2_ref, aw_ref, cw_ref, ht_ref, at_ref):
    for j in range(ht_ref.shape[1] // PEER_LANES):
        lanes = slice(j * PEER_LANES, (j + 1) * PEER_LANES)
        for r in range(PEER_N_KEYS // PEER_ROWS):
            g = [None, None]
            for h in range(PEER_HEADS):
                count = _bf16_rows(cw_ref[0, h, k0 + k:k0 + k + 1, lanes], PEER_ROWS)
                a_half = _bf16_rows(aw_ref[0, h, k0 + k:k0 + k + 1, lanes], PEER_ROWS)
                rows = slice(h * PEER_N_KEYS + r * PEER_ROWS, h * PEER_N_KEYS + (r + 1) * PEER_ROWS)
                bq = bq_ref[rows, lanes]
                contrib = jnp.where(r2_ref[rows, lanes] < count, bq, jnp.zeros_like(bq)) * a_half
                g[h % 2] = contrib if g[h % 2] is None else g[h % 2] + contrib
            krows = slice(k * PEER_N_KEYS + r * PEER_ROWS, k * PEER_N_KEYS + (r + 1) * PEER_ROWS)
            hk = ht_ref[krows, lanes]
            gelu2 = hk * (1.0 + lax.erf(hk * math.sqrt(0.5)))
            at_ref[krows, lanes] = gelu2.astype(BF16) * (g[0] + g[1])


def _peer_step(x_ref, bq_ref, r2_ref, aw_ref, cw_ref, u_ref, v_ref, o_ref, at_ref, ht_write, ht_read, k0):
    keys_per_tile = PEER_TE // PEER_N_KEYS
    keys_per_chunk = PEER_CHUNK // PEER_N_KEYS
    tm = x_ref.shape[0]
    tok = tm // keys_per_tile
    u = u_ref[...].astype(BF16) if ht_write is not None else None
    for k in range(keys_per_tile):
        if ht_read is not None:
            _peer_gated_slab(k, k0, bq_ref, r2_ref, aw_ref, cw_ref, ht_read, at_ref)
        if ht_write is not None:
            ht_write[:, k * tok:(k + 1) * tok] = lax.dot_general(
                u, x_ref[k * tok:(k + 1) * tok, :], NT_DIMS, preferred_element_type=F32)
        if ht_read is not None and (k + 1) % keys_per_chunk == 0:
            chunk = slice((k + 1) * PEER_N_KEYS - PEER_CHUNK, (k + 1) * PEER_N_KEYS)
            o_ref[...] += lax.dot_general(at_ref[chunk, :], v_ref[chunk, :].astype(BF16), TN_DIMS,
                                          preferred_element_type=F32)


def _peer_kernel(x_ref, bq_ref, r2_ref, aw_ref, cw_ref, u_ref, v_ref, o_ref, ht0_ref, ht1_ref, at_ref):
    e = pl.program_id(1)
    last = PEER_N_TILES
    keys_per_tile = PEER_TE // PEER_N_KEYS
    assert 8 // keys_per_tile == 2
    step = functools.partial(_peer_step, x_ref, bq_ref, r2_ref, aw_ref, cw_ref, u_ref, v_ref, o_ref, at_ref)
    even = e % 2 == 0

    @pl.when(e == 0)
    def _():
        o_ref[...] = jnp.zeros_like(o_ref)
        step(ht0_ref, None, None)

    @pl.when(jnp.logical_not(even))
    def _():
        step(ht1_ref, ht0_ref, 0)

    @pl.when(even & (e > 0) & (e < last))
    def _():
        step(ht0_ref, ht1_ref, keys_per_tile)

    @pl.when(e == last)
    def _():
        step(None, ht1_ref, keys_per_tile)


def peer_experts(u, bq, r2, aw, cw, exp_u, exp_v, layer):
    tm, te = PEER_TM, PEER_TE
    n_tok = u.shape[0]
    assert PEER_N_TILES % 2 == 0 and n_tok % tm == 0
    tiles_per_row_block = 8 * PEER_N_KEYS // te
    col_spec = pl.BlockSpec((PEER_HEADS * PEER_N_KEYS, tm), lambda i, e: (0, i))
    row_spec = pl.BlockSpec((1, PEER_HEADS, 8, tm),
                            lambda i, e: (jnp.maximum(e - 1, 0) // tiles_per_row_block, 0, 0, i))
    return pl.pallas_call(
        _peer_kernel,
        out_shape=jax.ShapeDtypeStruct((n_tok, D_MODEL), F32),
        grid=(n_tok // tm, PEER_N_TILES + 1),
        in_specs=[pl.BlockSpec((tm, D_MODEL), lambda i, e: (i, 0)),
                  col_spec, col_spec,
                  row_spec, row_spec,
                  pl.BlockSpec((None, te, D_MODEL), lambda i, e: (layer, jnp.minimum(e, PEER_N_TILES - 1), 0)),
                  pl.BlockSpec((None, te, D_MODEL), lambda i, e: (layer, jnp.maximum(e - 1, 0), 0))],
        out_specs=pl.BlockSpec((tm, D_MODEL), lambda i, e: (i, 0)),
        scratch_shapes=[pltpu.VMEM((te, tm), F32), pltpu.VMEM((te, tm), F32), pltpu.VMEM((te, tm), BF16)],
        compiler_params=_params(("parallel", "arbitrary"), 60),
        name="peer_experts",
    )(u, bq, r2, aw, cw, exp_u, exp_v)


def peer_ffn(u, w_q, sub_keys, exp_u, exp_v, layer):
    q = matmul_resident(u, w_q, BF16, "peer_query", layer=layer)
    bq, r2, aw, cw = peer_route(q, sub_keys, layer)
    return peer_experts(u, bq, r2, aw, cw, exp_u, exp_v, layer)


def mla_mixer(u, cache_ckv, cache_krope, w_dq, q_norm, w_uq, w_dkv, kv_norm, w_uk, w_uv, w_o):
    pad = (-(MLA_Q_LORA + MLA_KV_LORA + MLA_ROPE)) % 128
    w_cat = jnp.concatenate([w_dq, w_dkv, jnp.zeros((D_MODEL, pad), F32)], axis=1)
    cq, ckv, kr = mla_down(u, w_cat, q_norm, kv_norm)
    w_uq3 = w_uq.reshape(MLA_Q_LORA, MLA_HEADS, MLA_NOPE + MLA_ROPE)
    w_q_cat = jnp.concatenate([w_uq3[:, :, :MLA_NOPE].reshape(MLA_Q_LORA, -1),
                               w_uq3[:, :, MLA_NOPE:].reshape(MLA_Q_LORA, -1)], axis=1)
    q = matmul_resident(cq, w_q_cat, BF16, "mla_q_up")
    w_nope = MLA_HEADS * MLA_NOPE
    ckv_src = jnp.concatenate([ckv, cache_ckv.reshape(-1, MLA_KV_LORA)], axis=0).astype(BF16)
    kv_up = matmul_resident(ckv_src, jnp.concatenate([w_uk, w_uv], axis=1), BF16, "mla_kv_up")
    cos, sin = (jnp.asarray(t, F32) for t in _rope_tables())
    w_rope = MLA_HEADS * MLA_ROPE
    qr_rot = rope_rows(q, N_PROMPT // 256, w_nope // w_rope, w_rope, N_SAMPLE, cos, sin, BF16)
    cos2, sin2 = (t.reshape(DEC_SEQ // 2, 128) for t in (cos[:, :MLA_ROPE], sin[:, :MLA_ROPE]))
    kr_pairs = kr[N_PROMPT:].reshape(N_SAMPLE // 2, 2 * MLA_ROPE)
    kr_rot = rope_rows(kr_pairs, 0, 0, 2 * MLA_ROPE, N_SAMPLE // 2, cos2, sin2, BF16).reshape(N_SAMPLE, MLA_ROPE)
    o_p = mla_attention_prompt(q, kv_up, kr)
    o_s = mla_attention_sample(q, qr_rot, kv_up, kr_rot, cache_krope.reshape(-1, MLA_ROPE))
    op = matmul_resident(o_p, w_o, BF16, "mla_out", x_sample=o_s)
    return op, ckv, kr


def gqa_mixer(u, cache_k, cache_v, w_qkv, sink, w_o):
    nq = GQA_HEADS * GQA_HEAD_DIM
    nk = GQA_KV_HEADS * GQA_HEAD_DIM
    q = matmul_resident(u, w_qkv, BF16, "gqa_q", n=nq)
    kv = matmul_resident(u, w_qkv, F32, "gqa_kv", col0=nq)
    cos, sin = (jnp.asarray(t, F32) for t in _rope_tables())
    q_rot = rope_rows(q, N_PROMPT // 256, 0, nq, N_SAMPLE, cos, sin, BF16)
    k_rot = rope_rows(kv, N_PROMPT // 256, 0, nk, N_SAMPLE, cos, sin, BF16)
    o_p = gqa_attention_prompt(q, kv, sink)
    o_s = gqa_attention_sample(q_rot, k_rot, kv, cache_k.reshape(-1, nk), cache_v.reshape(-1, nk), sink)
    op = matmul_resident(o_p, w_o, BF16, "gqa_out", x_sample=o_s)
    return op, kv


def fnet_mixer(u, w_out):
    f_p = fourier_real_2d(u, 0, BATCH, SEQ)
    f_s = fourier_real_2d(u, N_PROMPT // DEC_SEQ, DEC_BATCH, DEC_SEQ)
    return matmul_resident(f_p, w_out, BF16, "fnet_out", x_sample=f_s)


def conv_mixer(u, w_in, conv_w, conv_b, w_out):
    h3 = matmul(u, w_in, BF16, name="conv_in")
    y_p = gated_conv(h3, conv_w, conv_b, 0, BATCH, SEQ)
    y_s = gated_conv(h3, conv_w, conv_b, N_PROMPT // DEC_SEQ, DEC_BATCH, DEC_SEQ)
    return matmul_resident(y_p, w_out, BF16, "conv_out", x_sample=y_s)


def kernel(x_prompt, x_sample, cache_l0_ckv, cache_l0_krope, cache_l1_k, cache_l1_v, c, c_ctx, ada_w, ada_b, ln1_g, ln1_b, ln2_g, ln2_b, mla_w_dq, mla_q_norm, mla_w_uq, mla_w_dkv, mla_kv_norm, mla_w_uk, mla_w_uv, mla_w_o, gqa_w_qkv, gqa_sink, gqa_w_o, fnet_w_out, conv_w_in, conv_w, conv_b, conv_w_out, peer_w_q, peer_sub_keys, peer_u, peer_v):
    xp = x_prompt.reshape(N_PROMPT, D_MODEL)
    xs = x_sample.reshape(N_SAMPLE, D_MODEL)
    cond = jnp.concatenate([c_ctx[None, :], c, jnp.zeros((N_COND - 1 - DEC_BATCH, D_MODEL), F32)], axis=0)
    mods_all = ada_modulation_all(cond, ada_w, ada_b).reshape(DEPTH, N_COND, 6, 1, D_MODEL)

    u = modulate_tokens(xp, xs, mods_all[0])
    new_ckv = new_krope = new_kv = None
    for i in range(DEPTH):
        mods = mods_all[i]
        if i == 0:
            delta, new_ckv, new_krope = mla_mixer(u, cache_l0_ckv, cache_l0_krope, mla_w_dq, mla_q_norm, mla_w_uq,
                                                  mla_w_dkv, mla_kv_norm, mla_w_uk, mla_w_uv, mla_w_o)
        elif i == 1:
            delta, new_kv = gqa_mixer(u, cache_l1_k, cache_l1_v, gqa_w_qkv, gqa_sink, gqa_w_o)
        elif i == 2:
            delta = fnet_mixer(u, fnet_w_out)
        else:
            delta = conv_mixer(u, conv_w_in, conv_w, conv_b, conv_w_out)
        if i == 0:
            x, u = post_norm(xp, delta, mods, 2, ln1_g[i], ln1_b[i], mods, 3, x_sample=xs)
        else:
            x, u = post_norm(x, delta, mods, 2, ln1_g[i], ln1_b[i], mods, 3)
        delta = peer_ffn(u, peer_w_q, peer_sub_keys, peer_u, peer_v, i)
        if i + 1 < DEPTH:
            x, u = post_norm(x, delta, mods, 5, ln2_g[i], ln2_b[i], mods_all[i + 1], 0)
        else:
            y_prompt = post_norm(x, delta, mods, 5, ln2_g[i], ln2_b[i], rows=(0, N_PROMPT))
            y_sample = post_norm(x, delta, mods, 5, ln2_g[i], ln2_b[i], rows=(N_PROMPT, N_SAMPLE))

    nk = GQA_KV_HEADS * GQA_HEAD_DIM
    return (y_prompt.reshape(BATCH, SEQ, D_MODEL),
            y_sample.reshape(DEC_BATCH, DEC_SEQ, D_MODEL),
            new_ckv[:N_PROMPT].reshape(BATCH, SEQ, MLA_KV_LORA),
            new_krope[:N_PROMPT].reshape(BATCH, SEQ, MLA_ROPE),
            new_kv[:N_PROMPT, :nk].reshape(BATCH, SEQ, GQA_KV_HEADS, GQA_HEAD_DIM),
            new_kv[:N_PROMPT, nk:].reshape(BATCH, SEQ, GQA_KV_HEADS, GQA_HEAD_DIM))
```
